```python
import math
import jax, jax.numpy as jnp
from jax import lax
import numpy as np

D_MODEL = 1024
BATCH = 16
SEQ = 256
DEPTH = 2
DEC_BATCH = 2
DEC_SEQ = 4096
PAST_LEN = 256

GRID_W = 64
EPS = 1e-6
N_MOD = 6
D_MIX = D_MODEL
A_WIDTH = D_MODEL // 4
B_WIDTH = D_MODEL // 2
C_WIDTH = D_MODEL // 4
SSM_CG = 16
SSM_G = A_WIDTH // SSM_CG
SSM_P = 64
N_DIR = 2
MLA_HEADS = 8
MLA_NOPE = 64
MLA_ROPE = 32
MLA_V = B_WIDTH // MLA_HEADS
MLA_QK = MLA_NOPE + MLA_ROPE
Q_RANK = D_MODEL // 4
KV_RANK = D_MODEL // 8
ROPE_BASE = 10000.0
Q_BLOCK = 128
ATTN_SCALE = 1.0 / math.sqrt(MLA_QK)
GMLP_HEADS = 4
GMLP_CH = C_WIDTH // GMLP_HEADS
CHUNK = 128
OFF_SSM = 0
OFF_Q = OFF_SSM + A_WIDTH
OFF_KV = OFF_Q + Q_RANK
OFF_KR = OFF_KV + KV_RANK
OFF_GM = OFF_KR + MLA_ROPE
IN_COLS = OFF_GM + 2 * C_WIDTH
D_FF = 2816

kernel_name = 'hybrid_s5_mla_gmlp_prefix_dit_step'

F32 = jnp.float32


def _rms(x):
    xf = x.astype(F32)
    return xf * lax.rsqrt(jnp.mean(xf * xf, axis=-1, keepdims=True) + EPS)


def rmsnorm(x, g):
    return (_rms(x) * g.astype(F32)).astype(x.dtype)


def modulate(x, shift, scale):
    return (_rms(x) * (1.0 + scale.astype(F32)) + shift.astype(F32)).astype(x.dtype)


def modulation(cond, w_mod, b_mod):
    m = jax.nn.silu(cond) @ w_mod + b_mod
    m = m.reshape(m.shape[0], 1, N_MOD, D_MODEL)
    return tuple(m[:, :, i] for i in range(N_MOD))


def axial_rope_tables(n_tokens):
    rows = n_tokens // GRID_W
    row = jnp.repeat(jnp.arange(rows, dtype=F32), GRID_W)
    col = jnp.tile(jnp.arange(GRID_W, dtype=F32), rows)
    n_freq = MLA_ROPE // 4
    inv = ROPE_BASE ** (-jnp.arange(n_freq, dtype=F32) / n_freq)
    ang = jnp.concatenate([row[:, None] * inv, col[:, None] * inv], axis=-1)
    return jnp.cos(ang), jnp.sin(ang)


def apply_rope(x, cos, sin):
    x1, x2 = jnp.split(x.astype(F32), 2, axis=-1)
    c = cos[None, :, None, :]
    s = sin[None, :, None, :]
    return jnp.concatenate([x1 * c - x2 * s, x2 * c + x1 * s], axis=-1).astype(x.dtype)


def _scan_combine(e1, e2):
    a1, b1 = e1
    a2, b2 = e2
    return a1 * a2, a2 * b1 + b2


def ssm_direction(u, h0, a_re, a_im, b_re, b_im, c_re, c_im, log_dt, reverse):
    A = lax.complex(a_re.astype(F32), a_im.astype(F32))
    Bm = lax.complex(b_re.astype(F32), b_im.astype(F32))
    Cm = lax.complex(c_re.astype(F32), c_im.astype(F32))
    dt = jnp.exp(log_dt.astype(F32))[:, None]
    a_bar = jnp.exp(A * dt)
    b_bar = ((a_bar - 1.0) / A)[..., None] * Bm
    bu = jnp.einsum('gpc,blgc->blgp', b_bar, u.astype(jnp.complex64))
    edge = -1 if reverse else 0
    bu = bu.at[:, edge].add(a_bar[None] * h0)
    a = jnp.broadcast_to(a_bar, bu.shape)
    _, h = lax.associative_scan(_scan_combine, (a, bu), axis=1, reverse=reverse)
    y = jnp.einsum('gcp,blgp->blgc', Cm, h).real
    h_final = h[:, 0] if reverse else h[:, -1]
    return y, h_final


def mixer_ssm(u, h0_re, h0_im, p):
    bn, L, _ = u.shape
    uf = u.astype(F32).reshape(bn, L, SSM_G, SSM_CG)
    h0 = lax.complex(h0_re.astype(F32), h0_im.astype(F32))
    ys, hs = [], []
    for d in range(N_DIR):
        y_d, h_d = ssm_direction(uf, h0[:, d], p['ssm_a_re'][d], p['ssm_a_im'][d], p['ssm_b_re'][d],
                                 p['ssm_b_im'][d], p['ssm_c_re'][d], p['ssm_c_im'][d], p['ssm_log_dt'][d],
                                 reverse=(d == 1))
        ys.append(y_d)
        hs.append(h_d)
    y = (ys[0] + ys[1]).reshape(bn, L, A_WIDTH) + p['ssm_d'].astype(F32) * u.astype(F32)
    y = jax.nn.gelu(y).astype(u.dtype)
    y = y * jax.nn.sigmoid(y @ p['ssm_w_glu'])
    h_fin = jnp.stack(hs, axis=1)
    return y, h_fin.real, h_fin.imag


def mla_queries(c_q, p, rope):
    bn, L, _ = c_q.shape
    q = (rmsnorm(c_q, p['q_a_norm']) @ p['w_uq']).reshape(bn, L, MLA_HEADS, MLA_QK)
    q = rmsnorm(q, p['q_norm'])
    if rope is not None:
        q = jnp.concatenate([q[..., :MLA_NOPE], apply_rope(q[..., MLA_NOPE:], *rope)], axis=-1)
    return q


def mla_keys_values(ckv, k_rope, p, rope):
    bn, L, _ = ckv.shape
    kv = (ckv @ p['w_ukv']).reshape(bn, L, MLA_HEADS, MLA_NOPE + MLA_V)
    k_nope, v = kv[..., :MLA_NOPE], kv[..., MLA_NOPE:]
    kr = jnp.broadcast_to(k_rope[:, :, None, :], (bn, L, MLA_HEADS, MLA_ROPE)).astype(k_nope.dtype)
    k = rmsnorm(jnp.concatenate([k_nope, kr], axis=-1), p['k_norm'])
    if rope is not None:
        k = jnp.concatenate([k[..., :MLA_NOPE], apply_rope(k[..., MLA_NOPE:], *rope)], axis=-1)
    return k, v


def block_attention(q, k, v):
    bn, L, H, dk = q.shape
    nb = L // Q_BLOCK
    qb = q.reshape(bn, nb, Q_BLOCK, H, dk).transpose(1, 0, 2, 3, 4)

    def one_block(q_blk):
        s = jnp.einsum('bqhd,bkhd->bhqk', q_blk, k).astype(F32) * ATTN_SCALE
        w = jax.nn.softmax(s, axis=-1).astype(v.dtype)
        return jnp.einsum('bhqk,bkhd->bqhd', w, v)

    out = lax.map(one_block, qb)
    return out.transpose(1, 0, 2, 3, 4).reshape(bn, L, H * MLA_V)


def mixer_gmlp(uv, p):
    bn, L, _ = uv.shape
    u, v = uv[..., :C_WIDTH], uv[..., C_WIDTH:]
    v = rmsnorm(v, p['gmlp_v_norm']).reshape(bn, L // CHUNK, CHUNK, GMLP_HEADS, GMLP_CH)
    mixed = jnp.einsum('hqk,bnkhc->bnqhc', p['gmlp_w_s'], v) + p['gmlp_b_s'].T[None, None, :, :, None]
    return u * mixed.reshape(bn, L, C_WIDTH)


def merge_heads(y_a, y_b, y_c, g, w_out):
    y = jnp.concatenate([_rms(y_a), _rms(y_b), _rms(y_c)], axis=-1) * g.astype(F32)
    return y.astype(y_a.dtype) @ w_out


def conv_ffn(h, p):
    up = h @ p['ffn_w_up']
    L = up.shape[1]
    pad = jnp.pad(up, ((0, 0), (1, 1), (0, 0)))
    w = p['ffn_conv_w']
    conv = pad[:, :L] * w[0] + pad[:, 1:L + 1] * w[1] + pad[:, 2:] * w[2] + p['ffn_conv_b']
    gate, val = conv[..., :D_FF], conv[..., D_FF:]
    return (jax.nn.silu(gate) * val) @ p['ffn_w_down']


def trunk_layer(x, mod, p, ctx):
    shift1, scale1, gate1, shift2, scale2, gate2 = mod
    bn, L, _ = x.shape
    h = modulate(x, shift1, scale1)
    z = h @ p['w_in']
    u_ssm = z[..., OFF_SSM:OFF_Q]
    c_q = z[..., OFF_Q:OFF_KV]
    ckv = rmsnorm(z[..., OFF_KV:OFF_KR], p['kv_a_norm'])
    k_rope = z[..., OFF_KR:OFF_GM]
    uv = z[..., OFF_GM:]
    if ctx is None:
        zeros = jnp.zeros((bn, N_DIR, SSM_G, SSM_P), F32)
        y_a, h_re, h_im = mixer_ssm(u_ssm, zeros, zeros, p)
        q = mla_queries(c_q, p, None)
        k, v = mla_keys_values(ckv, k_rope, p, None)
        new_ctx = (ckv, k_rope, h_re, h_im)
    else:
        ckv_c, kr_c, h0_re, h0_im = ctx
        rope = axial_rope_tables(L)
        y_a, _, _ = mixer_ssm(u_ssm, h0_re, h0_im, p)
        q = mla_queries(c_q, p, rope)
        k_l, v_l = mla_keys_values(ckv, k_rope, p, rope)
        k_c, v_c = mla_keys_values(ckv_c.astype(ckv.dtype), kr_c, p, None)
        k = jnp.concatenate([k_c, k_l], axis=1)
        v = jnp.concatenate([v_c.astype(v_l.dtype), v_l], axis=1)
        new_ctx = None
    y_b = block_attention(q, k, v)
    y_c = mixer_gmlp(uv, p)
    x = x + gate1 * merge_heads(y_a, y_b, y_c, p['w_out_norm'], p['w_out'])
    x = x + gate2 * conv_ffn(modulate(x, shift2, scale2), p)
    return x, new_ctx


def setup_inputs(seed: int = 0) -> dict:
    key = jax.random.key(seed)
    ks = iter(jax.random.split(key, 48))

    def nrm(shape, s):
        return jax.random.normal(next(ks), shape, F32) * s

    def gain(shape):
        return 1.0 + nrm(shape, 0.02)

    n_idx = jnp.arange(SSM_P, dtype=F32)
    sh_a = (DEPTH, N_DIR, SSM_G, SSM_P)
    return {
        'x_prompt': nrm((BATCH, SEQ, D_MODEL), 1.0),
        'x_sample': nrm((DEC_BATCH, DEC_SEQ, D_MODEL), 1.0),
        'cache_ckv': nrm((DEC_BATCH, DEPTH, PAST_LEN, KV_RANK), 1.0),
        'cache_krope': nrm((DEC_BATCH, DEPTH, PAST_LEN, MLA_ROPE), 1.0),
        'state_ssm_re': nrm((DEC_BATCH, DEPTH, N_DIR, SSM_G, SSM_P), 0.3),
        'state_ssm_im': nrm((DEC_BATCH, DEPTH, N_DIR, SSM_G, SSM_P), 0.3),
        'c': nrm((DEC_BATCH, D_MODEL), 1.0),
        'c_ctx': nrm((D_MODEL,), 1.0),
        'w_mod': nrm((DEPTH, D_MODEL, N_MOD * D_MODEL), D_MODEL ** -0.5),
        'b_mod': nrm((DEPTH, N_MOD * D_MODEL), 0.02),
        'w_in': nrm((DEPTH, D_MODEL, IN_COLS), D_MODEL ** -0.5),
        'ssm_a_re': -0.5 + nrm(sh_a, 0.01),
        'ssm_a_im': math.pi * n_idx + nrm(sh_a, 0.01),
        'ssm_b_re': nrm((DEPTH, N_DIR, SSM_G, SSM_P, SSM_CG), (2.0 * SSM_CG) ** -0.5),
        'ssm_b_im': nrm((DEPTH, N_DIR, SSM_G, SSM_P, SSM_CG), (2.0 * SSM_CG) ** -0.5),
        'ssm_c_re': nrm((DEPTH, N_DIR, SSM_G, SSM_CG, SSM_P), (2.0 * SSM_P) ** -0.5),
        'ssm_c_im': nrm((DEPTH, N_DIR, SSM_G, SSM_CG, SSM_P), (2.0 * SSM_P) ** -0.5),
        'ssm_log_dt': jax.random.uniform(next(ks), (DEPTH, N_DIR, SSM_G), F32, math.log(1e-3), math.log(1e-1)),
        'ssm_d': nrm((DEPTH, A_WIDTH), 0.5),
        'ssm_w_glu': nrm((DEPTH, A_WIDTH, A_WIDTH), A_WIDTH ** -0.5),
        'q_a_norm': gain((DEPTH, Q_RANK)),
        'kv_a_norm': gain((DEPTH, KV_RANK)),
        'w_uq': nrm((DEPTH, Q_RANK, MLA_HEADS * MLA_QK), Q_RANK ** -0.5),
        'w_ukv': nrm((DEPTH, KV_RANK, MLA_HEADS * (MLA_NOPE + MLA_V)), KV_RANK ** -0.5),
        'q_norm': gain((DEPTH, MLA_QK)),
        'k_norm': gain((DEPTH, MLA_QK)),
        'gmlp_v_norm': gain((DEPTH, C_WIDTH)),
        'gmlp_w_s': nrm((DEPTH, GMLP_HEADS, CHUNK, CHUNK), CHUNK ** -0.5),
        'gmlp_b_s': 1.0 + nrm((DEPTH, GMLP_HEADS, CHUNK), 0.02),
        'w_out_norm': gain((DEPTH, D_MIX)),
        'w_out': nrm((DEPTH, D_MIX, D_MODEL), D_MIX ** -0.5),
        'ffn_w_up': nrm((DEPTH, D_MODEL, 2 * D_FF), D_MODEL ** -0.5),
        'ffn_conv_w': nrm((DEPTH, 3, 2 * D_FF), 3.0 ** -0.5),
        'ffn_conv_b': nrm((DEPTH, 2 * D_FF), 0.02),
        'ffn_w_down': nrm((DEPTH, D_FF, D_MODEL), D_FF ** -0.5),
    }


def reference(x_prompt, x_sample, cache_ckv, cache_krope, state_ssm_re, state_ssm_im, c, c_ctx,
              w_mod, b_mod, w_in, ssm_a_re, ssm_a_im, ssm_b_re, ssm_b_im, ssm_c_re, ssm_c_im,
              ssm_log_dt, ssm_d, ssm_w_glu, q_a_norm, kv_a_norm, w_uq, w_ukv, q_norm, k_norm,
              gmlp_v_norm, gmlp_w_s, gmlp_b_s, w_out_norm, w_out, ffn_w_up, ffn_conv_w, ffn_conv_b,
              ffn_w_down):
    xp = x_prompt
    xs = x_sample
    ckv_list, kr_list, hre_list, him_list = [], [], [], []
    for l in range(DEPTH):
        p = {
            'w_in': w_in[l], 'ssm_a_re': ssm_a_re[l], 'ssm_a_im': ssm_a_im[l], 'ssm_b_re': ssm_b_re[l],
            'ssm_b_im': ssm_b_im[l], 'ssm_c_re': ssm_c_re[l], 'ssm_c_im': ssm_c_im[l],
            'ssm_log_dt': ssm_log_dt[l], 'ssm_d': ssm_d[l], 'ssm_w_glu': ssm_w_glu[l],
            'q_a_norm': q_a_norm[l], 'kv_a_norm': kv_a_norm[l], 'w_uq': w_uq[l], 'w_ukv': w_ukv[l],
            'q_norm': q_norm[l], 'k_norm': k_norm[l], 'gmlp_v_norm': gmlp_v_norm[l],
            'gmlp_w_s': gmlp_w_s[l], 'gmlp_b_s': gmlp_b_s[l], 'w_out_norm': w_out_norm[l],
            'w_out': w_out[l], 'ffn_w_up': ffn_w_up[l], 'ffn_conv_w': ffn_conv_w[l],
            'ffn_conv_b': ffn_conv_b[l], 'ffn_w_down': ffn_w_down[l],
        }
        mod_ctx = modulation(c_ctx[None, :], w_mod[l], b_mod[l])
        xp, (ckv_l, kr_l, hre_l, him_l) = trunk_layer(xp, mod_ctx, p, None)
        ckv_list.append(ckv_l)
        kr_list.append(kr_l)
        hre_list.append(hre_l)
        him_list.append(him_l)
        mod_lat = modulation(c, w_mod[l], b_mod[l])
        xs, _ = trunk_layer(xs, mod_lat, p,
                            (cache_ckv[:, l], cache_krope[:, l], state_ssm_re[:, l], state_ssm_im[:, l]))
    new_ckv = jnp.stack(ckv_list, axis=1)
    new_krope = jnp.stack(kr_list, axis=1)
    new_ssm_re = jnp.stack(hre_list, axis=1)
    new_ssm_im = jnp.stack(him_list, axis=1)
    return (xp, xs, new_ckv, new_krope, new_ssm_re, new_ssm_im)
```

```python
import functools
import math

import jax
import jax.numpy as jnp
from jax import lax
from jax.experimental import pallas as pl
from jax.experimental.pallas import tpu as pltpu

F32 = jnp.float32
BF16 = jnp.bfloat16

D_MODEL = 1024
BATCH = 16
SEQ = 256
DEPTH = 2
DEC_BATCH = 2
DEC_SEQ = 4096
PAST_LEN = 256
GRID_W = 64
EPS = 1e-6
N_MOD = 6
A_WIDTH = 256
B_WIDTH = 512
C_WIDTH = 256
SSM_CG = 16
SSM_G = 16
SSM_P = 64
N_DIR = 2
MLA_HEADS = 8
MLA_NOPE = 64
MLA_ROPE = 32
MLA_V = 64
MLA_QK = 96
Q_RANK = 256
KV_RANK = 128
ROPE_BASE = 10000.0
ATTN_SCALE = 1.0 / math.sqrt(MLA_QK)
GMLP_HEADS = 4
GMLP_CH = 64
CHUNK = 128
OFF_SSM = 0
OFF_Q = OFF_SSM + A_WIDTH
OFF_KV = OFF_Q + Q_RANK
OFF_KR = OFF_KV + KV_RANK
OFF_GM = OFF_KR + MLA_ROPE
D_FF = 2816

N_PROMPT = BATCH * SEQ
N_SAMPLE = DEC_BATCH * DEC_SEQ
N_TOK = N_PROMPT + N_SAMPLE
SSM_STATE = SSM_G * SSM_P

LANES = 128
HEAD_PAD = LANES
ROPE_LANE0 = MLA_NOPE

PROJ_TILE = 512
SSM_STREAMS = 8
SSM_STEPS = 256
SSM_UNIT = SSM_STREAMS * SSM_STEPS
ATTN_TQ = 256
ATTN_KC = 512
FFN_TILE = 1024
FFN_FC = 256
FFN_HALO = 16
VMEM_LIMIT = 48 * 1024 * 1024


def _mod_row(tile, tile_tokens):
    start = tile * tile_tokens
    return jnp.where(start < N_PROMPT, 0, 1 + (start - N_PROMPT) // DEC_SEQ)


def _rms(x):
    return x * lax.rsqrt(jnp.mean(x * x, axis=-1, keepdims=True) + EPS)


def _sigmoid(x):
    return 1.0 / (1.0 + jnp.exp(-x))


def _gelu_tanh(x):
    return 0.5 * x * (1.0 + jnp.tanh(math.sqrt(2.0 / math.pi) * (x + 0.044715 * (x * x * x))))


def _dot(a, b):
    return jnp.dot(a, b, preferred_element_type=F32)


def _mod_kernel(cond_ref, w_ref, b_ref, o_ref):
    cond = cond_ref[...]
    s = (cond * _sigmoid(cond)).astype(BF16)
    o_ref[0] = _dot(s, w_ref[0].astype(BF16)) + b_ref[0]


def _modulation(cond8, w_mod, b_mod):
    tn = 1536
    n_cols = N_MOD * D_MODEL
    return pl.pallas_call(
        _mod_kernel,
        grid=(DEPTH, n_cols // tn),
        in_specs=[
            pl.BlockSpec((8, D_MODEL), lambda l, j: (0, 0)),
            pl.BlockSpec((1, D_MODEL, tn), lambda l, j: (l, 0, j)),
            pl.BlockSpec((1, 1, tn), lambda l, j: (l, 0, j)),
        ],
        out_specs=pl.BlockSpec((1, 8, tn), lambda l, j: (l, 0, j)),
        out_shape=jax.ShapeDtypeStruct((DEPTH, 8, n_cols), F32),
        compiler_params=pltpu.CompilerParams(vmem_limit_bytes=VMEM_LIMIT),
        name="modulation",
    )(cond8, w_mod, b_mod.reshape(DEPTH, 1, n_cols))


def _write_heads(allh, extra, gain, rope, out_ref):
    for h in range(MLA_HEADS):
        t = allh[:, h * HEAD_PAD:(h + 1) * HEAD_PAD]
        if extra is not None:
            t = t + extra
        ms = jnp.sum(t * t, axis=-1, keepdims=True) * (1.0 / MLA_QK)
        t = t * lax.rsqrt(ms + EPS) * gain
        if rope is not None:
            cos, sin_a, sin_b = rope
            t = (t * cos + pltpu.roll(t, HEAD_PAD - MLA_ROPE // 2, 1) * sin_a
                 + pltpu.roll(t, MLA_ROPE // 2, 1) * sin_b)
        out_ref[h] = t.astype(BF16)


def _proj_kernel(x_ref, mod_ref, cos_ref, sa_ref, sb_ref, w_in_ref, w_kr_ref, qan_ref, kvn_ref,
                 w_uq_ref, w_uk_ref, w_uv_ref, qn_ref, kn_ref,
                 u_ref, gm_ref, ckv_ref, kr_ref, q_ref, k_ref, v_ref):
    x = x_ref[...]
    shift = mod_ref[0, 0:1, :]
    scale = mod_ref[0, 1:2, :]
    h = (_rms(x) * (1.0 + scale) + shift).astype(BF16)
    z = _dot(h, w_in_ref[...])
    kr = _dot(h, w_kr_ref[...])
    u_ref[...] = z[:, 0:A_WIDTH]
    gm_ref[...] = z[:, A_WIDTH + Q_RANK + KV_RANK:]
    kr_ref[...] = kr
    ckv = _rms(z[:, A_WIDTH + Q_RANK:A_WIDTH + Q_RANK + KV_RANK]) * kvn_ref[...]
    ckv_ref[...] = ckv
    rope = (cos_ref[...], sa_ref[...], sb_ref[...])
    cq = (_rms(z[:, A_WIDTH:A_WIDTH + Q_RANK]) * qan_ref[...]).astype(BF16)
    _write_heads(_dot(cq, w_uq_ref[...]), None, qn_ref[...], rope, q_ref)
    ckv_b = ckv.astype(BF16)
    _write_heads(_dot(ckv_b, w_uk_ref[...]), kr, kn_ref[...], rope, k_ref)
    v_ref[...] = _dot(ckv_b, w_uv_ref[...]).astype(BF16)


def _projection(x, mod_l, rope_tabs, wl):
    t = PROJ_TILE
    n_tiles = N_TOK // t
    tok = lambda w: pl.BlockSpec((t, w), lambda i: (i, 0))
    full = lambda a: pl.BlockSpec(a.shape, lambda i: (0,) * a.ndim)
    heads = pl.BlockSpec((MLA_HEADS, t, HEAD_PAD), lambda i: (0, i, 0))
    weights = [wl['w_in_a'], wl['w_in_kr'], wl['q_a_norm'], wl['kv_a_norm'], wl['w_uq'], wl['w_uk'],
               wl['w_uv'], wl['q_norm'], wl['k_norm']]
    return pl.pallas_call(
        _proj_kernel,
        grid=(n_tiles,),
        in_specs=[tok(D_MODEL),
                  pl.BlockSpec((1, N_MOD, D_MODEL), lambda i: (_mod_row(i, t), 0, 0)),
                  tok(HEAD_PAD), tok(HEAD_PAD), tok(HEAD_PAD)] + [full(w) for w in weights],
        out_specs=[tok(A_WIDTH), tok(2 * C_WIDTH), tok(KV_RANK), tok(HEAD_PAD), heads, heads, tok(B_WIDTH)],
        out_shape=[
            jax.ShapeDtypeStruct((N_TOK, A_WIDTH), F32),
            jax.ShapeDtypeStruct((N_TOK, 2 * C_WIDTH), F32),
            jax.ShapeDtypeStruct((N_TOK, KV_RANK), F32),
            jax.ShapeDtypeStruct((N_TOK, HEAD_PAD), F32),
            jax.ShapeDtypeStruct((MLA_HEADS, N_TOK, HEAD_PAD), BF16),
            jax.ShapeDtypeStruct((MLA_HEADS, N_TOK, HEAD_PAD), BF16),
            jax.ShapeDtypeStruct((N_TOK, B_WIDTH), BF16),
        ],
        compiler_params=pltpu.CompilerParams(vmem_limit_bytes=VMEM_LIMIT),
        name="projection",
    )(x, mod_l, *rope_tabs, *weights)


def _ctx_kv_kernel(ckv_ref, kr_ref, w_uk_ref, w_uv_ref, kn_ref, k_ref, v_ref):
    ckv_b = ckv_ref[...].astype(BF16)
    _write_heads(_dot(ckv_b, w_uk_ref[...]), kr_ref[...], kn_ref[...], None, k_ref)
    v_ref[...] = _dot(ckv_b, w_uv_ref[...]).astype(BF16)


def _ctx_kv(ckv, kr128, wl):
    n = ckv.shape[0]
    full = lambda a: pl.BlockSpec(a.shape, lambda i: (0,) * a.ndim)
    weights = [wl['w_uk'], wl['w_uv'], wl['k_norm']]
    return pl.pallas_call(
        _ctx_kv_kernel,
        grid=(1,),
        in_specs=[full(ckv), full(kr128)] + [full(w) for w in weights],
        out_specs=[pl.BlockSpec((MLA_HEADS, n, HEAD_PAD), lambda i: (0, 0, 0)),
                   pl.BlockSpec((n, B_WIDTH), lambda i: (0, 0))],
        out_shape=[jax.ShapeDtypeStruct((MLA_HEADS, n, HEAD_PAD), BF16),
                   jax.ShapeDtypeStruct((n, B_WIDTH), BF16)],
        compiler_params=pltpu.CompilerParams(vmem_limit_bytes=VMEM_LIMIT),
        name="context_kv",
    )(ckv, kr128, *weights)


def _ssm_block(d, k):
    return jnp.where((d == 1) & (k >= 2), k ^ 1, k)


def _lane_tiles_get(ref, rows):
    return jnp.concatenate([ref[c, rows, :] for c in range(ref.shape[0])], axis=-1)


def _lane_tiles_put(ref, rows, val):
    for c in range(ref.shape[0]):
        ref[c, rows, :] = val[:, c * LANES:(c + 1) * LANES]


def _ssm_kernel(u_ref, h0r_ref, h0i_ref, ar_ref, ai_ref, pwr_ref, pwi_ref, bm_ref, cr_ref, ci_ref,
                y_ref, hfr_ref, hfi_ref, sre, sim, hin_r, hin_i, car_r, car_i):
    d = pl.program_id(0)
    k = pl.program_id(1)
    rows = 256

    @pl.when(k == 0)
    def _():
        car_r[...] = jnp.zeros_like(car_r)
        car_i[...] = jnp.zeros_like(car_i)

    for c in range(SSM_UNIT // rows):
        sl = slice(c * rows, (c + 1) * rows)
        bu = _dot(u_ref[sl, :].astype(BF16), bm_ref[0])
        _lane_tiles_put(sre, sl, bu[:, :SSM_STATE])
        _lane_tiles_put(sim, sl, bu[:, SSM_STATE:])

    a_r = jnp.broadcast_to(ar_ref[0], (SSM_STREAMS, SSM_STATE))
    a_i = jnp.broadcast_to(ai_ref[0], (SSM_STREAMS, SSM_STATE))

    def stream_rows(tau):
        return pl.ds(tau, SSM_STREAMS, stride=SSM_STEPS)

    def scan_step(i, carry):
        h_r, h_i = carry
        tau = jnp.where(d == 0, i, SSM_STEPS - 1 - i)
        n_r = a_r * h_r - a_i * h_i + _lane_tiles_get(sre, stream_rows(tau))
        n_i = a_r * h_i + a_i * h_r + _lane_tiles_get(sim, stream_rows(tau))
        _lane_tiles_put(sre, stream_rows(tau), n_r)
        _lane_tiles_put(sim, stream_rows(tau), n_i)
        return n_r, n_i

    zero = jnp.zeros((SSM_STREAMS, SSM_STATE), F32)
    end_r, end_i = lax.fori_loop(0, SSM_STEPS, scan_step, (zero, zero))

    @pl.when(k < 2)
    def _():
        hfr_ref[0] = end_r
        hfi_ref[0] = end_i

    @pl.when(k >= 2)
    def _():
        b = (k - 2) // 2
        first = (k % 2) == 0
        c_r = jnp.where(first, h0r_ref[0, pl.ds(b, 1), :], car_r[...])
        c_i = jnp.where(first, h0i_ref[0, pl.ds(b, 1), :], car_i[...])
        full_r = pwr_ref[0, SSM_STEPS - 1:SSM_STEPS, :]
        full_i = pwi_ref[0, SSM_STEPS - 1:SSM_STEPS, :]

        def chain(order):
            cr, ci = c_r, c_i
            for j in order:
                hin_r[j:j + 1, :] = cr
                hin_i[j:j + 1, :] = ci
                nr = end_r[j:j + 1, :] + full_r * cr - full_i * ci
                ni = end_i[j:j + 1, :] + full_r * ci + full_i * cr
                cr, ci = nr, ni
            car_r[...] = cr
            car_i[...] = ci

        @pl.when(d == 0)
        def _():
            chain(range(SSM_STREAMS))

        @pl.when(d == 1)
        def _():
            chain(range(SSM_STREAMS - 1, -1, -1))

        g_r = hin_r[...]
        g_i = hin_i[...]

        def fix_step(i, carry):
            p = jnp.where(d == 0, i, SSM_STEPS - 1 - i)
            p_r = jnp.broadcast_to(pwr_ref[0, pl.ds(p, 1), :], (SSM_STREAMS, SSM_STATE))
            p_i = jnp.broadcast_to(pwi_ref[0, pl.ds(p, 1), :], (SSM_STREAMS, SSM_STATE))
            sr = stream_rows(i)
            _lane_tiles_put(sre, sr, _lane_tiles_get(sre, sr) + (p_r * g_r - p_i * g_i))
            _lane_tiles_put(sim, sr, _lane_tiles_get(sim, sr) + (p_r * g_i + p_i * g_r))
            return carry

        lax.fori_loop(0, SSM_STEPS, fix_step, 0)

    for c in range(SSM_UNIT // rows):
        sl = slice(c * rows, (c + 1) * rows)
        y_ref[0, sl, :] = (_dot(_lane_tiles_get(sre, sl).astype(BF16), cr_ref[0])
                           + _dot(_lane_tiles_get(sim, sl).astype(BF16), ci_ref[0]))


def _ssm(u, h0r, h0i, tabs):
    n_units = N_TOK // SSM_UNIT
    per_dir = lambda a: pl.BlockSpec((1,) + a.shape[1:], lambda d, k: (d,) + (0,) * (a.ndim - 1))
    return pl.pallas_call(
        _ssm_kernel,
        grid=(N_DIR, n_units),
        in_specs=[pl.BlockSpec((SSM_UNIT, A_WIDTH), lambda d, k: (_ssm_block(d, k), 0)),
                  per_dir(h0r), per_dir(h0i)] + [per_dir(tabs[n]) for n in
                                                 ('a_r', 'a_i', 'pw_r', 'pw_i', 'b_blk', 'c_r', 'c_i')],
        out_specs=[pl.BlockSpec((1, SSM_UNIT, A_WIDTH), lambda d, k: (d, _ssm_block(d, k), 0)),
                   pl.BlockSpec((1, SSM_STREAMS, SSM_STATE), lambda d, k: (d, jnp.minimum(k, 1), 0)),
                   pl.BlockSpec((1, SSM_STREAMS, SSM_STATE), lambda d, k: (d, jnp.minimum(k, 1), 0))],
        out_shape=[jax.ShapeDtypeStruct((N_DIR, N_TOK, A_WIDTH), F32),
                   jax.ShapeDtypeStruct((N_DIR, BATCH, SSM_STATE), F32),
                   jax.ShapeDtypeStruct((N_DIR, BATCH, SSM_STATE), F32)],
        scratch_shapes=[pltpu.VMEM((SSM_STATE // LANES, SSM_UNIT, LANES), F32),
                        pltpu.VMEM((SSM_STATE // LANES, SSM_UNIT, LANES), F32),
                        pltpu.VMEM((SSM_STREAMS, SSM_STATE), F32), pltpu.VMEM((SSM_STREAMS, SSM_STATE), F32),
                        pltpu.VMEM((1, SSM_STATE), F32), pltpu.VMEM((1, SSM_STATE), F32)],
        compiler_params=pltpu.CompilerParams(vmem_limit_bytes=VMEM_LIMIT,
                                             dimension_semantics=("arbitrary", "arbitrary")),
        name="ssm_scan",
    )(u, h0r, h0i, tabs['a_r'], tabs['a_i'], tabs['pw_r'], tabs['pw_i'], tabs['b_blk'], tabs['c_r'],
      tabs['c_i'])


def _attn_kernel(*refs, key_lens):
    n_src = len(key_lens)
    q_ref = refs[0]
    k_refs = refs[1:1 + n_src]
    v_refs = refs[1 + n_src:1 + 2 * n_src]
    o_ref = refs[1 + 2 * n_src]
    s_ref = refs[2 + 2 * n_src]
    chunks = []
    off = 0
    for si, n in enumerate(key_lens):
        for c0 in range(0, n, ATTN_KC):
            cl = min(ATTN_KC, n - c0)
            chunks.append((si, c0, cl, off))
            off += cl
    outs = []
    for hh in range(2):
        q = q_ref[hh]
        m = jnp.full((ATTN_TQ, 1), -jnp.inf, F32)
        for si, c0, cl, o in chunks:
            s = lax.dot_general(q, k_refs[si][hh, c0:c0 + cl, :], (((1,), (1,)), ((), ())),
                                preferred_element_type=F32) * ATTN_SCALE
            s_ref[:, o:o + cl] = s
            m = jnp.maximum(m, jnp.max(s, axis=-1, keepdims=True))
        l = jnp.zeros((ATTN_TQ, 1), F32)
        acc = jnp.zeros((ATTN_TQ, LANES), F32)
        for si, c0, cl, o in chunks:
            p = jnp.exp(s_ref[:, o:o + cl] - m)
            l = l + jnp.sum(p, axis=-1, keepdims=True)
            acc = acc + _dot(p.astype(BF16), v_refs[si][c0:c0 + cl, :])
        outs.append(acc / l)
    lane = lax.broadcasted_iota(jnp.int32, (ATTN_TQ, LANES), 1)
    o_ref[...] = jnp.where(lane < MLA_V, outs[0], outs[1])


def _attention(q, ks, vs, n_seq, seq_len, tok0, key_lens, key_tok0):
    n_qt = seq_len // ATTN_TQ
    q0 = tok0 // ATTN_TQ
    in_specs = [pl.BlockSpec((2, ATTN_TQ, HEAD_PAD), lambda b, hp, qi: (hp, q0 + b * n_qt + qi, 0))]
    for n, t0 in zip(key_lens, key_tok0):
        in_specs.append(pl.BlockSpec((2, n, HEAD_PAD), lambda b, hp, qi, n=n, t0=t0: (hp, t0 // n + b, 0)))
    for n, t0 in zip(key_lens, key_tok0):
        in_specs.append(pl.BlockSpec((n, LANES), lambda b, hp, qi, n=n, t0=t0: (t0 // n + b, hp)))
    return pl.pallas_call(
        functools.partial(_attn_kernel, key_lens=tuple(key_lens)),
        grid=(n_seq, MLA_HEADS // 2, n_qt),
        in_specs=in_specs,
        out_specs=pl.BlockSpec((ATTN_TQ, LANES), lambda b, hp, qi: (b * n_qt + qi, hp)),
        out_shape=jax.ShapeDtypeStruct((n_seq * seq_len, B_WIDTH), F32),
        scratch_shapes=[pltpu.VMEM((ATTN_TQ, sum(key_lens)), F32)],
        compiler_params=pltpu.CompilerParams(vmem_limit_bytes=VMEM_LIMIT,
                                             dimension_semantics=("arbitrary", "arbitrary", "arbitrary")),
        name="attention",
    )(q, *ks, *vs)


def _merge_kernel(x_ref, mod_ref, yssm_ref, u_ref, yb_ref, gm_ref, dskip_ref, w_glu_ref, vn_ref,
                  w_s_ref, bias_ref, g_ref, w_out_ref, x1_ref, h2_ref):
    t = x_ref.shape[0]
    u = u_ref[...]
    y = yssm_ref[0] + yssm_ref[1] + dskip_ref[...] * u
    y = _gelu_tanh(y)
    y_a = y * _sigmoid(_dot(y.astype(BF16), w_glu_ref[...]))
    gm = gm_ref[...]
    vn = (_rms(gm[:, C_WIDTH:]) * vn_ref[...]).astype(BF16)
    lane = lax.broadcasted_iota(jnp.int32, (CHUNK, C_WIDTH), 1)
    mixed = []
    for c in range(t // CHUNK):
        vc = vn[c * CHUNK:(c + 1) * CHUNK, :]
        m = _dot(w_s_ref[0], vc)
        for h in range(1, GMLP_HEADS):
            m = jnp.where(lane >= h * GMLP_CH, _dot(w_s_ref[h], vc), m)
        mixed.append(m + bias_ref[...])
    y_c = gm[:, :C_WIDTH] * jnp.concatenate(mixed, axis=0)
    g = g_ref[...]
    n_a = (_rms(y_a) * g[:, :A_WIDTH]).astype(BF16)
    n_b = (_rms(yb_ref[...]) * g[:, A_WIDTH:A_WIDTH + B_WIDTH]).astype(BF16)
    n_c = (_rms(y_c) * g[:, A_WIDTH + B_WIDTH:]).astype(BF16)
    o = (_dot(n_a, w_out_ref[0:A_WIDTH, :]) + _dot(n_b, w_out_ref[A_WIDTH:A_WIDTH + B_WIDTH, :])
         + _dot(n_c, w_out_ref[A_WIDTH + B_WIDTH:, :]))
    x1 = x_ref[...] + mod_ref[0, 2:3, :] * o
    x1_ref[...] = x1
    h2_ref[...] = (_rms(x1) * (1.0 + mod_ref[0, 4:5, :]) + mod_ref[0, 3:4, :]).astype(BF16)


def _merge(x, mod_l, yssm, u, yb, gm, wl):
    t = PROJ_TILE
    tok = lambda w: pl.BlockSpec((t, w), lambda i: (i, 0))
    full = lambda a: pl.BlockSpec(a.shape, lambda i: (0,) * a.ndim)
    weights = [wl['ssm_d'], wl['w_glu'], wl['gmlp_v_norm'], wl['gmlp_w_s'], wl['gmlp_bias'],
               wl['w_out_norm'], wl['w_out']]
    return pl.pallas_call(
        _merge_kernel,
        grid=(N_TOK // t,),
        in_specs=[tok(D_MODEL),
                  pl.BlockSpec((1, N_MOD, D_MODEL), lambda i: (_mod_row(i, t), 0, 0)),
                  pl.BlockSpec((N_DIR, t, A_WIDTH), lambda i: (0, i, 0)),
                  tok(A_WIDTH), tok(B_WIDTH), tok(2 * C_WIDTH)] + [full(w) for w in weights],
        out_specs=[tok(D_MODEL), tok(D_MODEL)],
        out_shape=[jax.ShapeDtypeStruct((N_TOK, D_MODEL), F32),
                   jax.ShapeDtypeStruct((N_TOK, D_MODEL), BF16)],
        compiler_params=pltpu.CompilerParams(vmem_limit_bytes=VMEM_LIMIT),
        name="merge",
    )(x, mod_l, yssm, u, yb, gm, *weights)


def _ffn_kernel(x1_ref, mod_ref, hp_ref, hm_ref, hn_ref, wg_ref, wv_ref, cwg_ref, cwv_ref, cbg_ref,
                cbv_ref, wd_ref, o_ref, hext, up, acc):
    i = pl.program_id(0)
    j = pl.program_id(1)
    t = FFN_TILE
    halo = FFN_HALO

    @pl.when(j == 0)
    def _():
        hext[0:halo, :] = hp_ref[...]
        hext[halo:halo + t, :] = hm_ref[...]
        hext[halo + t:, :] = hn_ref[...]
        acc[...] = jnp.zeros_like(acc)

    he = hext[...]
    up[:, 0:FFN_FC] = _dot(he, wg_ref[...])
    up[:, FFN_FC:] = _dot(he, wv_ref[...])

    tok = i * t + lax.broadcasted_iota(jnp.int32, (t, 1), 0)
    seq_len = jnp.where(tok < N_PROMPT, SEQ, DEC_SEQ)
    pos = jnp.where(tok < N_PROMPT, tok & (SEQ - 1), (tok - N_PROMPT) & (DEC_SEQ - 1))
    has_prev = (pos != 0).astype(F32)
    has_next = (pos != seq_len - 1).astype(F32)

    def conv(c0, cw_ref, cb_ref):
        prev = up[halo - 1:halo - 1 + t, c0:c0 + FFN_FC] * has_prev
        cur = up[halo:halo + t, c0:c0 + FFN_FC]
        nxt = up[halo + 1:halo + 1 + t, c0:c0 + FFN_FC] * has_next
        return prev * cw_ref[0:1, :] + cur * cw_ref[1:2, :] + nxt * cw_ref[2:3, :] + cb_ref[...]

    gate = conv(0, cwg_ref, cbg_ref)
    val = conv(FFN_FC, cwv_ref, cbv_ref)
    a = (gate * _sigmoid(gate) * val).astype(BF16)
    acc[...] += _dot(a, wd_ref[...])

    @pl.when(j == pl.num_programs(1) - 1)
    def _():
        o_ref[...] = x1_ref[...] + mod_ref[0, 5:6, :] * acc[...]


def _ffn(x1, mod_l, h2, wl):
    t = FFN_TILE
    n_fc = D_FF // FFN_FC
    hb = t // FFN_HALO
    last_halo = N_TOK // FFN_HALO - 1
    return pl.pallas_call(
        _ffn_kernel,
        grid=(N_TOK // t, n_fc),
        in_specs=[pl.BlockSpec((t, D_MODEL), lambda i, j: (i, 0)),
                  pl.BlockSpec((1, N_MOD, D_MODEL), lambda i, j: (_mod_row(i, t), 0, 0)),
                  pl.BlockSpec((FFN_HALO, D_MODEL), lambda i, j: (jnp.maximum(i * hb - 1, 0), 0)),
                  pl.BlockSpec((t, D_MODEL), lambda i, j: (i, 0)),
                  pl.BlockSpec((FFN_HALO, D_MODEL), lambda i, j: (jnp.minimum((i + 1) * hb, last_halo), 0)),
                  pl.BlockSpec((D_MODEL, FFN_FC), lambda i, j: (0, j)),
                  pl.BlockSpec((D_MODEL, FFN_FC), lambda i, j: (0, n_fc + j)),
                  pl.BlockSpec((3, FFN_FC), lambda i, j: (0, j)),
                  pl.BlockSpec((3, FFN_FC), lambda i, j: (0, n_fc + j)),
                  pl.BlockSpec((1, FFN_FC), lambda i, j: (0, j)),
                  pl.BlockSpec((1, FFN_FC), lambda i, j: (0, n_fc + j)),
                  pl.BlockSpec((FFN_FC, D_MODEL), lambda i, j: (j, 0))],
        out_specs=pl.BlockSpec((t, D_MODEL), lambda i, j: (i, 0)),
        out_shape=jax.ShapeDtypeStruct((N_TOK, D_MODEL), F32),
        scratch_shapes=[pltpu.VMEM((t + 2 * FFN_HALO, D_MODEL), BF16),
                        pltpu.VMEM((t + 2 * FFN_HALO, 2 * FFN_FC), F32),
                        pltpu.VMEM((t, D_MODEL), F32)],
        compiler_params=pltpu.CompilerParams(vmem_limit_bytes=VMEM_LIMIT,
                                             dimension_semantics=("arbitrary", "arbitrary")),
        name="conv_ffn",
    )(x1, mod_l, h2, h2, h2, wl['ffn_w_up'], wl['ffn_w_up'], wl['ffn_conv_w'], wl['ffn_conv_w'],
      wl['ffn_conv_b'], wl['ffn_conv_b'], wl['ffn_w_down'])


def _pad_heads(w, head_w, used, offset=0):
    lead = w.shape[:-1]
    w = w.reshape(lead + (MLA_HEADS, head_w))[..., :used]
    pad = [(0, 0)] * (len(lead) + 1) + [(offset, HEAD_PAD - used - offset)]
    return jnp.pad(w, pad).reshape(lead + (MLA_HEADS * HEAD_PAD,))


def _head_gain(g):
    return jnp.pad(g, (0, HEAD_PAD - MLA_QK)).reshape(1, HEAD_PAD)


def _ssm_tables(a_re, a_im, b_re, b_im, c_re, c_im, log_dt):
    a = lax.complex(a_re, a_im)
    dt = jnp.exp(log_dt)[..., None]
    a_bar = jnp.exp(a * dt)
    b_bar = ((a_bar - 1.0) / a)[..., None] * lax.complex(b_re, b_im)
    steps = jnp.arange(1, SSM_STEPS + 1, dtype=F32)[None, :, None, None]
    pw = jnp.exp((a * dt)[:, None] * steps)
    eye = jnp.eye(SSM_G, dtype=F32)
    blk_b = lambda m: jnp.einsum('dgpc,gh->dgchp', m, eye).reshape(N_DIR, A_WIDTH, SSM_STATE)
    blk_c = lambda m: jnp.einsum('dgcp,gh->dgphc', m, eye).reshape(N_DIR, SSM_STATE, A_WIDTH)
    flat = lambda m: m.reshape(N_DIR, 1, SSM_STATE)
    return {
        'a_r': flat(a_bar.real), 'a_i': flat(a_bar.imag),
        'pw_r': pw.real.reshape(N_DIR, SSM_STEPS, SSM_STATE),
        'pw_i': pw.imag.reshape(N_DIR, SSM_STEPS, SSM_STATE),
        'b_blk': jnp.concatenate([blk_b(b_bar.real), blk_b(b_bar.imag)], axis=-1).astype(BF16),
        'c_r': blk_c(c_re).astype(BF16),
        'c_i': blk_c(-c_im).astype(BF16),
    }


def _rope_tables():
    rows = DEC_SEQ // GRID_W
    row = jnp.repeat(jnp.arange(rows, dtype=F32), GRID_W)
    col = jnp.tile(jnp.arange(GRID_W, dtype=F32), rows)
    n_freq = MLA_ROPE // 4
    inv = ROPE_BASE ** (-jnp.arange(n_freq, dtype=F32) / n_freq)
    ang = jnp.concatenate([row[:, None] * inv, col[:, None] * inv], axis=-1)
    cos, sin = jnp.cos(ang), jnp.sin(ang)
    half = MLA_ROPE // 2
    z = jnp.zeros((DEC_SEQ, HEAD_PAD), F32)
    cos_t = jnp.ones((DEC_SEQ, HEAD_PAD), F32)
    cos_t = cos_t.at[:, ROPE_LANE0:ROPE_LANE0 + half].set(cos).at[:, ROPE_LANE0 + half:ROPE_LANE0 + 2 * half].set(cos)
    sin_a = z.at[:, ROPE_LANE0:ROPE_LANE0 + half].set(-sin)
    sin_b = z.at[:, ROPE_LANE0 + half:ROPE_LANE0 + 2 * half].set(sin)
    lat = lambda tb: jnp.tile(tb, (DEC_BATCH, 1))
    ones = jnp.ones((N_PROMPT, HEAD_PAD), F32)
    zeros = jnp.zeros((N_PROMPT, HEAD_PAD), F32)
    return (jnp.concatenate([ones, lat(cos_t)]), jnp.concatenate([zeros, lat(sin_a)]),
            jnp.concatenate([zeros, lat(sin_b)]))


def _layer_weights(l, w_in, ssm_d, ssm_w_glu, q_a_norm, kv_a_norm, w_uq, w_ukv, q_norm, k_norm,
                   gmlp_v_norm, gmlp_w_s, gmlp_b_s, w_out_norm, w_out, ffn_w_up, ffn_conv_w,
                   ffn_conv_b, ffn_w_down):
    wi = w_in[l]
    w_in_a = jnp.concatenate([wi[:, OFF_SSM:OFF_KR], wi[:, OFF_GM:]], axis=1).astype(BF16)
    w_in_kr = jnp.pad(wi[:, OFF_KR:OFF_GM], ((0, 0), (ROPE_LANE0, HEAD_PAD - ROPE_LANE0 - MLA_ROPE))).astype(BF16)
    ukv = w_ukv[l].reshape(KV_RANK, MLA_HEADS, MLA_NOPE + MLA_V)
    return {
        'w_in_a': w_in_a, 'w_in_kr': w_in_kr,
        'q_a_norm': q_a_norm[l].reshape(1, Q_RANK), 'kv_a_norm': kv_a_norm[l].reshape(1, KV_RANK),
        'w_uq': _pad_heads(w_uq[l], MLA_QK, MLA_QK).astype(BF16),
        'w_uk': _pad_heads(ukv[..., :MLA_NOPE].reshape(KV_RANK, -1), MLA_NOPE, MLA_NOPE).astype(BF16),
        'w_uv': ukv[..., MLA_NOPE:].reshape(KV_RANK, B_WIDTH).astype(BF16),
        'q_norm': _head_gain(q_norm[l]), 'k_norm': _head_gain(k_norm[l]),
        'ssm_d': ssm_d[l].reshape(1, A_WIDTH), 'w_glu': ssm_w_glu[l].astype(BF16),
        'gmlp_v_norm': gmlp_v_norm[l].reshape(1, C_WIDTH), 'gmlp_w_s': gmlp_w_s[l].astype(BF16),
        'gmlp_bias': jnp.repeat(gmlp_b_s[l].T, GMLP_CH, axis=1),
        'w_out_norm': w_out_norm[l].reshape(1, D_MODEL), 'w_out': w_out[l].astype(BF16),
        'ffn_w_up': ffn_w_up[l].astype(BF16), 'ffn_conv_w': ffn_conv_w[l],
        'ffn_conv_b': ffn_conv_b[l].reshape(1, 2 * D_FF), 'ffn_w_down': ffn_w_down[l].astype(BF16),
    }


def kernel(x_prompt, x_sample, cache_ckv, cache_krope, state_ssm_re, state_ssm_im, c, c_ctx, w_mod, b_mod, w_in, ssm_a_re, ssm_a_im, ssm_b_re, ssm_b_im, ssm_c_re, ssm_c_im, ssm_log_dt, ssm_d, ssm_w_glu, q_a_norm, kv_a_norm, w_uq, w_ukv, q_norm, k_norm, gmlp_v_norm, gmlp_w_s, gmlp_b_s, w_out_norm, w_out, ffn_w_up, ffn_conv_w, ffn_conv_b, ffn_w_down):
    x = jnp.concatenate([x_prompt.reshape(N_PROMPT, D_MODEL), x_sample.reshape(N_SAMPLE, D_MODEL)], axis=0)
    cond8 = jnp.concatenate([c_ctx[None, :], c, jnp.zeros((8 - 1 - DEC_BATCH, D_MODEL), F32)], axis=0)
    mod = _modulation(cond8, w_mod, b_mod).reshape(DEPTH, 8, N_MOD, D_MODEL)
    rope_tabs = _rope_tables()

    ckv_out, kr_out, hre_out, him_out = [], [], [], []
    for l in range(DEPTH):
        wl = _layer_weights(l, w_in, ssm_d, ssm_w_glu, q_a_norm, kv_a_norm, w_uq, w_ukv, q_norm, k_norm,
                            gmlp_v_norm, gmlp_w_s, gmlp_b_s, w_out_norm, w_out, ffn_w_up, ffn_conv_w,
                            ffn_conv_b, ffn_w_down)
        tabs = _ssm_tables(ssm_a_re[l], ssm_a_im[l], ssm_b_re[l], ssm_b_im[l], ssm_c_re[l], ssm_c_im[l],
                           ssm_log_dt[l])
        mod_l = mod[l]

        u, gm, ckv, kr, q, k, v = _projection(x, mod_l, rope_tabs, wl)

        kr_c = jnp.pad(cache_krope[:, l].reshape(DEC_BATCH * PAST_LEN, MLA_ROPE),
                       ((0, 0), (ROPE_LANE0, HEAD_PAD - ROPE_LANE0 - MLA_ROPE)))
        k_c, v_c = _ctx_kv(cache_ckv[:, l].reshape(DEC_BATCH * PAST_LEN, KV_RANK), kr_c, wl)

        def h0(s):
            s = jnp.transpose(s[:, l].reshape(DEC_BATCH, N_DIR, SSM_STATE), (1, 0, 2))
            return jnp.pad(s, ((0, 0), (0, SSM_STREAMS - DEC_BATCH), (0, 0)))

        yssm, hf_r, hf_i = _ssm(u, h0(state_ssm_re), h0(state_ssm_im), tabs)

        yb_p = _attention(q, [k], [v], BATCH, SEQ, 0, [SEQ], [0])
        yb_s = _attention(q, [k_c, k], [v_c, v], DEC_BATCH, DEC_SEQ, N_PROMPT,
                          [PAST_LEN, DEC_SEQ], [0, N_PROMPT])
        yb = jnp.concatenate([yb_p, yb_s], axis=0)

        x1, h2 = _merge(x, mod_l, yssm, u, yb, gm, wl)
        x = _ffn(x1, mod_l, h2, wl)

        ckv_out.append(ckv[:N_PROMPT].reshape(BATCH, SEQ, KV_RANK))
        kr_out.append(kr[:N_PROMPT, ROPE_LANE0:ROPE_LANE0 + MLA_ROPE].reshape(BATCH, SEQ, MLA_ROPE))
        to_state = lambda s: jnp.transpose(s, (1, 0, 2)).reshape(BATCH, N_DIR, SSM_G, SSM_P)
        hre_out.append(to_state(hf_r))
        him_out.append(to_state(hf_i))

    y_prompt = x[:N_PROMPT].reshape(BATCH, SEQ, D_MODEL)
    y_sample = x[N_PROMPT:].reshape(DEC_BATCH, DEC_SEQ, D_MODEL)
    return (y_prompt, y_sample, jnp.stack(ckv_out, axis=1), jnp.stack(kr_out, axis=1),
            jnp.stack(hre_out, axis=1), jnp.stack(him_out, axis=1))
```

```python
import functools
import math

import jax
import jax.numpy as jnp
from jax import lax
from jax.experimental import pallas as pl
from jax.experimental.pallas import tpu as pltpu

F32 = jnp.float32
BF16 = jnp.bfloat16

D_MODEL = 1024
BATCH = 16
SEQ = 256
DEPTH = 2
DEC_BATCH = 2
DEC_SEQ = 4096
PAST_LEN = 256
GRID_W = 64
EPS = 1e-6
N_MOD = 6
A_WIDTH = 256
B_WIDTH = 512
C_WIDTH = 256
SSM_CG = 16
SSM_G = 16
SSM_P = 64
N_DIR = 2
MLA_HEADS = 8
MLA_NOPE = 64
MLA_ROPE = 32
MLA_V = 64
MLA_QK = 96
Q_RANK = 256
KV_RANK = 128
ROPE_BASE = 10000.0
ATTN_SCALE = 1.0 / math.sqrt(MLA_QK)
GMLP_HEADS = 4
GMLP_CH = 64
CHUNK = 128
OFF_SSM = 0
OFF_Q = OFF_SSM + A_WIDTH
OFF_KV = OFF_Q + Q_RANK
OFF_KR = OFF_KV + KV_RANK
OFF_GM = OFF_KR + MLA_ROPE
D_FF = 2816

N_PROMPT = BATCH * SEQ
N_SAMPLE = DEC_BATCH * DEC_SEQ
N_TOK = N_PROMPT + N_SAMPLE
SSM_STATE = SSM_G * SSM_P

LANES = 128
HEAD_PAD = LANES
ROPE_LANE0 = MLA_NOPE

PROJ_TILE = 512
SSM_STREAMS = 8
SSM_STEPS = 256
SSM_UNIT = SSM_STREAMS * SSM_STEPS
ATTN_TQ = 256
ATTN_KC = 512
FFN_TILE = 1024
FFN_FC = 256
FFN_HALO = 16
VMEM_LIMIT = 48 * 1024 * 1024


def _mod_row(tile, tile_tokens):
    start = tile * tile_tokens
    return jnp.where(start < N_PROMPT, 0, 1 + (start - N_PROMPT) // DEC_SEQ)


def _rms(x):
    return x * lax.rsqrt(jnp.mean(x * x, axis=-1, keepdims=True) + EPS)


def _sigmoid(x):
    return 1.0 / (1.0 + jnp.exp(-x))


def _gelu_tanh(x):
    return 0.5 * x * (1.0 + jnp.tanh(math.sqrt(2.0 / math.pi) * (x + 0.044715 * (x * x * x))))


def _dot(a, b):
    return jnp.dot(a, b, preferred_element_type=F32)


def _mod_kernel(cond_ref, w_ref, b_ref, o_ref):
    cond = cond_ref[...]
    s = (cond * _sigmoid(cond)).astype(BF16)
    o_ref[0] = _dot(s, w_ref[0].astype(BF16)) + b_ref[0]


def _modulation(cond8, w_mod, b_mod):
    tn = 1536
    n_cols = N_MOD * D_MODEL
    return pl.pallas_call(
        _mod_kernel,
        grid=(DEPTH, n_cols // tn),
        in_specs=[
            pl.BlockSpec((8, D_MODEL), lambda l, j: (0, 0)),
            pl.BlockSpec((1, D_MODEL, tn), lambda l, j: (l, 0, j)),
            pl.BlockSpec((1, 1, tn), lambda l, j: (l, 0, j)),
        ],
        out_specs=pl.BlockSpec((1, 8, tn), lambda l, j: (l, 0, j)),
        out_shape=jax.ShapeDtypeStruct((DEPTH, 8, n_cols), F32),
        compiler_params=pltpu.CompilerParams(vmem_limit_bytes=VMEM_LIMIT),
        name="modulation",
    )(cond8, w_mod, b_mod.reshape(DEPTH, 1, n_cols))


def _write_heads(allh, extra, gain, rope, out_ref):
    for h in range(MLA_HEADS):
        t = allh[:, h * HEAD_PAD:(h + 1) * HEAD_PAD]
        if extra is not None:
            t = t + extra
        ms = jnp.sum(t * t, axis=-1, keepdims=True) * (1.0 / MLA_QK)
        t = t * lax.rsqrt(ms + EPS) * gain
        if rope is not None:
            cos, sin_a, sin_b = rope
            t = (t * cos + pltpu.roll(t, HEAD_PAD - MLA_ROPE // 2, 1) * sin_a
                 + pltpu.roll(t, MLA_ROPE // 2, 1) * sin_b)
        out_ref[h] = t.astype(BF16)


def _proj_kernel(x_ref, mod_ref, cos_ref, sa_ref, sb_ref, w_in_ref, w_kr_ref, qan_ref, kvn_ref,
                 w_uq_ref, w_uk_ref, w_uv_ref, qn_ref, kn_ref,
                 u_ref, gm_ref, ckv_ref, kr_ref, q_ref, k_ref, v_ref):
    x = x_ref[...]
    shift = mod_ref[0, 0:1, :]
    scale = mod_ref[0, 1:2, :]
    h = (_rms(x) * (1.0 + scale) + shift).astype(BF16)
    z = _dot(h, w_in_ref[...])
    kr = _dot(h, w_kr_ref[...])
    u_ref[...] = z[:, 0:A_WIDTH]
    gm_ref[...] = z[:, A_WIDTH + Q_RANK + KV_RANK:]
    kr_ref[...] = kr
    ckv = _rms(z[:, A_WIDTH + Q_RANK:A_WIDTH + Q_RANK + KV_RANK]) * kvn_ref[...]
    ckv_ref[...] = ckv
    rope = (cos_ref[...], sa_ref[...], sb_ref[...])
    cq = (_rms(z[:, A_WIDTH:A_WIDTH + Q_RANK]) * qan_ref[...]).astype(BF16)
    _write_heads(_dot(cq, w_uq_ref[...]), None, qn_ref[...], rope, q_ref)
    ckv_b = ckv.astype(BF16)
    _write_heads(_dot(ckv_b, w_uk_ref[...]), kr, kn_ref[...], rope, k_ref)
    v_ref[...] = _dot(ckv_b, w_uv_ref[...]).astype(BF16)


def _projection(x, mod_l, rope_tabs, wl):
    t = PROJ_TILE
    n_tiles = N_TOK // t
    tok = lambda w: pl.BlockSpec((t, w), lambda i: (i, 0))
    full = lambda a: pl.BlockSpec(a.shape, lambda i: (0,) * a.ndim)
    heads = pl.BlockSpec((MLA_HEADS, t, HEAD_PAD), lambda i: (0, i, 0))
    weights = [wl['w_in_a'], wl['w_in_kr'], wl['q_a_norm'], wl['kv_a_norm'], wl['w_uq'], wl['w_uk'],
               wl['w_uv'], wl['q_norm'], wl['k_norm']]
    return pl.pallas_call(
        _proj_kernel,
        grid=(n_tiles,),
        in_specs=[tok(D_MODEL),
                  pl.BlockSpec((1, N_MOD, D_MODEL), lambda i: (_mod_row(i, t), 0, 0)),
                  tok(HEAD_PAD), tok(HEAD_PAD), tok(HEAD_PAD)] + [full(w) for w in weights],
        out_specs=[tok(A_WIDTH), tok(2 * C_WIDTH), tok(KV_RANK), tok(HEAD_PAD), heads, heads, tok(B_WIDTH)],
        out_shape=[
            jax.ShapeDtypeStruct((N_TOK, A_WIDTH), F32),
            jax.ShapeDtypeStruct((N_TOK, 2 * C_WIDTH), F32),
            jax.ShapeDtypeStruct((N_TOK, KV_RANK), F32),
            jax.ShapeDtypeStruct((N_TOK, HEAD_PAD), F32),
            jax.ShapeDtypeStruct((MLA_HEADS, N_TOK, HEAD_PAD), BF16),
            jax.ShapeDtypeStruct((MLA_HEADS, N_TOK, HEAD_PAD), BF16),
            jax.ShapeDtypeStruct((N_TOK, B_WIDTH), BF16),
        ],
        compiler_params=pltpu.CompilerParams(vmem_limit_bytes=VMEM_LIMIT),
        name="projection",
    )(x, mod_l, *rope_tabs, *weights)


def _ctx_kv_kernel(ckv_ref, kr_ref, w_uk_ref, w_uv_ref, kn_ref, k_ref, v_ref):
    ckv_b = ckv_ref[...].astype(BF16)
    _write_heads(_dot(ckv_b, w_uk_ref[...]), kr_ref[...], kn_ref[...], None, k_ref)
    v_ref[...] = _dot(ckv_b, w_uv_ref[...]).astype(BF16)


def _ctx_kv(ckv, kr128, wl):
    n = ckv.shape[0]
    full = lambda a: pl.BlockSpec(a.shape, lambda i: (0,) * a.ndim)
    weights = [wl['w_uk'], wl['w_uv'], wl['k_norm']]
    return pl.pallas_call(
        _ctx_kv_kernel,
        grid=(1,),
        in_specs=[full(ckv), full(kr128)] + [full(w) for w in weights],
        out_specs=[pl.BlockSpec((MLA_HEADS, n, HEAD_PAD), lambda i: (0, 0, 0)),
                   pl.BlockSpec((n, B_WIDTH), lambda i: (0, 0))],
        out_shape=[jax.ShapeDtypeStruct((MLA_HEADS, n, HEAD_PAD), BF16),
                   jax.ShapeDtypeStruct((n, B_WIDTH), BF16)],
        compiler_params=pltpu.CompilerParams(vmem_limit_bytes=VMEM_LIMIT),
        name="context_kv",
    )(ckv, kr128, *weights)


def _ssm_block(d, k):
    return jnp.where((d == 1) & (k >= 2), k ^ 1, k)


def _ssm_kernel(u_ref, h0r_ref, h0i_ref, ar_ref, ai_ref, pwr_ref, pwi_ref, bm_ref, cr_ref, ci_ref,
                y_ref, hfr_ref, hfi_ref, sre, sim, hin_r, hin_i, car_r, car_i):
    d = pl.program_id(0)
    k = pl.program_id(1)
    rows = 256

    @pl.when(k == 0)
    def _():
        car_r[...] = jnp.zeros_like(car_r)
        car_i[...] = jnp.zeros_like(car_i)

    for c in range(SSM_UNIT // rows):
        sl = slice(c * rows, (c + 1) * rows)
        bu = _dot(u_ref[sl, :].astype(BF16), bm_ref[0])
        sre[sl, :] = bu[:, :SSM_STATE]
        sim[sl, :] = bu[:, SSM_STATE:]

    a_r = jnp.broadcast_to(ar_ref[0], (SSM_STREAMS, SSM_STATE))
    a_i = jnp.broadcast_to(ai_ref[0], (SSM_STREAMS, SSM_STATE))

    def stream_rows(tau):
        return pl.ds(pl.multiple_of(tau * SSM_STREAMS, SSM_STREAMS), SSM_STREAMS)

    def scan_step(i, carry):
        h_r, h_i = carry
        tau = jnp.where(d == 0, i, SSM_STEPS - 1 - i)
        n_r = a_r * h_r - a_i * h_i + sre[stream_rows(tau), :]
        n_i = a_r * h_i + a_i * h_r + sim[stream_rows(tau), :]
        sre[stream_rows(tau), :] = n_r
        sim[stream_rows(tau), :] = n_i
        return n_r, n_i

    zero = jnp.zeros((SSM_STREAMS, SSM_STATE), F32)
    end_r, end_i = lax.fori_loop(0, SSM_STEPS, scan_step, (zero, zero))

    @pl.when(k < 2)
    def _():
        hfr_ref[0] = end_r
        hfi_ref[0] = end_i

    @pl.when(k >= 2)
    def _():
        b = (k - 2) // 2
        first = (k % 2) == 0
        c_r = jnp.where(first, h0r_ref[0, pl.ds(b, 1), :], car_r[...])
        c_i = jnp.where(first, h0i_ref[0, pl.ds(b, 1), :], car_i[...])
        full_r = pwr_ref[0, SSM_STEPS - 1:SSM_STEPS, :]
        full_i = pwi_ref[0, SSM_STEPS - 1:SSM_STEPS, :]

        def chain(order):
            cr, ci = c_r, c_i
            for j in order:
                hin_r[j:j + 1, :] = cr
                hin_i[j:j + 1, :] = ci
                nr = end_r[j:j + 1, :] + full_r * cr - full_i * ci
                ni = end_i[j:j + 1, :] + full_r * ci + full_i * cr
                cr, ci = nr, ni
            car_r[...] = cr
            car_i[...] = ci

        @pl.when(d == 0)
        def _():
            chain(range(SSM_STREAMS))

        @pl.when(d == 1)
        def _():
            chain(range(SSM_STREAMS - 1, -1, -1))

        g_r = hin_r[...]
        g_i = hin_i[...]

        def fix_step(i, carry):
            p = jnp.where(d == 0, i, SSM_STEPS - 1 - i)
            p_r = jnp.broadcast_to(pwr_ref[0, pl.ds(p, 1), :], (SSM_STREAMS, SSM_STATE))
            p_i = jnp.broadcast_to(pwi_ref[0, pl.ds(p, 1), :], (SSM_STREAMS, SSM_STATE))
            sr = stream_rows(i)
            sre[sr, :] = sre[sr, :] + (p_r * g_r - p_i * g_i)
            sim[sr, :] = sim[sr, :] + (p_r * g_i + p_i * g_r)
            return carry

        lax.fori_loop(0, SSM_STEPS, fix_step, 0)

    for c in range(SSM_UNIT // rows):
        sl = slice(c * rows, (c + 1) * rows)
        y_ref[0, sl, :] = (_dot(sre[sl, :].astype(BF16), cr_ref[0])
                           + _dot(sim[sl, :].astype(BF16), ci_ref[0]))


def _step_major(x, inverse=False):
    lead = x.shape[:-2]
    a, b = (SSM_STEPS, SSM_STREAMS) if inverse else (SSM_STREAMS, SSM_STEPS)
    x = x.reshape(lead + (N_TOK // SSM_UNIT, a, b, x.shape[-1]))
    return jnp.swapaxes(x, -3, -2).reshape(lead + (N_TOK, x.shape[-1]))


def _ssm(u, h0r, h0i, tabs):
    n_units = N_TOK // SSM_UNIT
    u = _step_major(u)
    per_dir = lambda a: pl.BlockSpec((1,) + a.shape[1:], lambda d, k: (d,) + (0,) * (a.ndim - 1))
    y, hf_r, hf_i = pl.pallas_call(
        _ssm_kernel,
        grid=(N_DIR, n_units),
        in_specs=[pl.BlockSpec((SSM_UNIT, A_WIDTH), lambda d, k: (_ssm_block(d, k), 0)),
                  per_dir(h0r), per_dir(h0i)] + [per_dir(tabs[n]) for n in
                                                 ('a_r', 'a_i', 'pw_r', 'pw_i', 'b_blk', 'c_r', 'c_i')],
        out_specs=[pl.BlockSpec((1, SSM_UNIT, A_WIDTH), lambda d, k: (d, _ssm_block(d, k), 0)),
                   pl.BlockSpec((1, SSM_STREAMS, SSM_STATE), lambda d, k: (d, jnp.minimum(k, 1), 0)),
                   pl.BlockSpec((1, SSM_STREAMS, SSM_STATE), lambda d, k: (d, jnp.minimum(k, 1), 0))],
        out_shape=[jax.ShapeDtypeStruct((N_DIR, N_TOK, A_WIDTH), F32),
                   jax.ShapeDtypeStruct((N_DIR, BATCH, SSM_STATE), F32),
                   jax.ShapeDtypeStruct((N_DIR, BATCH, SSM_STATE), F32)],
        scratch_shapes=[pltpu.VMEM((SSM_UNIT, SSM_STATE), F32), pltpu.VMEM((SSM_UNIT, SSM_STATE), F32),
                        pltpu.VMEM((SSM_STREAMS, SSM_STATE), F32), pltpu.VMEM((SSM_STREAMS, SSM_STATE), F32),
                        pltpu.VMEM((1, SSM_STATE), F32), pltpu.VMEM((1, SSM_STATE), F32)],
        compiler_params=pltpu.CompilerParams(vmem_limit_bytes=VMEM_LIMIT,
                                             dimension_semantics=("arbitrary", "arbitrary")),
        name="ssm_scan",
    )(u, h0r, h0i, tabs['a_r'], tabs['a_i'], tabs['pw_r'], tabs['pw_i'], tabs['b_blk'], tabs['c_r'],
      tabs['c_i'])
    return _step_major(y, inverse=True), hf_r, hf_i


def _attn_kernel(*refs, key_lens):
    n_src = len(key_lens)
    q_ref = refs[0]
    k_refs = refs[1:1 + n_src]
    v_refs = refs[1 + n_src:1 + 2 * n_src]
    o_ref = refs[1 + 2 * n_src]
    s_ref = refs[2 + 2 * n_src]
    chunks = []
    off = 0
    for si, n in enumerate(key_lens):
        for c0 in range(0, n, ATTN_KC):
            cl = min(ATTN_KC, n - c0)
            chunks.append((si, c0, cl, off))
            off += cl
    outs = []
    for hh in range(2):
        q = q_ref[hh]
        m = jnp.full((ATTN_TQ, 1), -jnp.inf, F32)
        for si, c0, cl, o in chunks:
            s = lax.dot_general(q, k_refs[si][hh, c0:c0 + cl, :], (((1,), (1,)), ((), ())),
                                preferred_element_type=F32) * ATTN_SCALE
            s_ref[:, o:o + cl] = s
            m = jnp.maximum(m, jnp.max(s, axis=-1, keepdims=True))
        l = jnp.zeros((ATTN_TQ, 1), F32)
        acc = jnp.zeros((ATTN_TQ, LANES), F32)
        for si, c0, cl, o in chunks:
            p = jnp.exp(s_ref[:, o:o + cl] - m)
            l = l + jnp.sum(p, axis=-1, keepdims=True)
            acc = acc + _dot(p.astype(BF16), v_refs[si][c0:c0 + cl, :])
        outs.append(acc / l)
    lane = lax.broadcasted_iota(jnp.int32, (ATTN_TQ, LANES), 1)
    o_ref[...] = jnp.where(lane < MLA_V, outs[0], outs[1])


def _attention(q, ks, vs, n_seq, seq_len, tok0, key_lens, key_tok0):
    n_qt = seq_len // ATTN_TQ
    q0 = tok0 // ATTN_TQ
    in_specs = [pl.BlockSpec((2, ATTN_TQ, HEAD_PAD), lambda b, hp, qi: (hp, q0 + b * n_qt + qi, 0))]
    for n, t0 in zip(key_lens, key_tok0):
        in_specs.append(pl.BlockSpec((2, n, HEAD_PAD), lambda b, hp, qi, n=n, t0=t0: (hp, t0 // n + b, 0)))
    for n, t0 in zip(key_lens, key_tok0):
        in_specs.append(pl.BlockSpec((n, LANES), lambda b, hp, qi, n=n, t0=t0: (t0 // n + b, hp)))
    return pl.pallas_call(
        functools.partial(_attn_kernel, key_lens=tuple(key_lens)),
        grid=(n_seq, MLA_HEADS // 2, n_qt),
        in_specs=in_specs,
        out_specs=pl.BlockSpec((ATTN_TQ, LANES), lambda b, hp, qi: (b * n_qt + qi, hp)),
        out_shape=jax.ShapeDtypeStruct((n_seq * seq_len, B_WIDTH), F32),
        scratch_shapes=[pltpu.VMEM((ATTN_TQ, sum(key_lens)), F32)],
        compiler_params=pltpu.CompilerParams(vmem_limit_bytes=VMEM_LIMIT,
                                             dimension_semantics=("arbitrary", "arbitrary", "arbitrary")),
        name="attention",
    )(q, *ks, *vs)


def _merge_kernel(x_ref, mod_ref, yssm_ref, u_ref, yb_ref, gm_ref, dskip_ref, w_glu_ref, vn_ref,
                  w_s_ref, bias_ref, g_ref, w_out_ref, x1_ref, h2_ref):
    t = x_ref.shape[0]
    u = u_ref[...]
    y = yssm_ref[0] + yssm_ref[1] + dskip_ref[...] * u
    y = _gelu_tanh(y)
    y_a = y * _sigmoid(_dot(y.astype(BF16), w_glu_ref[...]))
    gm = gm_ref[...]
    vn = (_rms(gm[:, C_WIDTH:]) * vn_ref[...]).astype(BF16)
    lane = lax.broadcasted_iota(jnp.int32, (CHUNK, C_WIDTH), 1)
    mixed = []
    for c in range(t // CHUNK):
        vc = vn[c * CHUNK:(c + 1) * CHUNK, :]
        m = _dot(w_s_ref[0], vc)
        for h in range(1, GMLP_HEADS):
            m = jnp.where(lane >= h * GMLP_CH, _dot(w_s_ref[h], vc), m)
        mixed.append(m + bias_ref[...])
    y_c = gm[:, :C_WIDTH] * jnp.concatenate(mixed, axis=0)
    g = g_ref[...]
    n_a = (_rms(y_a) * g[:, :A_WIDTH]).astype(BF16)
    n_b = (_rms(yb_ref[...]) * g[:, A_WIDTH:A_WIDTH + B_WIDTH]).astype(BF16)
    n_c = (_rms(y_c) * g[:, A_WIDTH + B_WIDTH:]).astype(BF16)
    o = (_dot(n_a, w_out_ref[0:A_WIDTH, :]) + _dot(n_b, w_out_ref[A_WIDTH:A_WIDTH + B_WIDTH, :])
         + _dot(n_c, w_out_ref[A_WIDTH + B_WIDTH:, :]))
    x1 = x_ref[...] + mod_ref[0, 2:3, :] * o
    x1_ref[...] = x1
    h2_ref[...] = (_rms(x1) * (1.0 + mod_ref[0, 4:5, :]) + mod_ref[0, 3:4, :]).astype(BF16)


def _merge(x, mod_l, yssm, u, yb, gm, wl):
    t = PROJ_TILE
    tok = lambda w: pl.BlockSpec((t, w), lambda i: (i, 0))
    full = lambda a: pl.BlockSpec(a.shape, lambda i: (0,) * a.ndim)
    weights = [wl['ssm_d'], wl['w_glu'], wl['gmlp_v_norm'], wl['gmlp_w_s'], wl['gmlp_bias'],
               wl['w_out_norm'], wl['w_out']]
    return pl.pallas_call(
        _merge_kernel,
        grid=(N_TOK // t,),
        in_specs=[tok(D_MODEL),
                  pl.BlockSpec((1, N_MOD, D_MODEL), lambda i: (_mod_row(i, t), 0, 0)),
                  pl.BlockSpec((N_DIR, t, A_WIDTH), lambda i: (0, i, 0)),
                  tok(A_WIDTH), tok(B_WIDTH), tok(2 * C_WIDTH)] + [full(w) for w in weights],
        out_specs=[tok(D_MODEL), tok(D_MODEL)],
        out_shape=[jax.ShapeDtypeStruct((N_TOK, D_MODEL), F32),
                   jax.ShapeDtypeStruct((N_TOK, D_MODEL), BF16)],
        compiler_params=pltpu.CompilerParams(vmem_limit_bytes=VMEM_LIMIT),
        name="merge",
    )(x, mod_l, yssm, u, yb, gm, *weights)


def _ffn_kernel(x1_ref, mod_ref, hp_ref, hm_ref, hn_ref, wg_ref, wv_ref, cwg_ref, cwv_ref, cbg_ref,
                cbv_ref, wd_ref, o_ref, hext, up, acc):
    i = pl.program_id(0)
    j = pl.program_id(1)
    t = FFN_TILE
    halo = FFN_HALO

    @pl.when(j == 0)
    def _():
        hext[0:halo, :] = hp_ref[...]
        hext[halo:halo + t, :] = hm_ref[...]
        hext[halo + t:, :] = hn_ref[...]
        acc[...] = jnp.zeros_like(acc)

    he = hext[...]
    up[:, 0:FFN_FC] = _dot(he, wg_ref[...])
    up[:, FFN_FC:] = _dot(he, wv_ref[...])

    tok = i * t + lax.broadcasted_iota(jnp.int32, (t, 1), 0)
    seq_len = jnp.where(tok < N_PROMPT, SEQ, DEC_SEQ)
    pos = jnp.where(tok < N_PROMPT, tok & (SEQ - 1), (tok - N_PROMPT) & (DEC_SEQ - 1))
    has_prev = (pos != 0).astype(F32)
    has_next = (pos != seq_len - 1).astype(F32)

    def conv(c0, cw_ref, cb_ref):
        prev = up[halo - 1:halo - 1 + t, c0:c0 + FFN_FC] * has_prev
        cur = up[halo:halo + t, c0:c0 + FFN_FC]
        nxt = up[halo + 1:halo + 1 + t, c0:c0 + FFN_FC] * has_next
        return prev * cw_ref[0:1, :] + cur * cw_ref[1:2, :] + nxt * cw_ref[2:3, :] + cb_ref[...]

    gate = conv(0, cwg_ref, cbg_ref)
    val = conv(FFN_FC, cwv_ref, cbv_ref)
    a = (gate * _sigmoid(gate) * val).astype(BF16)
    acc[...] += _dot(a, wd_ref[...])

    @pl.when(j == pl.num_programs(1) - 1)
    def _():
        o_ref[...] = x1_ref[...] + mod_ref[0, 5:6, :] * acc[...]


def _ffn(x1, mod_l, h2, wl):
    t = FFN_TILE
    n_fc = D_FF // FFN_FC
    hb = t // FFN_HALO
    last_halo = N_TOK // FFN_HALO - 1
    return pl.pallas_call(
        _ffn_kernel,
        grid=(N_TOK // t, n_fc),
        in_specs=[pl.BlockSpec((t, D_MODEL), lambda i, j: (i, 0)),
                  pl.BlockSpec((1, N_MOD, D_MODEL), lambda i, j: (_mod_row(i, t), 0, 0)),
                  pl.BlockSpec((FFN_HALO, D_MODEL), lambda i, j: (jnp.maximum(i * hb - 1, 0), 0)),
                  pl.BlockSpec((t, D_MODEL), lambda i, j: (i, 0)),
                  pl.BlockSpec((FFN_HALO, D_MODEL), lambda i, j: (jnp.minimum((i + 1) * hb, last_halo), 0)),
                  pl.BlockSpec((D_MODEL, FFN_FC), lambda i, j: (0, j)),
                  pl.BlockSpec((D_MODEL, FFN_FC), lambda i, j: (0, n_fc + j)),
                  pl.BlockSpec((3, FFN_FC), lambda i, j: (0, j)),
                  pl.BlockSpec((3, FFN_FC), lambda i, j: (0, n_fc + j)),
                  pl.BlockSpec((1, FFN_FC), lambda i, j: (0, j)),
                  pl.BlockSpec((1, FFN_FC), lambda i, j: (0, n_fc + j)),
                  pl.BlockSpec((FFN_FC, D_MODEL), lambda i, j: (j, 0))],
        out_specs=pl.BlockSpec((t, D_MODEL), lambda i, j: (i, 0)),
        out_shape=jax.ShapeDtypeStruct((N_TOK, D_MODEL), F32),
        scratch_shapes=[pltpu.VMEM((t + 2 * FFN_HALO, D_MODEL), BF16),
                        pltpu.VMEM((t + 2 * FFN_HALO, 2 * FFN_FC), F32),
                        pltpu.VMEM((t, D_MODEL), F32)],
        compiler_params=pltpu.CompilerParams(vmem_limit_bytes=VMEM_LIMIT,
                                             dimension_semantics=("arbitrary", "arbitrary")),
        name="conv_ffn",
    )(x1, mod_l, h2, h2, h2, wl['ffn_w_up'], wl['ffn_w_up'], wl['ffn_conv_w'], wl['ffn_conv_w'],
      wl['ffn_conv_b'], wl['ffn_conv_b'], wl['ffn_w_down'])


def _pad_heads(w, head_w, used, offset=0):
    lead = w.shape[:-1]
    w = w.reshape(lead + (MLA_HEADS, head_w))[..., :used]
    pad = [(0, 0)] * (len(lead) + 1) + [(offset, HEAD_PAD - used - offset)]
    return jnp.pad(w, pad).reshape(lead + (MLA_HEADS * HEAD_PAD,))


def _head_gain(g):
    return jnp.pad(g, (0, HEAD_PAD - MLA_QK)).reshape(1, HEAD_PAD)


def _ssm_tables(a_re, a_im, b_re, b_im, c_re, c_im, log_dt):
    a = lax.complex(a_re, a_im)
    dt = jnp.exp(log_dt)[..., None]
    a_bar = jnp.exp(a * dt)
    b_bar = ((a_bar - 1.0) / a)[..., None] * lax.complex(b_re, b_im)
    steps = jnp.arange(1, SSM_STEPS + 1, dtype=F32)[None, :, None, None]
    pw = jnp.exp((a * dt)[:, None] * steps)
    eye = jnp.eye(SSM_G, dtype=F32)
    blk_b = lambda m: jnp.einsum('dgpc,gh->dgchp', m, eye).reshape(N_DIR, A_WIDTH, SSM_STATE)
    blk_c = lambda m: jnp.einsum('dgcp,gh->dgphc', m, eye).reshape(N_DIR, SSM_STATE, A_WIDTH)
    flat = lambda m: m.reshape(N_DIR, 1, SSM_STATE)
    return {
        'a_r': flat(a_bar.real), 'a_i': flat(a_bar.imag),
        'pw_r': pw.real.reshape(N_DIR, SSM_STEPS, SSM_STATE),
        'pw_i': pw.imag.reshape(N_DIR, SSM_STEPS, SSM_STATE),
        'b_blk': jnp.concatenate([blk_b(b_bar.real), blk_b(b_bar.imag)], axis=-1).astype(BF16),
        'c_r': blk_c(c_re).astype(BF16),
        'c_i': blk_c(-c_im).astype(BF16),
    }


def _rope_tables():
    rows = DEC_SEQ // GRID_W
    row = jnp.repeat(jnp.arange(rows, dtype=F32), GRID_W)
    col = jnp.tile(jnp.arange(GRID_W, dtype=F32), rows)
    n_freq = MLA_ROPE // 4
    inv = ROPE_BASE ** (-jnp.arange(n_freq, dtype=F32) / n_freq)
    ang = jnp.concatenate([row[:, None] * inv, col[:, None] * inv], axis=-1)
    cos, sin = jnp.cos(ang), jnp.sin(ang)
    half = MLA_ROPE // 2
    z = jnp.zeros((DEC_SEQ, HEAD_PAD), F32)
    cos_t = jnp.ones((DEC_SEQ, HEAD_PAD), F32)
    cos_t = cos_t.at[:, ROPE_LANE0:ROPE_LANE0 + half].set(cos).at[:, ROPE_LANE0 + half:ROPE_LANE0 + 2 * half].set(cos)
    sin_a = z.at[:, ROPE_LANE0:ROPE_LANE0 + half].set(-sin)
    sin_b = z.at[:, ROPE_LANE0 + half:ROPE_LANE0 + 2 * half].set(sin)
    lat = lambda tb: jnp.tile(tb, (DEC_BATCH, 1))
    ones = jnp.ones((N_PROMPT, HEAD_PAD), F32)
    zeros = jnp.zeros((N_PROMPT, HEAD_PAD), F32)
    return (jnp.concatenate([ones, lat(cos_t)]), jnp.concatenate([zeros, lat(sin_a)]),
            jnp.concatenate([zeros, lat(sin_b)]))


def _layer_weights(l, w_in, ssm_d, ssm_w_glu, q_a_norm, kv_a_norm, w_uq, w_ukv, q_norm, k_norm,
                   gmlp_v_norm, gmlp_w_s, gmlp_b_s, w_out_norm, w_out, ffn_w_up, ffn_conv_w,
                   ffn_conv_b, ffn_w_down):
    wi = w_in[l]
    w_in_a = jnp.concatenate([wi[:, OFF_SSM:OFF_KR], wi[:, OFF_GM:]], axis=1).astype(BF16)
    w_in_kr = jnp.pad(wi[:, OFF_KR:OFF_GM], ((0, 0), (ROPE_LANE0, HEAD_PAD - ROPE_LANE0 - MLA_ROPE))).astype(BF16)
    ukv = w_ukv[l].reshape(KV_RANK, MLA_HEADS, MLA_NOPE + MLA_V)
    return {
        'w_in_a': w_in_a, 'w_in_kr': w_in_kr,
        'q_a_norm': q_a_norm[l].reshape(1, Q_RANK), 'kv_a_norm': kv_a_norm[l].reshape(1, KV_RANK),
        'w_uq': _pad_heads(w_uq[l], MLA_QK, MLA_QK).astype(BF16),
        'w_uk': _pad_heads(ukv[..., :MLA_NOPE].reshape(KV_RANK, -1), MLA_NOPE, MLA_NOPE).astype(BF16),
        'w_uv': ukv[..., MLA_NOPE:].reshape(KV_RANK, B_WIDTH).astype(BF16),
        'q_norm': _head_gain(q_norm[l]), 'k_norm': _head_gain(k_norm[l]),
        'ssm_d': ssm_d[l].reshape(1, A_WIDTH), 'w_glu': ssm_w_glu[l].astype(BF16),
        'gmlp_v_norm': gmlp_v_norm[l].reshape(1, C_WIDTH), 'gmlp_w_s': gmlp_w_s[l].astype(BF16),
        'gmlp_bias': jnp.repeat(gmlp_b_s[l].T, GMLP_CH, axis=1),
        'w_out_norm': w_out_norm[l].reshape(1, D_MODEL), 'w_out': w_out[l].astype(BF16),
        'ffn_w_up': ffn_w_up[l].astype(BF16), 'ffn_conv_w': ffn_conv_w[l],
        'ffn_conv_b': ffn_conv_b[l].reshape(1, 2 * D_FF), 'ffn_w_down': ffn_w_down[l].astype(BF16),
    }


def kernel(x_prompt, x_sample, cache_ckv, cache_krope, state_ssm_re, state_ssm_im, c, c_ctx, w_mod, b_mod, w_in, ssm_a_re, ssm_a_im, ssm_b_re, ssm_b_im, ssm_c_re, ssm_c_im, ssm_log_dt, ssm_d, ssm_w_glu, q_a_norm, kv_a_norm, w_uq, w_ukv, q_norm, k_norm, gmlp_v_norm, gmlp_w_s, gmlp_b_s, w_out_norm, w_out, ffn_w_up, ffn_conv_w, ffn_conv_b, ffn_w_down):
    x = jnp.concatenate([x_prompt.reshape(N_PROMPT, D_MODEL), x_sample.reshape(N_SAMPLE, D_MODEL)], axis=0)
    cond8 = jnp.concatenate([c_ctx[None, :], c, jnp.zeros((8 - 1 - DEC_BATCH, D_MODEL), F32)], axis=0)
    mod = _modulation(cond8, w_mod, b_mod).reshape(DEPTH, 8, N_MOD, D_MODEL)
    rope_tabs = _rope_tables()

    ckv_out, kr_out, hre_out, him_out = [], [], [], []
    for l in range(DEPTH):
        wl = _layer_weights(l, w_in, ssm_d, ssm_w_glu, q_a_norm, kv_a_norm, w_uq, w_ukv, q_norm, k_norm,
                            gmlp_v_norm, gmlp_w_s, gmlp_b_s, w_out_norm, w_out, ffn_w_up, ffn_conv_w,
                            ffn_conv_b, ffn_w_down)
        tabs = _ssm_tables(ssm_a_re[l], ssm_a_im[l], ssm_b_re[l], ssm_b_im[l], ssm_c_re[l], ssm_c_im[l],
                           ssm_log_dt[l])
        mod_l = mod[l]

        u, gm, ckv, kr, q, k, v = _projection(x, mod_l, rope_tabs, wl)

        kr_c = jnp.pad(cache_krope[:, l].reshape(DEC_BATCH * PAST_LEN, MLA_ROPE),
                       ((0, 0), (ROPE_LANE0, HEAD_PAD - ROPE_LANE0 - MLA_ROPE)))
        k_c, v_c = _ctx_kv(cache_ckv[:, l].reshape(DEC_BATCH * PAST_LEN, KV_RANK), kr_c, wl)

        def h0(s):
            s = jnp.transpose(s[:, l].reshape(DEC_BATCH, N_DIR, SSM_STATE), (1, 0, 2))
            return jnp.pad(s, ((0, 0), (0, SSM_STREAMS - DEC_BATCH), (0, 0)))

        yssm, hf_r, hf_i = _ssm(u, h0(state_ssm_re), h0(state_ssm_im), tabs)

        yb_p = _attention(q, [k], [v], BATCH, SEQ, 0, [SEQ], [0])
        yb_s = _attention(q, [k_c, k], [v_c, v], DEC_BATCH, DEC_SEQ, N_PROMPT,
                          [PAST_LEN, DEC_SEQ], [0, N_PROMPT])
        yb = jnp.concatenate([yb_p, yb_s], axis=0)

        x1, h2 = _merge(x, mod_l, yssm, u, yb, gm, wl)
        x = _ffn(x1, mod_l, h2, wl)

        ckv_out.append(ckv[:N_PROMPT].reshape(BATCH, SEQ, KV_RANK))
        kr_out.append(kr[:N_PROMPT, ROPE_LANE0:ROPE_LANE0 + MLA_ROPE].reshape(BATCH, SEQ, MLA_ROPE))
        to_state = lambda s: jnp.transpose(s, (1, 0, 2)).reshape(BATCH, N_DIR, SSM_G, SSM_P)
        hre_out.append(to_state(hf_r))
        him_out.append(to_state(hf_i))

    y_prompt = x[:N_PROMPT].reshape(BATCH, SEQ, D_MODEL)
    y_sample = x[N_PROMPT:].reshape(DEC_BATCH, DEC_SEQ, D_MODEL)
    return (y_prompt, y_sample, jnp.stack(ckv_out, axis=1), jnp.stack(kr_out, axis=1),
            jnp.stack(hre_out, axis=1), jnp.stack(him_out, axis=1))
```

```python
import functools
import math

import jax
import jax.numpy as jnp
from jax import lax
from jax.experimental import pallas as pl
from jax.experimental.pallas import tpu as pltpu

F32 = jnp.float32
BF16 = jnp.bfloat16

D_MODEL = 1024
BATCH = 16
SEQ = 256
DEPTH = 2
DEC_BATCH = 2
DEC_SEQ = 4096
PAST_LEN = 256
GRID_W = 64
EPS = 1e-6
N_MOD = 6
A_WIDTH = 256
B_WIDTH = 512
C_WIDTH = 256
SSM_CG = 16
SSM_G = 16
SSM_P = 64
N_DIR = 2
MLA_HEADS = 8
MLA_NOPE = 64
MLA_ROPE = 32
MLA_V = 64
MLA_QK = 96
Q_RANK = 256
KV_RANK = 128
ROPE_BASE = 10000.0
ATTN_SCALE = 1.0 / math.sqrt(MLA_QK)
GMLP_HEADS = 4
GMLP_CH = 64
CHUNK = 128
OFF_SSM = 0
OFF_Q = OFF_SSM + A_WIDTH
OFF_KV = OFF_Q + Q_RANK
OFF_KR = OFF_KV + KV_RANK
OFF_GM = OFF_KR + MLA_ROPE
D_FF = 2816

N_PROMPT = BATCH * SEQ
N_SAMPLE = DEC_BATCH * DEC_SEQ
N_TOK = N_PROMPT + N_SAMPLE
SSM_STATE = SSM_G * SSM_P

LANES = 128
HEAD_PAD = LANES
ROPE_LANE0 = MLA_NOPE

PROJ_TILE = 512
SSM_STREAMS = 8
SSM_STEPS = 256
SSM_UNIT = SSM_STREAMS * SSM_STEPS
ATTN_TQ = 256
ATTN_KC = 512
FFN_ROWS = SEQ
FFN_FC = 256
FFN_HALO = 16
VMEM_LIMIT = 48 * 1024 * 1024


def _mod_row(tile, tile_tokens):
    start = tile * tile_tokens
    return jnp.where(start < N_PROMPT, 0, 1 + (start - N_PROMPT) // DEC_SEQ)


def _rms(x):
    return x * lax.rsqrt(jnp.mean(x * x, axis=-1, keepdims=True) + EPS)


def _sigmoid(x):
    return 1.0 / (1.0 + jnp.exp(-x))


def _gelu_tanh(x):
    return 0.5 * x * (1.0 + jnp.tanh(math.sqrt(2.0 / math.pi) * (x + 0.044715 * (x * x * x))))


def _dot(a, b):
    return jnp.dot(a, b, preferred_element_type=F32)


def _mod_kernel(cond_ref, w_ref, b_ref, o_ref):
    cond = cond_ref[...]
    s = (cond * _sigmoid(cond)).astype(BF16)
    o_ref[0] = _dot(s, w_ref[0].astype(BF16)) + b_ref[0]


def _modulation(cond8, w_mod, b_mod):
    tn = 1536
    n_cols = N_MOD * D_MODEL
    return pl.pallas_call(
        _mod_kernel,
        grid=(DEPTH, n_cols // tn),
        in_specs=[
            pl.BlockSpec((8, D_MODEL), lambda l, j: (0, 0)),
            pl.BlockSpec((1, D_MODEL, tn), lambda l, j: (l, 0, j)),
            pl.BlockSpec((1, 1, tn), lambda l, j: (l, 0, j)),
        ],
        out_specs=pl.BlockSpec((1, 8, tn), lambda l, j: (l, 0, j)),
        out_shape=jax.ShapeDtypeStruct((DEPTH, 8, n_cols), F32),
        compiler_params=pltpu.CompilerParams(vmem_limit_bytes=VMEM_LIMIT),
        name="modulation",
    )(cond8, w_mod, b_mod.reshape(DEPTH, 1, n_cols))


def _write_heads(allh, extra, gain, rope, out_ref):
    for h in range(MLA_HEADS):
        t = allh[:, h * HEAD_PAD:(h + 1) * HEAD_PAD]
        if extra is not None:
            t = t + extra
        ms = jnp.sum(t * t, axis=-1, keepdims=True) * (1.0 / MLA_QK)
        t = t * lax.rsqrt(ms + EPS) * gain
        if rope is not None:
            cos, sin_a, sin_b = rope
            t = (t * cos + pltpu.roll(t, HEAD_PAD - MLA_ROPE // 2, 1) * sin_a
                 + pltpu.roll(t, MLA_ROPE // 2, 1) * sin_b)
        out_ref[h] = t.astype(BF16)


def _proj_kernel(x_ref, mod_ref, cos_ref, sa_ref, sb_ref, w_in_ref, w_kr_ref, qan_ref, kvn_ref,
                 w_uq_ref, w_uk_ref, w_uv_ref, qn_ref, kn_ref,
                 u_ref, gm_ref, ckv_ref, kr_ref, q_ref, k_ref, v_ref):
    x = x_ref[...]
    shift = mod_ref[0, 0:1, :]
    scale = mod_ref[0, 1:2, :]
    h = (_rms(x) * (1.0 + scale) + shift).astype(BF16)
    z = _dot(h, w_in_ref[...])
    kr = _dot(h, w_kr_ref[...])
    u_ref[...] = z[:, 0:A_WIDTH]
    gm_ref[...] = z[:, A_WIDTH + Q_RANK + KV_RANK:]
    kr_ref[...] = kr
    ckv = _rms(z[:, A_WIDTH + Q_RANK:A_WIDTH + Q_RANK + KV_RANK]) * kvn_ref[...]
    ckv_ref[...] = ckv
    rope = (cos_ref[...], sa_ref[...], sb_ref[...])
    cq = (_rms(z[:, A_WIDTH:A_WIDTH + Q_RANK]) * qan_ref[...]).astype(BF16)
    _write_heads(_dot(cq, w_uq_ref[...]), None, qn_ref[...], rope, q_ref)
    ckv_b = ckv.astype(BF16)
    _write_heads(_dot(ckv_b, w_uk_ref[...]), kr, kn_ref[...], rope, k_ref)
    v_ref[...] = _dot(ckv_b, w_uv_ref[...]).astype(BF16)


def _projection(x, mod_l, rope_tabs, wl):
    t = PROJ_TILE
    n_tiles = N_TOK // t
    tok = lambda w: pl.BlockSpec((t, w), lambda i: (i, 0))
    full = lambda a: pl.BlockSpec(a.shape, lambda i: (0,) * a.ndim)
    heads = pl.BlockSpec((MLA_HEADS, t, HEAD_PAD), lambda i: (0, i, 0))
    weights = [wl['w_in_a'], wl['w_in_kr'], wl['q_a_norm'], wl['kv_a_norm'], wl['w_uq'], wl['w_uk'],
               wl['w_uv'], wl['q_norm'], wl['k_norm']]
    return pl.pallas_call(
        _proj_kernel,
        grid=(n_tiles,),
        in_specs=[tok(D_MODEL),
                  pl.BlockSpec((1, N_MOD, D_MODEL), lambda i: (_mod_row(i, t), 0, 0)),
                  tok(HEAD_PAD), tok(HEAD_PAD), tok(HEAD_PAD)] + [full(w) for w in weights],
        out_specs=[tok(A_WIDTH), tok(2 * C_WIDTH), tok(KV_RANK), tok(HEAD_PAD), heads, heads, tok(B_WIDTH)],
        out_shape=[
            jax.ShapeDtypeStruct((N_TOK, A_WIDTH), F32),
            jax.ShapeDtypeStruct((N_TOK, 2 * C_WIDTH), F32),
            jax.ShapeDtypeStruct((N_TOK, KV_RANK), F32),
            jax.ShapeDtypeStruct((N_TOK, HEAD_PAD), F32),
            jax.ShapeDtypeStruct((MLA_HEADS, N_TOK, HEAD_PAD), BF16),
            jax.ShapeDtypeStruct((MLA_HEADS, N_TOK, HEAD_PAD), BF16),
            jax.ShapeDtypeStruct((N_TOK, B_WIDTH), BF16),
        ],
        compiler_params=pltpu.CompilerParams(vmem_limit_bytes=VMEM_LIMIT),
        name="projection",
    )(x, mod_l, *rope_tabs, *weights)


def _ctx_kv_kernel(ckv_ref, kr_ref, w_uk_ref, w_uv_ref, kn_ref, k_ref, v_ref):
    ckv_b = ckv_ref[...].astype(BF16)
    _write_heads(_dot(ckv_b, w_uk_ref[...]), kr_ref[...], kn_ref[...], None, k_ref)
    v_ref[...] = _dot(ckv_b, w_uv_ref[...]).astype(BF16)


def _ctx_kv(ckv, kr128, wl):
    n = ckv.shape[0]
    full = lambda a: pl.BlockSpec(a.shape, lambda i: (0,) * a.ndim)
    weights = [wl['w_uk'], wl['w_uv'], wl['k_norm']]
    return pl.pallas_call(
        _ctx_kv_kernel,
        grid=(1,),
        in_specs=[full(ckv), full(kr128)] + [full(w) for w in weights],
        out_specs=[pl.BlockSpec((MLA_HEADS, n, HEAD_PAD), lambda i: (0, 0, 0)),
                   pl.BlockSpec((n, B_WIDTH), lambda i: (0, 0))],
        out_shape=[jax.ShapeDtypeStruct((MLA_HEADS, n, HEAD_PAD), BF16),
                   jax.ShapeDtypeStruct((n, B_WIDTH), BF16)],
        compiler_params=pltpu.CompilerParams(vmem_limit_bytes=VMEM_LIMIT),
        name="context_kv",
    )(ckv, kr128, *weights)


def _ssm_block(d, k):
    return jnp.where((d == 1) & (k >= 2), k ^ 1, k)


def _ssm_kernel(u_ref, h0r_ref, h0i_ref, ar_ref, ai_ref, pwr_ref, pwi_ref, bm_ref, cr_ref, ci_ref,
                y_ref, hfr_ref, hfi_ref, sre, sim, hin_r, hin_i, car_r, car_i):
    d = pl.program_id(0)
    k = pl.program_id(1)
    rows = 256

    @pl.when(k == 0)
    def _():
        car_r[...] = jnp.zeros_like(car_r)
        car_i[...] = jnp.zeros_like(car_i)

    for c in range(SSM_UNIT // rows):
        sl = slice(c * rows, (c + 1) * rows)
        bu = _dot(u_ref[sl, :].astype(BF16), bm_ref[0])
        sre[sl, :] = bu[:, :SSM_STATE]
        sim[sl, :] = bu[:, SSM_STATE:]

    a_r = jnp.broadcast_to(ar_ref[0], (SSM_STREAMS, SSM_STATE))
    a_i = jnp.broadcast_to(ai_ref[0], (SSM_STREAMS, SSM_STATE))

    def stream_rows(tau):
        return pl.ds(pl.multiple_of(tau * SSM_STREAMS, SSM_STREAMS), SSM_STREAMS)

    def scan_step(i, carry):
        h_r, h_i = carry
        tau = jnp.where(d == 0, i, SSM_STEPS - 1 - i)
        n_r = a_r * h_r - a_i * h_i + sre[stream_rows(tau), :]
        n_i = a_r * h_i + a_i * h_r + sim[stream_rows(tau), :]
        sre[stream_rows(tau), :] = n_r
        sim[stream_rows(tau), :] = n_i
        return n_r, n_i

    zero = jnp.zeros((SSM_STREAMS, SSM_STATE), F32)
    end_r, end_i = lax.fori_loop(0, SSM_STEPS, scan_step, (zero, zero))

    @pl.when(k < 2)
    def _():
        hfr_ref[0] = end_r
        hfi_ref[0] = end_i

    @pl.when(k >= 2)
    def _():
        b = (k - 2) // 2
        first = (k % 2) == 0
        c_r = jnp.where(first, h0r_ref[0, pl.ds(b, 1), :], car_r[...])
        c_i = jnp.where(first, h0i_ref[0, pl.ds(b, 1), :], car_i[...])
        full_r = pwr_ref[0, SSM_STEPS - 1:SSM_STEPS, :]
        full_i = pwi_ref[0, SSM_STEPS - 1:SSM_STEPS, :]

        def chain(order):
            cr, ci = c_r, c_i
            for j in order:
                hin_r[j:j + 1, :] = cr
                hin_i[j:j + 1, :] = ci
                nr = end_r[j:j + 1, :] + full_r * cr - full_i * ci
                ni = end_i[j:j + 1, :] + full_r * ci + full_i * cr
                cr, ci = nr, ni
            car_r[...] = cr
            car_i[...] = ci

        @pl.when(d == 0)
        def _():
            chain(range(SSM_STREAMS))

        @pl.when(d == 1)
        def _():
            chain(range(SSM_STREAMS - 1, -1, -1))

        g_r = hin_r[...]
        g_i = hin_i[...]

        def fix_step(i, carry):
            p = jnp.where(d == 0, i, SSM_STEPS - 1 - i)
            p_r = jnp.broadcast_to(pwr_ref[0, pl.ds(p, 1), :], (SSM_STREAMS, SSM_STATE))
            p_i = jnp.broadcast_to(pwi_ref[0, pl.ds(p, 1), :], (SSM_STREAMS, SSM_STATE))
            sr = stream_rows(i)
            sre[sr, :] = sre[sr, :] + (p_r * g_r - p_i * g_i)
            sim[sr, :] = sim[sr, :] + (p_r * g_i + p_i * g_r)
            return carry

        lax.fori_loop(0, SSM_STEPS, fix_step, 0)

    for c in range(SSM_UNIT // rows):
        sl = slice(c * rows, (c + 1) * rows)
        y_ref[0, sl, :] = (_dot(sre[sl, :].astype(BF16), cr_ref[0])
                           + _dot(sim[sl, :].astype(BF16), ci_ref[0]))


def _step_major(x, inverse=False):
    lead = x.shape[:-2]
    a, b = (SSM_STEPS, SSM_STREAMS) if inverse else (SSM_STREAMS, SSM_STEPS)
    x = x.reshape(lead + (N_TOK // SSM_UNIT, a, b, x.shape[-1]))
    return jnp.swapaxes(x, -3, -2).reshape(lead + (N_TOK, x.shape[-1]))


def _ssm(u, h0r, h0i, tabs):
    n_units = N_TOK // SSM_UNIT
    u = _step_major(u)
    per_dir = lambda a: pl.BlockSpec((1,) + a.shape[1:], lambda d, k: (d,) + (0,) * (a.ndim - 1))
    y, hf_r, hf_i = pl.pallas_call(
        _ssm_kernel,
        grid=(N_DIR, n_units),
        in_specs=[pl.BlockSpec((SSM_UNIT, A_WIDTH), lambda d, k: (_ssm_block(d, k), 0)),
                  per_dir(h0r), per_dir(h0i)] + [per_dir(tabs[n]) for n in
                                                 ('a_r', 'a_i', 'pw_r', 'pw_i', 'b_blk', 'c_r', 'c_i')],
        out_specs=[pl.BlockSpec((1, SSM_UNIT, A_WIDTH), lambda d, k: (d, _ssm_block(d, k), 0)),
                   pl.BlockSpec((1, SSM_STREAMS, SSM_STATE), lambda d, k: (d, jnp.minimum(k, 1), 0)),
                   pl.BlockSpec((1, SSM_STREAMS, SSM_STATE), lambda d, k: (d, jnp.minimum(k, 1), 0))],
        out_shape=[jax.ShapeDtypeStruct((N_DIR, N_TOK, A_WIDTH), F32),
                   jax.ShapeDtypeStruct((N_DIR, BATCH, SSM_STATE), F32),
                   jax.ShapeDtypeStruct((N_DIR, BATCH, SSM_STATE), F32)],
        scratch_shapes=[pltpu.VMEM((SSM_UNIT, SSM_STATE), F32), pltpu.VMEM((SSM_UNIT, SSM_STATE), F32),
                        pltpu.VMEM((SSM_STREAMS, SSM_STATE), F32), pltpu.VMEM((SSM_STREAMS, SSM_STATE), F32),
                        pltpu.VMEM((1, SSM_STATE), F32), pltpu.VMEM((1, SSM_STATE), F32)],
        compiler_params=pltpu.CompilerParams(vmem_limit_bytes=VMEM_LIMIT,
                                             dimension_semantics=("arbitrary", "arbitrary")),
        name="ssm_scan",
    )(u, h0r, h0i, tabs['a_r'], tabs['a_i'], tabs['pw_r'], tabs['pw_i'], tabs['b_blk'], tabs['c_r'],
      tabs['c_i'])
    return _step_major(y, inverse=True), hf_r, hf_i


def _attn_kernel(*refs, key_lens):
    n_src = len(key_lens)
    q_ref = refs[0]
    k_refs = refs[1:1 + n_src]
    v_refs = refs[1 + n_src:1 + 2 * n_src]
    o_ref = refs[1 + 2 * n_src]
    s_ref = refs[2 + 2 * n_src]
    chunks = []
    off = 0
    for si, n in enumerate(key_lens):
        for c0 in range(0, n, ATTN_KC):
            cl = min(ATTN_KC, n - c0)
            chunks.append((si, c0, cl, off))
            off += cl
    outs = []
    for hh in range(2):
        q = q_ref[hh]
        m = jnp.full((ATTN_TQ, 1), -jnp.inf, F32)
        for si, c0, cl, o in chunks:
            s = lax.dot_general(q, k_refs[si][hh, c0:c0 + cl, :], (((1,), (1,)), ((), ())),
                                preferred_element_type=F32) * ATTN_SCALE
            s_ref[:, o:o + cl] = s
            m = jnp.maximum(m, jnp.max(s, axis=-1, keepdims=True))
        l = jnp.zeros((ATTN_TQ, 1), F32)
        acc = jnp.zeros((ATTN_TQ, LANES), F32)
        for si, c0, cl, o in chunks:
            p = jnp.exp(s_ref[:, o:o + cl] - m)
            l = l + jnp.sum(p, axis=-1, keepdims=True)
            acc = acc + _dot(p.astype(BF16), v_refs[si][c0:c0 + cl, :])
        outs.append(acc / l)
    lane = lax.broadcasted_iota(jnp.int32, (ATTN_TQ, LANES), 1)
    o_ref[...] = jnp.where(lane < MLA_V, outs[0], outs[1])


def _attention(q, ks, vs, n_seq, seq_len, tok0, key_lens, key_tok0):
    n_qt = seq_len // ATTN_TQ
    q0 = tok0 // ATTN_TQ
    in_specs = [pl.BlockSpec((2, ATTN_TQ, HEAD_PAD), lambda b, hp, qi: (hp, q0 + b * n_qt + qi, 0))]
    for n, t0 in zip(key_lens, key_tok0):
        in_specs.append(pl.BlockSpec((2, n, HEAD_PAD), lambda b, hp, qi, n=n, t0=t0: (hp, t0 // n + b, 0)))
    for n, t0 in zip(key_lens, key_tok0):
        in_specs.append(pl.BlockSpec((n, LANES), lambda b, hp, qi, n=n, t0=t0: (t0 // n + b, hp)))
    return pl.pallas_call(
        functools.partial(_attn_kernel, key_lens=tuple(key_lens)),
        grid=(n_seq, MLA_HEADS // 2, n_qt),
        in_specs=in_specs,
        out_specs=pl.BlockSpec((ATTN_TQ, LANES), lambda b, hp, qi: (b * n_qt + qi, hp)),
        out_shape=jax.ShapeDtypeStruct((n_seq * seq_len, B_WIDTH), F32),
        scratch_shapes=[pltpu.VMEM((ATTN_TQ, sum(key_lens)), F32)],
        compiler_params=pltpu.CompilerParams(vmem_limit_bytes=VMEM_LIMIT,
                                             dimension_semantics=("arbitrary", "arbitrary", "arbitrary")),
        name="attention",
    )(q, *ks, *vs)


def _merge_kernel(x_ref, mod_ref, yssm_ref, u_ref, yb_ref, gm_ref, dskip_ref, w_glu_ref, vn_ref,
                  w_s_ref, bias_ref, g_ref, w_out_ref, x1_ref, h2_ref):
    t = x_ref.shape[0]
    u = u_ref[...]
    y = yssm_ref[0] + yssm_ref[1] + dskip_ref[...] * u
    y = _gelu_tanh(y)
    y_a = y * _sigmoid(_dot(y.astype(BF16), w_glu_ref[...]))
    gm = gm_ref[...]
    vn = (_rms(gm[:, C_WIDTH:]) * vn_ref[...]).astype(BF16)
    lane = lax.broadcasted_iota(jnp.int32, (CHUNK, C_WIDTH), 1)
    mixed = []
    for c in range(t // CHUNK):
        vc = vn[c * CHUNK:(c + 1) * CHUNK, :]
        m = _dot(w_s_ref[0], vc)
        for h in range(1, GMLP_HEADS):
            m = jnp.where(lane >= h * GMLP_CH, _dot(w_s_ref[h], vc), m)
        mixed.append(m + bias_ref[...])
    y_c = gm[:, :C_WIDTH] * jnp.concatenate(mixed, axis=0)
    g = g_ref[...]
    n_a = (_rms(y_a) * g[:, :A_WIDTH]).astype(BF16)
    n_b = (_rms(yb_ref[...]) * g[:, A_WIDTH:A_WIDTH + B_WIDTH]).astype(BF16)
    n_c = (_rms(y_c) * g[:, A_WIDTH + B_WIDTH:]).astype(BF16)
    o = (_dot(n_a, w_out_ref[0:A_WIDTH, :]) + _dot(n_b, w_out_ref[A_WIDTH:A_WIDTH + B_WIDTH, :])
         + _dot(n_c, w_out_ref[A_WIDTH + B_WIDTH:, :]))
    x1 = x_ref[...] + mod_ref[0, 2:3, :] * o
    x1_ref[...] = x1
    h2_ref[...] = (_rms(x1) * (1.0 + mod_ref[0, 4:5, :]) + mod_ref[0, 3:4, :]).astype(BF16)


def _merge(x, mod_l, yssm, u, yb, gm, wl):
    t = PROJ_TILE
    tok = lambda w: pl.BlockSpec((t, w), lambda i: (i, 0))
    full = lambda a: pl.BlockSpec(a.shape, lambda i: (0,) * a.ndim)
    weights = [wl['ssm_d'], wl['w_glu'], wl['gmlp_v_norm'], wl['gmlp_w_s'], wl['gmlp_bias'],
               wl['w_out_norm'], wl['w_out']]
    return pl.pallas_call(
        _merge_kernel,
        grid=(N_TOK // t,),
        in_specs=[tok(D_MODEL),
                  pl.BlockSpec((1, N_MOD, D_MODEL), lambda i: (_mod_row(i, t), 0, 0)),
                  pl.BlockSpec((N_DIR, t, A_WIDTH), lambda i: (0, i, 0)),
                  tok(A_WIDTH), tok(B_WIDTH), tok(2 * C_WIDTH)] + [full(w) for w in weights],
        out_specs=[tok(D_MODEL), tok(D_MODEL)],
        out_shape=[jax.ShapeDtypeStruct((N_TOK, D_MODEL), F32),
                   jax.ShapeDtypeStruct((N_TOK, D_MODEL), BF16)],
        compiler_params=pltpu.CompilerParams(vmem_limit_bytes=VMEM_LIMIT),
        name="merge",
    )(x, mod_l, yssm, u, yb, gm, *weights)


def _ffn_kernel(x1_ref, mod_ref, hp_ref, hm_ref, hn_ref, wup_ref, cw_ref, cb_ref, wd_ref, o_ref, a_scr):
    seg = pl.program_id(0) - N_PROMPT // FFN_ROWS
    per_seq = DEC_SEQ // FFN_ROWS
    starts = (seg < 0) | ((seg & (per_seq - 1)) == 0)
    ends = (seg < 0) | ((seg & (per_seq - 1)) == per_seq - 1)
    h_prev = hp_ref[...]
    h_next = hn_ref[...]
    h_prev = jnp.where(starts, jnp.zeros_like(h_prev), h_prev)
    h_next = jnp.where(ends, jnp.zeros_like(h_next), h_next)
    he = jnp.concatenate([h_prev, hm_ref[...], h_next], axis=0)
    r0 = FFN_HALO

    def conv(u, c0):
        cols = slice(c0, c0 + FFN_FC)
        return (u[r0 - 1:r0 - 1 + FFN_ROWS] * cw_ref[0:1, cols] + u[r0:r0 + FFN_ROWS] * cw_ref[1:2, cols]
                + u[r0 + 1:r0 + 1 + FFN_ROWS] * cw_ref[2:3, cols] + cb_ref[:, cols])

    for j in range(D_FF // FFN_FC):
        cg = j * FFN_FC
        cv = D_FF + j * FFN_FC
        gate = conv(_dot(he, wup_ref[:, cg:cg + FFN_FC]), cg)
        val = conv(_dot(he, wup_ref[:, cv:cv + FFN_FC]), cv)
        a_scr[:, cg:cg + FFN_FC] = (gate * _sigmoid(gate) * val).astype(BF16)

    o_ref[...] = x1_ref[...] + mod_ref[0, 5:6, :] * _dot(a_scr[...], wd_ref[...])


def _ffn(x1, mod_l, h2, wl):
    t = FFN_ROWS
    hb = t // FFN_HALO
    last_halo = N_TOK // FFN_HALO - 1
    resident = lambda a: pl.BlockSpec(a.shape, lambda i: (0,) * a.ndim, pipeline_mode=pl.Buffered(1))
    weights = [wl['ffn_w_up'], wl['ffn_conv_w'], wl['ffn_conv_b'], wl['ffn_w_down']]
    return pl.pallas_call(
        _ffn_kernel,
        grid=(N_TOK // t,),
        in_specs=[pl.BlockSpec((t, D_MODEL), lambda i: (i, 0)),
                  pl.BlockSpec((1, N_MOD, D_MODEL), lambda i: (_mod_row(i, t), 0, 0)),
                  pl.BlockSpec((FFN_HALO, D_MODEL), lambda i: (jnp.maximum(i * hb - 1, 0), 0)),
                  pl.BlockSpec((t, D_MODEL), lambda i: (i, 0)),
                  pl.BlockSpec((FFN_HALO, D_MODEL), lambda i: (jnp.minimum((i + 1) * hb, last_halo), 0))]
                 + [resident(w) for w in weights],
        out_specs=pl.BlockSpec((t, D_MODEL), lambda i: (i, 0)),
        out_shape=jax.ShapeDtypeStruct((N_TOK, D_MODEL), F32),
        scratch_shapes=[pltpu.VMEM((t, D_FF), BF16)],
        compiler_params=pltpu.CompilerParams(vmem_limit_bytes=VMEM_LIMIT,
                                             dimension_semantics=("arbitrary",)),
        name="conv_ffn",
    )(x1, mod_l, h2, h2, h2, *weights)


def _pad_heads(w, head_w, used, offset=0):
    lead = w.shape[:-1]
    w = w.reshape(lead + (MLA_HEADS, head_w))[..., :used]
    pad = [(0, 0)] * (len(lead) + 1) + [(offset, HEAD_PAD - used - offset)]
    return jnp.pad(w, pad).reshape(lead + (MLA_HEADS * HEAD_PAD,))


def _head_gain(g):
    return jnp.pad(g, (0, HEAD_PAD - MLA_QK)).reshape(1, HEAD_PAD)


def _ssm_tables(a_re, a_im, b_re, b_im, c_re, c_im, log_dt):
    a = lax.complex(a_re, a_im)
    dt = jnp.exp(log_dt)[..., None]
    a_bar = jnp.exp(a * dt)
    b_bar = ((a_bar - 1.0) / a)[..., None] * lax.complex(b_re, b_im)
    steps = jnp.arange(1, SSM_STEPS + 1, dtype=F32)[None, :, None, None]
    pw = jnp.exp((a * dt)[:, None] * steps)
    eye = jnp.eye(SSM_G, dtype=F32)
    blk_b = lambda m: jnp.einsum('dgpc,gh->dgchp', m, eye).reshape(N_DIR, A_WIDTH, SSM_STATE)
    blk_c = lambda m: jnp.einsum('dgcp,gh->dgphc', m, eye).reshape(N_DIR, SSM_STATE, A_WIDTH)
    flat = lambda m: m.reshape(N_DIR, 1, SSM_STATE)
    return {
        'a_r': flat(a_bar.real), 'a_i': flat(a_bar.imag),
        'pw_r': pw.real.reshape(N_DIR, SSM_STEPS, SSM_STATE),
        'pw_i': pw.imag.reshape(N_DIR, SSM_STEPS, SSM_STATE),
        'b_blk': jnp.concatenate([blk_b(b_bar.real), blk_b(b_bar.imag)], axis=-1).astype(BF16),
        'c_r': blk_c(c_re).astype(BF16),
        'c_i': blk_c(-c_im).astype(BF16),
    }


def _rope_tables():
    rows = DEC_SEQ // GRID_W
    row = jnp.repeat(jnp.arange(rows, dtype=F32), GRID_W)
    col = jnp.tile(jnp.arange(GRID_W, dtype=F32), rows)
    n_freq = MLA_ROPE // 4
    inv = ROPE_BASE ** (-jnp.arange(n_freq, dtype=F32) / n_freq)
    ang = jnp.concatenate([row[:, None] * inv, col[:, None] * inv], axis=-1)
    cos, sin = jnp.cos(ang), jnp.sin(ang)
    half = MLA_ROPE // 2
    z = jnp.zeros((DEC_SEQ, HEAD_PAD), F32)
    cos_t = jnp.ones((DEC_SEQ, HEAD_PAD), F32)
    cos_t = cos_t.at[:, ROPE_LANE0:ROPE_LANE0 + half].set(cos).at[:, ROPE_LANE0 + half:ROPE_LANE0 + 2 * half].set(cos)
    sin_a = z.at[:, ROPE_LANE0:ROPE_LANE0 + half].set(-sin)
    sin_b = z.at[:, ROPE_LANE0 + half:ROPE_LANE0 + 2 * half].set(sin)
    lat = lambda tb: jnp.tile(tb, (DEC_BATCH, 1))
    ones = jnp.ones((N_PROMPT, HEAD_PAD), F32)
    zeros = jnp.zeros((N_PROMPT, HEAD_PAD), F32)
    return (jnp.concatenate([ones, lat(cos_t)]), jnp.concatenate([zeros, lat(sin_a)]),
            jnp.concatenate([zeros, lat(sin_b)]))


def _layer_weights(l, w_in, ssm_d, ssm_w_glu, q_a_norm, kv_a_norm, w_uq, w_ukv, q_norm, k_norm,
                   gmlp_v_norm, gmlp_w_s, gmlp_b_s, w_out_norm, w_out, ffn_w_up, ffn_conv_w,
                   ffn_conv_b, ffn_w_down):
    wi = w_in[l]
    w_in_a = jnp.concatenate([wi[:, OFF_SSM:OFF_KR], wi[:, OFF_GM:]], axis=1).astype(BF16)
    w_in_kr = jnp.pad(wi[:, OFF_KR:OFF_GM], ((0, 0), (ROPE_LANE0, HEAD_PAD - ROPE_LANE0 - MLA_ROPE))).astype(BF16)
    ukv = w_ukv[l].reshape(KV_RANK, MLA_HEADS, MLA_NOPE + MLA_V)
    return {
        'w_in_a': w_in_a, 'w_in_kr': w_in_kr,
        'q_a_norm': q_a_norm[l].reshape(1, Q_RANK), 'kv_a_norm': kv_a_norm[l].reshape(1, KV_RANK),
        'w_uq': _pad_heads(w_uq[l], MLA_QK, MLA_QK).astype(BF16),
        'w_uk': _pad_heads(ukv[..., :MLA_NOPE].reshape(KV_RANK, -1), MLA_NOPE, MLA_NOPE).astype(BF16),
        'w_uv': ukv[..., MLA_NOPE:].reshape(KV_RANK, B_WIDTH).astype(BF16),
        'q_norm': _head_gain(q_norm[l]), 'k_norm': _head_gain(k_norm[l]),
        'ssm_d': ssm_d[l].reshape(1, A_WIDTH), 'w_glu': ssm_w_glu[l].astype(BF16),
        'gmlp_v_norm': gmlp_v_norm[l].reshape(1, C_WIDTH), 'gmlp_w_s': gmlp_w_s[l].astype(BF16),
        'gmlp_bias': jnp.repeat(gmlp_b_s[l].T, GMLP_CH, axis=1),
        'w_out_norm': w_out_norm[l].reshape(1, D_MODEL), 'w_out': w_out[l].astype(BF16),
        'ffn_w_up': ffn_w_up[l].astype(BF16), 'ffn_conv_w': ffn_conv_w[l],
        'ffn_conv_b': ffn_conv_b[l].reshape(1, 2 * D_FF), 'ffn_w_down': ffn_w_down[l].astype(BF16),
    }


def kernel(x_prompt, x_sample, cache_ckv, cache_krope, state_ssm_re, state_ssm_im, c, c_ctx, w_mod, b_mod, w_in, ssm_a_re, ssm_a_im, ssm_b_re, ssm_b_im, ssm_c_re, ssm_c_im, ssm_log_dt, ssm_d, ssm_w_glu, q_a_norm, kv_a_norm, w_uq, w_ukv, q_norm, k_norm, gmlp_v_norm, gmlp_w_s, gmlp_b_s, w_out_norm, w_out, ffn_w_up, ffn_conv_w, ffn_conv_b, ffn_w_down):
    x = jnp.concatenate([x_prompt.reshape(N_PROMPT, D_MODEL), x_sample.reshape(N_SAMPLE, D_MODEL)], axis=0)
    cond8 = jnp.concatenate([c_ctx[None, :], c, jnp.zeros((8 - 1 - DEC_BATCH, D_MODEL), F32)], axis=0)
    mod = _modulation(cond8, w_mod, b_mod).reshape(DEPTH, 8, N_MOD, D_MODEL)
    rope_tabs = _rope_tables()

    ckv_out, kr_out, hre_out, him_out = [], [], [], []
    for l in range(DEPTH):
        wl = _layer_weights(l, w_in, ssm_d, ssm_w_glu, q_a_norm, kv_a_norm, w_uq, w_ukv, q_norm, k_norm,
                            gmlp_v_norm, gmlp_w_s, gmlp_b_s, w_out_norm, w_out, ffn_w_up, ffn_conv_w,
                            ffn_conv_b, ffn_w_down)
        tabs = _ssm_tables(ssm_a_re[l], ssm_a_im[l], ssm_b_re[l], ssm_b_im[l], ssm_c_re[l], ssm_c_im[l],
                           ssm_log_dt[l])
        mod_l = mod[l]

        u, gm, ckv, kr, q, k, v = _projection(x, mod_l, rope_tabs, wl)

        kr_c = jnp.pad(cache_krope[:, l].reshape(DEC_BATCH * PAST_LEN, MLA_ROPE),
                       ((0, 0), (ROPE_LANE0, HEAD_PAD - ROPE_LANE0 - MLA_ROPE)))
        k_c, v_c = _ctx_kv(cache_ckv[:, l].reshape(DEC_BATCH * PAST_LEN, KV_RANK), kr_c, wl)

        def h0(s):
            s = jnp.transpose(s[:, l].reshape(DEC_BATCH, N_DIR, SSM_STATE), (1, 0, 2))
            return jnp.pad(s, ((0, 0), (0, SSM_STREAMS - DEC_BATCH), (0, 0)))

        yssm, hf_r, hf_i = _ssm(u, h0(state_ssm_re), h0(state_ssm_im), tabs)

        yb_p = _attention(q, [k], [v], BATCH, SEQ, 0, [SEQ], [0])
        yb_s = _attention(q, [k_c, k], [v_c, v], DEC_BATCH, DEC_SEQ, N_PROMPT,
                          [PAST_LEN, DEC_SEQ], [0, N_PROMPT])
        yb = jnp.concatenate([yb_p, yb_s], axis=0)

        x1, h2 = _merge(x, mod_l, yssm, u, yb, gm, wl)
        x = _ffn(x1, mod_l, h2, wl)

        ckv_out.append(ckv[:N_PROMPT].reshape(BATCH, SEQ, KV_RANK))
        kr_out.append(kr[:N_PROMPT, ROPE_LANE0:ROPE_LANE0 + MLA_ROPE].reshape(BATCH, SEQ, MLA_ROPE))
        to_state = lambda s: jnp.transpose(s, (1, 0, 2)).reshape(BATCH, N_DIR, SSM_G, SSM_P)
        hre_out.append(to_state(hf_r))
        him_out.append(to_state(hf_i))

    y_prompt = x[:N_PROMPT].reshape(BATCH, SEQ, D_MODEL)
    y_sample = x[N_PROMPT:].reshape(DEC_BATCH, DEC_SEQ, D_MODEL)
    return (y_prompt, y_sample, jnp.stack(ckv_out, axis=1), jnp.stack(kr_out, axis=1),
            jnp.stack(hre_out, axis=1), jnp.stack(him_out, axis=1))
```

```python
import functools
import math

import jax
import jax.numpy as jnp
import numpy as np
from jax import lax
from jax.experimental import pallas as pl
from jax.experimental.pallas import tpu as pltpu

F32 = jnp.float32
BF16 = jnp.bfloat16

D_MODEL = 1024
BATCH = 16
SEQ = 256
DEPTH = 2
DEC_BATCH = 2
DEC_SEQ = 4096
PAST_LEN = 256
GRID_W = 64
EPS = 1e-6
N_MOD = 6
A_WIDTH = 256
B_WIDTH = 512
C_WIDTH = 256
SSM_CG = 16
SSM_G = 16
SSM_P = 64
N_DIR = 2
MLA_HEADS = 8
MLA_NOPE = 64
MLA_ROPE = 32
MLA_V = 64
MLA_QK = 96
Q_RANK = 256
KV_RANK = 128
ROPE_BASE = 10000.0
ATTN_SCALE = 1.0 / math.sqrt(MLA_QK)
Q_SCALE = ATTN_SCALE * math.log2(math.e)
GMLP_HEADS = 4
GMLP_CH = 64
CHUNK = 128
OFF_SSM = 0
OFF_Q = OFF_SSM + A_WIDTH
OFF_KV = OFF_Q + Q_RANK
OFF_KR = OFF_KV + KV_RANK
OFF_GM = OFF_KR + MLA_ROPE
D_FF = 2816

N_PROMPT = BATCH * SEQ
N_SAMPLE = DEC_BATCH * DEC_SEQ
N_TOK = N_PROMPT + N_SAMPLE
SSM_STATE = SSM_G * SSM_P

LANES = 128
HEAD_PAD = LANES
ROPE_LANE0 = MLA_NOPE

PROJ_TILE = 512
SSM_STREAMS = 8
SSM_STEPS = 256
SSM_UNIT = SSM_STREAMS * SSM_STEPS
ATTN_TQ = 256
ATTN_KC = 512
FFN_ROWS = SEQ
FFN_FC = 256
FFN_HALO = 16
VMEM_LIMIT = 48 * 1024 * 1024


def _mod_row(tile, tile_tokens):
    start = tile * tile_tokens
    return jnp.where(start < N_PROMPT, 0, 1 + (start - N_PROMPT) // DEC_SEQ)


def _rms(x):
    return x * lax.rsqrt(jnp.mean(x * x, axis=-1, keepdims=True) + EPS)


def _sigmoid(x):
    return 1.0 / (1.0 + jnp.exp(-x))


def _gelu_tanh(x):
    return 0.5 * x * (1.0 + jnp.tanh(math.sqrt(2.0 / math.pi) * (x + 0.044715 * (x * x * x))))


def _dot(a, b):
    return jnp.dot(a, b, preferred_element_type=F32)


def _mod_kernel(cond_ref, w_ref, b_ref, o_ref):
    cond = cond_ref[...]
    s = (cond * _sigmoid(cond)).astype(BF16)
    o_ref[0] = _dot(s, w_ref[0].astype(BF16)) + b_ref[0]


def _modulation(cond8, w_mod, b_mod):
    tn = 1536
    n_cols = N_MOD * D_MODEL
    return pl.pallas_call(
        _mod_kernel,
        grid=(DEPTH, n_cols // tn),
        in_specs=[
            pl.BlockSpec((8, D_MODEL), lambda l, j: (0, 0)),
            pl.BlockSpec((1, D_MODEL, tn), lambda l, j: (l, 0, j)),
            pl.BlockSpec((1, 1, tn), lambda l, j: (l, 0, j)),
        ],
        out_specs=pl.BlockSpec((1, 8, tn), lambda l, j: (l, 0, j)),
        out_shape=jax.ShapeDtypeStruct((DEPTH, 8, n_cols), F32),
        compiler_params=pltpu.CompilerParams(vmem_limit_bytes=VMEM_LIMIT),
        name="modulation",
    )(cond8, w_mod, b_mod.reshape(DEPTH, 1, n_cols))


def _write_heads(allh, extra, gain, partner, out_ref):
    for h in range(MLA_HEADS):
        tile = slice(h * HEAD_PAD, (h + 1) * HEAD_PAD)
        t = allh[:, tile]
        if extra is not None:
            t = t + extra
        rs = lax.rsqrt(jnp.sum(t * t, axis=-1, keepdims=True) * (1.0 / MLA_QK) + EPS)
        o = t * gain
        if isinstance(partner, tuple):
            o = o + partner[0][:, tile] * partner[1]
        elif partner is not None:
            o = o + partner
        out_ref[h] = (o * rs).astype(BF16)


def _write_values(vall, v_ref):
    lane = lax.broadcasted_iota(jnp.int32, (1, HEAD_PAD), 1)
    ones = jnp.where(lane >= MLA_V, 1.0, 0.0)
    for h in range(MLA_HEADS):
        v_ref[h] = (vall[:, h * HEAD_PAD:(h + 1) * HEAD_PAD] + ones).astype(BF16)


def _proj_kernel(x_ref, mod_ref, cos_ref, sin_ref, w_in_ref, w_kr_ref, w_krp_ref, qan_ref, kvn_ref,
                 w_uq_ref, w_uqp_ref, w_uk_ref, w_uv_ref, qn_ref, qnp_ref, kn_ref, knp_ref,
                 u_ref, gm_ref, ckv_ref, kr_ref, q_ref, k_ref, v_ref):
    x = x_ref[...]
    shift = mod_ref[0, 0:1, :]
    scale = mod_ref[0, 1:2, :]
    h = (_rms(x) * (1.0 + scale) + shift).astype(BF16)
    z = _dot(h, w_in_ref[...])
    kr = _dot(h, w_kr_ref[...])
    u_ref[...] = z[:, 0:A_WIDTH]
    gm_ref[...] = z[:, A_WIDTH + Q_RANK + KV_RANK:]
    kr_ref[...] = kr
    ckv = _rms(z[:, A_WIDTH + Q_RANK:A_WIDTH + Q_RANK + KV_RANK]) * kvn_ref[...]
    ckv_ref[...] = ckv
    cq = (_rms(z[:, A_WIDTH:A_WIDTH + Q_RANK]) * qan_ref[...]).astype(BF16)
    ckv_b = ckv.astype(BF16)
    qall = _dot(cq, w_uq_ref[...])
    kall = _dot(ckv_b, w_uk_ref[...])
    _write_values(_dot(ckv_b, w_uv_ref[...]), v_ref)
    latent = pl.program_id(0) >= N_PROMPT // PROJ_TILE

    @pl.when(jnp.logical_not(latent))
    def _():
        _write_heads(qall, None, qn_ref[...], None, q_ref)
        _write_heads(kall, kr, kn_ref[...], None, k_ref)

    @pl.when(latent)
    def _():
        cos = cos_ref[...]
        sin = sin_ref[...]
        _write_heads(qall, None, qn_ref[...] * cos, (_dot(cq, w_uqp_ref[...]), qnp_ref[...] * sin), q_ref)
        _write_heads(kall, kr, kn_ref[...] * cos, _dot(h, w_krp_ref[...]) * (knp_ref[...] * sin), k_ref)


def _projection(x, mod_l, rope_tabs, wl):
    t = PROJ_TILE
    n_tiles = N_TOK // t
    tok = lambda w: pl.BlockSpec((t, w), lambda i: (i, 0))
    full = lambda a: pl.BlockSpec(a.shape, lambda i: (0,) * a.ndim)
    heads = pl.BlockSpec((MLA_HEADS, t, HEAD_PAD), lambda i: (0, i, 0))
    pos = pl.BlockSpec((t, HEAD_PAD), lambda i: (jnp.maximum(i - N_PROMPT // t, 0) % (DEC_SEQ // t), 0))
    weights = [wl['w_in_a'], wl['w_in_kr'], wl['w_in_krp'], wl['q_a_norm'], wl['kv_a_norm'], wl['w_uq'],
               wl['w_uqp'], wl['w_uk'], wl['w_uv'], wl['q_norm'], wl['q_norm_p'], wl['k_norm'], wl['k_norm_p']]
    head_shape = jax.ShapeDtypeStruct((MLA_HEADS, N_TOK, HEAD_PAD), BF16)
    return pl.pallas_call(
        _proj_kernel,
        grid=(n_tiles,),
        in_specs=[tok(D_MODEL),
                  pl.BlockSpec((1, N_MOD, D_MODEL), lambda i: (_mod_row(i, t), 0, 0)),
                  pos, pos] + [full(w) for w in weights],
        out_specs=[tok(A_WIDTH), tok(2 * C_WIDTH), tok(KV_RANK), tok(HEAD_PAD), heads, heads, heads],
        out_shape=[
            jax.ShapeDtypeStruct((N_TOK, A_WIDTH), F32),
            jax.ShapeDtypeStruct((N_TOK, 2 * C_WIDTH), F32),
            jax.ShapeDtypeStruct((N_TOK, KV_RANK), F32),
            jax.ShapeDtypeStruct((N_TOK, HEAD_PAD), F32),
            head_shape, head_shape, head_shape,
        ],
        compiler_params=pltpu.CompilerParams(vmem_limit_bytes=VMEM_LIMIT),
        name="projection",
    )(x, mod_l, *rope_tabs, *weights)


def _ctx_kv_kernel(ckv_ref, kr_ref, w_uk_ref, w_uv_ref, kn_ref, k_ref, v_ref):
    ckv_b = ckv_ref[...].astype(BF16)
    _write_heads(_dot(ckv_b, w_uk_ref[...]), kr_ref[...], kn_ref[...], None, k_ref)
    _write_values(_dot(ckv_b, w_uv_ref[...]), v_ref)


def _ctx_kv(ckv, kr128, wl):
    n = ckv.shape[0]
    full = lambda a: pl.BlockSpec(a.shape, lambda i: (0,) * a.ndim)
    weights = [wl['w_uk'], wl['w_uv'], wl['k_norm']]
    heads = pl.BlockSpec((MLA_HEADS, n, HEAD_PAD), lambda i: (0, 0, 0))
    head_shape = jax.ShapeDtypeStruct((MLA_HEADS, n, HEAD_PAD), BF16)
    return pl.pallas_call(
        _ctx_kv_kernel,
        grid=(1,),
        in_specs=[full(ckv), full(kr128)] + [full(w) for w in weights],
        out_specs=[heads, heads],
        out_shape=[head_shape, head_shape],
        compiler_params=pltpu.CompilerParams(vmem_limit_bytes=VMEM_LIMIT),
        name="context_kv",
    )(ckv, kr128, *weights)


def _ssm_block(d, k):
    return jnp.where((d == 1) & (k >= 2), k ^ 1, k)


def _ssm_kernel(u_ref, h0r_ref, h0i_ref, ar_ref, ai_ref, pwr_ref, pwi_ref, bm_ref, cr_ref, ci_ref,
                y_ref, hfr_ref, hfi_ref, sre, sim, hin_r, hin_i, car_r, car_i):
    d = pl.program_id(0)
    k = pl.program_id(1)
    rows = 256

    @pl.when(k == 0)
    def _():
        car_r[...] = jnp.zeros_like(car_r)
        car_i[...] = jnp.zeros_like(car_i)

    for c in range(SSM_UNIT // rows):
        sl = slice(c * rows, (c + 1) * rows)
        bu = _dot(u_ref[sl, :].astype(BF16), bm_ref[0])
        sre[sl, :] = bu[:, :SSM_STATE]
        sim[sl, :] = bu[:, SSM_STATE:]

    a_r = jnp.broadcast_to(ar_ref[0], (SSM_STREAMS, SSM_STATE))
    a_i = jnp.broadcast_to(ai_ref[0], (SSM_STREAMS, SSM_STATE))

    def stream_rows(tau):
        return pl.ds(pl.multiple_of(tau * SSM_STREAMS, SSM_STREAMS), SSM_STREAMS)

    def scan_step(i, carry):
        h_r, h_i = carry
        tau = jnp.where(d == 0, i, SSM_STEPS - 1 - i)
        n_r = a_r * h_r - a_i * h_i + sre[stream_rows(tau), :]
        n_i = a_r * h_i + a_i * h_r + sim[stream_rows(tau), :]
        sre[stream_rows(tau), :] = n_r
        sim[stream_rows(tau), :] = n_i
        return n_r, n_i

    zero = jnp.zeros((SSM_STREAMS, SSM_STATE), F32)
    end_r, end_i = lax.fori_loop(0, SSM_STEPS, scan_step, (zero, zero), unroll=2)

    @pl.when(k < 2)
    def _():
        hfr_ref[0] = end_r
        hfi_ref[0] = end_i

    @pl.when(k >= 2)
    def _():
        b = (k - 2) // 2
        first = (k % 2) == 0
        c_r = jnp.where(first, h0r_ref[0, pl.ds(b, 1), :], car_r[...])
        c_i = jnp.where(first, h0i_ref[0, pl.ds(b, 1), :], car_i[...])
        full_r = pwr_ref[0, SSM_STEPS - 1:SSM_STEPS, :]
        full_i = pwi_ref[0, SSM_STEPS - 1:SSM_STEPS, :]

        def chain(order):
            cr, ci = c_r, c_i
            for j in order:
                hin_r[j:j + 1, :] = cr
                hin_i[j:j + 1, :] = ci
                nr = end_r[j:j + 1, :] + full_r * cr - full_i * ci
                ni = end_i[j:j + 1, :] + full_r * ci + full_i * cr
                cr, ci = nr, ni
            car_r[...] = cr
            car_i[...] = ci

        @pl.when(d == 0)
        def _():
            chain(range(SSM_STREAMS))

        @pl.when(d == 1)
        def _():
            chain(range(SSM_STREAMS - 1, -1, -1))

        g_r = hin_r[...]
        g_i = hin_i[...]

        def fix_step(i, carry):
            p = jnp.where(d == 0, i, SSM_STEPS - 1 - i)
            p_r = jnp.broadcast_to(pwr_ref[0, pl.ds(p, 1), :], (SSM_STREAMS, SSM_STATE))
            p_i = jnp.broadcast_to(pwi_ref[0, pl.ds(p, 1), :], (SSM_STREAMS, SSM_STATE))
            sr = stream_rows(i)
            sre[sr, :] = sre[sr, :] + (p_r * g_r - p_i * g_i)
            sim[sr, :] = sim[sr, :] + (p_r * g_i + p_i * g_r)
            return carry

        lax.fori_loop(0, SSM_STEPS, fix_step, 0, unroll=2)

    for c in range(SSM_UNIT // rows):
        sl = slice(c * rows, (c + 1) * rows)
        y_ref[0, sl, :] = (_dot(sre[sl, :].astype(BF16), cr_ref[0])
                           + _dot(sim[sl, :].astype(BF16), ci_ref[0]))


def _step_major(x, inverse=False):
    lead = x.shape[:-2]
    a, b = (SSM_STEPS, SSM_STREAMS) if inverse else (SSM_STREAMS, SSM_STEPS)
    x = x.reshape(lead + (N_TOK // SSM_UNIT, a, b, x.shape[-1]))
    return jnp.swapaxes(x, -3, -2).reshape(lead + (N_TOK, x.shape[-1]))


def _ssm(u, h0r, h0i, tabs):
    n_units = N_TOK // SSM_UNIT
    u = _step_major(u)
    per_dir = lambda a: pl.BlockSpec((1,) + a.shape[1:], lambda d, k: (d,) + (0,) * (a.ndim - 1))
    y, hf_r, hf_i = pl.pallas_call(
        _ssm_kernel,
        grid=(N_DIR, n_units),
        in_specs=[pl.BlockSpec((SSM_UNIT, A_WIDTH), lambda d, k: (_ssm_block(d, k), 0)),
                  per_dir(h0r), per_dir(h0i)] + [per_dir(tabs[n]) for n in
                                                 ('a_r', 'a_i', 'pw_r', 'pw_i', 'b_blk', 'c_r', 'c_i')],
        out_specs=[pl.BlockSpec((1, SSM_UNIT, A_WIDTH), lambda d, k: (d, _ssm_block(d, k), 0)),
                   pl.BlockSpec((1, SSM_STREAMS, SSM_STATE), lambda d, k: (d, jnp.minimum(k, 1), 0)),
                   pl.BlockSpec((1, SSM_STREAMS, SSM_STATE), lambda d, k: (d, jnp.minimum(k, 1), 0))],
        out_shape=[jax.ShapeDtypeStruct((N_DIR, N_TOK, A_WIDTH), F32),
                   jax.ShapeDtypeStruct((N_DIR, BATCH, SSM_STATE), F32),
                   jax.ShapeDtypeStruct((N_DIR, BATCH, SSM_STATE), F32)],
        scratch_shapes=[pltpu.VMEM((SSM_UNIT, SSM_STATE), F32), pltpu.VMEM((SSM_UNIT, SSM_STATE), F32),
                        pltpu.VMEM((SSM_STREAMS, SSM_STATE), F32), pltpu.VMEM((SSM_STREAMS, SSM_STATE), F32),
                        pltpu.VMEM((1, SSM_STATE), F32), pltpu.VMEM((1, SSM_STATE), F32)],
        compiler_params=pltpu.CompilerParams(vmem_limit_bytes=VMEM_LIMIT,
                                             dimension_semantics=("arbitrary", "arbitrary")),
        name="ssm_scan",
    )(u, h0r, h0i, tabs['a_r'], tabs['a_i'], tabs['pw_r'], tabs['pw_i'], tabs['b_blk'], tabs['c_r'],
      tabs['c_i'])
    return _step_major(y, inverse=True), hf_r, hf_i


def _attn_kernel(*refs, key_lens):
    n_src = len(key_lens)
    q_ref = refs[0]
    k_refs = refs[1:1 + n_src]
    v_refs = refs[1 + n_src:1 + 2 * n_src]
    o_ref = refs[1 + 2 * n_src]
    s_ref = refs[2 + 2 * n_src]
    chunks = []
    off = 0
    for si, n in enumerate(key_lens):
        for c0 in range(0, n, ATTN_KC):
            cl = min(ATTN_KC, n - c0)
            chunks.append((si, c0, cl, off))
            off += cl
    outs = []
    for hh in range(2):
        q = q_ref[hh]
        mp = jnp.full((ATTN_TQ, LANES), -jnp.inf, F32)
        for si, c0, cl, o in chunks:
            s = lax.dot_general(q, k_refs[si][hh, c0:c0 + cl, :], (((1,), (1,)), ((), ())),
                                preferred_element_type=F32)
            s_ref[:, o:o + cl] = s
            for g in range(cl // LANES):
                mp = jnp.maximum(mp, s[:, g * LANES:(g + 1) * LANES])
        m = jnp.max(mp, axis=-1, keepdims=True)
        acc = jnp.zeros((ATTN_TQ, LANES), F32)
        for si, c0, cl, o in chunks:
            p = jnp.exp2(s_ref[:, o:o + cl] - m)
            acc = acc + _dot(p.astype(BF16), v_refs[si][hh, c0:c0 + cl, :])
        outs.append(acc / pltpu.roll(acc, MLA_V, 1))
    lane = lax.broadcasted_iota(jnp.int32, (ATTN_TQ, LANES), 1)
    o_ref[...] = jnp.where(lane < MLA_V, outs[0], pltpu.roll(outs[1], MLA_V, 1))


def _attention(q, ks, vs, n_seq, seq_len, tok0, key_lens, key_tok0):
    n_qt = seq_len // ATTN_TQ
    q0 = tok0 // ATTN_TQ
    in_specs = [pl.BlockSpec((2, ATTN_TQ, HEAD_PAD), lambda b, hp, qi: (hp, q0 + b * n_qt + qi, 0))]
    for n, t0 in zip(key_lens, key_tok0):
        in_specs.append(pl.BlockSpec((2, n, HEAD_PAD), lambda b, hp, qi, n=n, t0=t0: (hp, t0 // n + b, 0)))
    for n, t0 in zip(key_lens, key_tok0):
        in_specs.append(pl.BlockSpec((2, n, HEAD_PAD), lambda b, hp, qi, n=n, t0=t0: (hp, t0 // n + b, 0)))
    return pl.pallas_call(
        functools.partial(_attn_kernel, key_lens=tuple(key_lens)),
        grid=(n_seq, MLA_HEADS // 2, n_qt),
        in_specs=in_specs,
        out_specs=pl.BlockSpec((ATTN_TQ, LANES), lambda b, hp, qi: (b * n_qt + qi, hp)),
        out_shape=jax.ShapeDtypeStruct((n_seq * seq_len, B_WIDTH), F32),
        scratch_shapes=[pltpu.VMEM((ATTN_TQ, sum(key_lens)), F32)],
        compiler_params=pltpu.CompilerParams(vmem_limit_bytes=VMEM_LIMIT,
                                             dimension_semantics=("arbitrary", "arbitrary", "arbitrary")),
        name="attention",
    )(q, *ks, *vs)


def _merge_kernel(x_ref, mod_ref, yssm_ref, u_ref, yb_ref, gm_ref, dskip_ref, w_glu_ref, vn_ref,
                  w_s_ref, bias_ref, g_ref, w_out_ref, x1_ref, h2_ref):
    t = x_ref.shape[0]
    u = u_ref[...]
    y = yssm_ref[0] + yssm_ref[1] + dskip_ref[...] * u
    y = _gelu_tanh(y)
    y_a = y * _sigmoid(_dot(y.astype(BF16), w_glu_ref[...]))
    gm = gm_ref[...]
    vn = (_rms(gm[:, C_WIDTH:]) * vn_ref[...]).astype(BF16)
    lane = lax.broadcasted_iota(jnp.int32, (CHUNK, C_WIDTH), 1)
    mixed = []
    for c in range(t // CHUNK):
        vc = vn[c * CHUNK:(c + 1) * CHUNK, :]
        m = _dot(w_s_ref[0], vc)
        for h in range(1, GMLP_HEADS):
            m = jnp.where(lane >= h * GMLP_CH, _dot(w_s_ref[h], vc), m)
        mixed.append(m + bias_ref[...])
    y_c = gm[:, :C_WIDTH] * jnp.concatenate(mixed, axis=0)
    g = g_ref[...]
    n_a = (_rms(y_a) * g[:, :A_WIDTH]).astype(BF16)
    n_b = (_rms(yb_ref[...]) * g[:, A_WIDTH:A_WIDTH + B_WIDTH]).astype(BF16)
    n_c = (_rms(y_c) * g[:, A_WIDTH + B_WIDTH:]).astype(BF16)
    o = (_dot(n_a, w_out_ref[0:A_WIDTH, :]) + _dot(n_b, w_out_ref[A_WIDTH:A_WIDTH + B_WIDTH, :])
         + _dot(n_c, w_out_ref[A_WIDTH + B_WIDTH:, :]))
    x1 = x_ref[...] + mod_ref[0, 2:3, :] * o
    x1_ref[...] = x1
    h2_ref[...] = (_rms(x1) * (1.0 + mod_ref[0, 4:5, :]) + mod_ref[0, 3:4, :]).astype(BF16)


def _merge(x, mod_l, yssm, u, yb, gm, wl):
    t = PROJ_TILE
    tok = lambda w: pl.BlockSpec((t, w), lambda i: (i, 0))
    full = lambda a: pl.BlockSpec(a.shape, lambda i: (0,) * a.ndim)
    weights = [wl['ssm_d'], wl['w_glu'], wl['gmlp_v_norm'], wl['gmlp_w_s'], wl['gmlp_bias'],
               wl['w_out_norm'], wl['w_out']]
    return pl.pallas_call(
        _merge_kernel,
        grid=(N_TOK // t,),
        in_specs=[tok(D_MODEL),
                  pl.BlockSpec((1, N_MOD, D_MODEL), lambda i: (_mod_row(i, t), 0, 0)),
                  pl.BlockSpec((N_DIR, t, A_WIDTH), lambda i: (0, i, 0)),
                  tok(A_WIDTH), tok(B_WIDTH), tok(2 * C_WIDTH)] + [full(w) for w in weights],
        out_specs=[tok(D_MODEL), tok(D_MODEL)],
        out_shape=[jax.ShapeDtypeStruct((N_TOK, D_MODEL), F32),
                   jax.ShapeDtypeStruct((N_TOK, D_MODEL), BF16)],
        compiler_params=pltpu.CompilerParams(vmem_limit_bytes=VMEM_LIMIT),
        name="merge",
    )(x, mod_l, yssm, u, yb, gm, *weights)


def _ffn_kernel(x1_ref, mod_ref, hp_ref, hm_ref, hn_ref, wup_ref, cw_ref, cb_ref, wd_ref, o_ref, a_scr):
    seg = pl.program_id(0) - N_PROMPT // FFN_ROWS
    per_seq = DEC_SEQ // FFN_ROWS
    starts = (seg < 0) | ((seg & (per_seq - 1)) == 0)
    ends = (seg < 0) | ((seg & (per_seq - 1)) == per_seq - 1)
    h_prev = hp_ref[...]
    h_next = hn_ref[...]
    h_prev = jnp.where(starts, jnp.zeros_like(h_prev), h_prev)
    h_next = jnp.where(ends, jnp.zeros_like(h_next), h_next)
    he = jnp.concatenate([h_prev, hm_ref[...], h_next], axis=0)
    r0 = FFN_HALO

    def conv(u, c0):
        cols = slice(c0, c0 + FFN_FC)
        return (u[r0 - 1:r0 - 1 + FFN_ROWS] * cw_ref[0:1, cols] + u[r0:r0 + FFN_ROWS] * cw_ref[1:2, cols]
                + u[r0 + 1:r0 + 1 + FFN_ROWS] * cw_ref[2:3, cols] + cb_ref[:, cols])

    for j in range(D_FF // FFN_FC):
        cg = j * FFN_FC
        cv = D_FF + j * FFN_FC
        gate = conv(_dot(he, wup_ref[:, cg:cg + FFN_FC]), cg)
        val = conv(_dot(he, wup_ref[:, cv:cv + FFN_FC]), cv)
        a_scr[:, cg:cg + FFN_FC] = (gate * _sigmoid(gate) * val).astype(BF16)

    o_ref[...] = x1_ref[...] + mod_ref[0, 5:6, :] * _dot(a_scr[...], wd_ref[...])


def _ffn(x1, mod_l, h2, wl):
    t = FFN_ROWS
    hb = t // FFN_HALO
    last_halo = N_TOK // FFN_HALO - 1
    resident = lambda a: pl.BlockSpec(a.shape, lambda i: (0,) * a.ndim, pipeline_mode=pl.Buffered(1))
    weights = [wl['ffn_w_up'], wl['ffn_conv_w'], wl['ffn_conv_b'], wl['ffn_w_down']]
    return pl.pallas_call(
        _ffn_kernel,
        grid=(N_TOK // t,),
        in_specs=[pl.BlockSpec((t, D_MODEL), lambda i: (i, 0)),
                  pl.BlockSpec((1, N_MOD, D_MODEL), lambda i: (_mod_row(i, t), 0, 0)),
                  pl.BlockSpec((FFN_HALO, D_MODEL), lambda i: (jnp.maximum(i * hb - 1, 0), 0)),
                  pl.BlockSpec((t, D_MODEL), lambda i: (i, 0)),
                  pl.BlockSpec((FFN_HALO, D_MODEL), lambda i: (jnp.minimum((i + 1) * hb, last_halo), 0))]
                 + [resident(w) for w in weights],
        out_specs=pl.BlockSpec((t, D_MODEL), lambda i: (i, 0)),
        out_shape=jax.ShapeDtypeStruct((N_TOK, D_MODEL), F32),
        scratch_shapes=[pltpu.VMEM((t, D_FF), BF16)],
        compiler_params=pltpu.CompilerParams(vmem_limit_bytes=VMEM_LIMIT,
                                             dimension_semantics=("arbitrary",)),
        name="conv_ffn",
    )(x1, mod_l, h2, h2, h2, *weights)


def _pad_heads(w, head_w, used, offset=0):
    lead = w.shape[:-1]
    w = w.reshape(lead + (MLA_HEADS, head_w))[..., :used]
    pad = [(0, 0)] * (len(lead) + 1) + [(offset, HEAD_PAD - used - offset)]
    return jnp.pad(w, pad).reshape(lead + (MLA_HEADS * HEAD_PAD,))


def _head_gain(g):
    return jnp.pad(g, (0, HEAD_PAD - MLA_QK)).reshape(1, HEAD_PAD)


def _ssm_tables(a_re, a_im, b_re, b_im, c_re, c_im, log_dt):
    a = lax.complex(a_re, a_im)
    dt = jnp.exp(log_dt)[..., None]
    a_bar = jnp.exp(a * dt)
    b_bar = ((a_bar - 1.0) / a)[..., None] * lax.complex(b_re, b_im)
    steps = jnp.arange(1, SSM_STEPS + 1, dtype=F32)[None, :, None, None]
    pw = jnp.exp((a * dt)[:, None] * steps)
    eye = jnp.eye(SSM_G, dtype=F32)
    blk_b = lambda m: jnp.einsum('dgpc,gh->dgchp', m, eye).reshape(N_DIR, A_WIDTH, SSM_STATE)
    blk_c = lambda m: jnp.einsum('dgcp,gh->dgphc', m, eye).reshape(N_DIR, SSM_STATE, A_WIDTH)
    flat = lambda m: m.reshape(N_DIR, 1, SSM_STATE)
    return {
        'a_r': flat(a_bar.real), 'a_i': flat(a_bar.imag),
        'pw_r': pw.real.reshape(N_DIR, SSM_STEPS, SSM_STATE),
        'pw_i': pw.imag.reshape(N_DIR, SSM_STEPS, SSM_STATE),
        'b_blk': jnp.concatenate([blk_b(b_bar.real), blk_b(b_bar.imag)], axis=-1).astype(BF16),
        'c_r': blk_c(c_re).astype(BF16),
        'c_i': blk_c(-c_im).astype(BF16),
    }


def _rope_tables():
    rows = DEC_SEQ // GRID_W
    row = np.repeat(np.arange(rows, dtype=np.float32), GRID_W)
    col = np.tile(np.arange(GRID_W, dtype=np.float32), rows)
    n_freq = MLA_ROPE // 4
    inv = (np.float32(ROPE_BASE) ** (-np.arange(n_freq, dtype=np.float32) / np.float32(n_freq))).astype(np.float32)
    ang = np.concatenate([row[:, None] * inv, col[:, None] * inv], axis=-1).astype(np.float32)
    cos, sin = np.cos(ang), np.sin(ang)
    cos_t = np.ones((DEC_SEQ, HEAD_PAD), np.float32)
    sin_t = np.zeros((DEC_SEQ, HEAD_PAD), np.float32)
    cos_t[:, ROPE_LANE0:ROPE_LANE0 + MLA_ROPE] = np.concatenate([cos, cos], axis=-1)
    sin_t[:, ROPE_LANE0:ROPE_LANE0 + MLA_ROPE] = np.concatenate([sin, sin], axis=-1)
    return jnp.asarray(cos_t), jnp.asarray(sin_t)


def _rope_partner(r):
    half = MLA_ROPE // 2
    return jnp.concatenate([-r[..., half:], r[..., :half]], axis=-1)


def _rope_lanes(r):
    return jnp.pad(r, [(0, 0)] * (r.ndim - 1) + [(ROPE_LANE0, HEAD_PAD - ROPE_LANE0 - MLA_ROPE)])


def _head_gain_partner(g):
    half = MLA_ROPE // 2
    r = g[MLA_NOPE:]
    return _rope_lanes(jnp.concatenate([r[half:], r[:half]])).reshape(1, HEAD_PAD)


def _layer_weights(l, w_in, ssm_d, ssm_w_glu, q_a_norm, kv_a_norm, w_uq, w_ukv, q_norm, k_norm,
                   gmlp_v_norm, gmlp_w_s, gmlp_b_s, w_out_norm, w_out, ffn_w_up, ffn_conv_w,
                   ffn_conv_b, ffn_w_down):
    wi = w_in[l]
    w_in_a = jnp.concatenate([wi[:, OFF_SSM:OFF_KR], wi[:, OFF_GM:]], axis=1).astype(BF16)
    w_in_kr = _rope_lanes(wi[:, OFF_KR:OFF_GM]).astype(BF16)
    w_in_krp = _rope_lanes(_rope_partner(wi[:, OFF_KR:OFF_GM])).astype(BF16)
    ukv = w_ukv[l].reshape(KV_RANK, MLA_HEADS, MLA_NOPE + MLA_V)
    uq = w_uq[l].reshape(Q_RANK, MLA_HEADS, MLA_QK)
    w_uqp = _rope_lanes(_rope_partner(uq[..., MLA_NOPE:])).reshape(Q_RANK, MLA_HEADS * HEAD_PAD).astype(BF16)
    q_gain = q_norm[l] * Q_SCALE
    return {
        'w_in_a': w_in_a, 'w_in_kr': w_in_kr, 'w_in_krp': w_in_krp, 'w_uqp': w_uqp,
        'q_a_norm': q_a_norm[l].reshape(1, Q_RANK), 'kv_a_norm': kv_a_norm[l].reshape(1, KV_RANK),
        'w_uq': _pad_heads(w_uq[l], MLA_QK, MLA_QK).astype(BF16),
        'w_uk': _pad_heads(ukv[..., :MLA_NOPE].reshape(KV_RANK, -1), MLA_NOPE, MLA_NOPE).astype(BF16),
        'w_uv': _pad_heads(ukv[..., MLA_NOPE:].reshape(KV_RANK, B_WIDTH), MLA_V, MLA_V).astype(BF16),
        'q_norm': _head_gain(q_gain), 'q_norm_p': _head_gain_partner(q_gain),
        'k_norm': _head_gain(k_norm[l]), 'k_norm_p': _head_gain_partner(k_norm[l]),
        'ssm_d': ssm_d[l].reshape(1, A_WIDTH), 'w_glu': ssm_w_glu[l].astype(BF16),
        'gmlp_v_norm': gmlp_v_norm[l].reshape(1, C_WIDTH), 'gmlp_w_s': gmlp_w_s[l].astype(BF16),
        'gmlp_bias': jnp.repeat(gmlp_b_s[l].T, GMLP_CH, axis=1),
        'w_out_norm': w_out_norm[l].reshape(1, D_MODEL), 'w_out': w_out[l].astype(BF16),
        'ffn_w_up': ffn_w_up[l].astype(BF16), 'ffn_conv_w': ffn_conv_w[l],
        'ffn_conv_b': ffn_conv_b[l].reshape(1, 2 * D_FF), 'ffn_w_down': ffn_w_down[l].astype(BF16),
    }


def kernel(x_prompt, x_sample, cache_ckv, cache_krope, state_ssm_re, state_ssm_im, c, c_ctx, w_mod, b_mod, w_in, ssm_a_re, ssm_a_im, ssm_b_re, ssm_b_im, ssm_c_re, ssm_c_im, ssm_log_dt, ssm_d, ssm_w_glu, q_a_norm, kv_a_norm, w_uq, w_ukv, q_norm, k_norm, gmlp_v_norm, gmlp_w_s, gmlp_b_s, w_out_norm, w_out, ffn_w_up, ffn_conv_w, ffn_conv_b, ffn_w_down):
    x = jnp.concatenate([x_prompt.reshape(N_PROMPT, D_MODEL), x_sample.reshape(N_SAMPLE, D_MODEL)], axis=0)
    cond8 = jnp.concatenate([c_ctx[None, :], c, jnp.zeros((8 - 1 - DEC_BATCH, D_MODEL), F32)], axis=0)
    mod = _modulation(cond8, w_mod, b_mod).reshape(DEPTH, 8, N_MOD, D_MODEL)
    rope_tabs = _rope_tables()

    ckv_out, kr_out, hre_out, him_out = [], [], [], []
    for l in range(DEPTH):
        wl = _layer_weights(l, w_in, ssm_d, ssm_w_glu, q_a_norm, kv_a_norm, w_uq, w_ukv, q_norm, k_norm,
                            gmlp_v_norm, gmlp_w_s, gmlp_b_s, w_out_norm, w_out, ffn_w_up, ffn_conv_w,
                            ffn_conv_b, ffn_w_down)
        tabs = _ssm_tables(ssm_a_re[l], ssm_a_im[l], ssm_b_re[l], ssm_b_im[l], ssm_c_re[l], ssm_c_im[l],
                           ssm_log_dt[l])
        mod_l = mod[l]

        u, gm, ckv, kr, q, k, v = _projection(x, mod_l, rope_tabs, wl)

        kr_c = _rope_lanes(cache_krope[:, l].reshape(DEC_BATCH * PAST_LEN, MLA_ROPE))
        k_c, v_c = _ctx_kv(cache_ckv[:, l].reshape(DEC_BATCH * PAST_LEN, KV_RANK), kr_c, wl)

        def h0(s):
            s = jnp.transpose(s[:, l].reshape(DEC_BATCH, N_DIR, SSM_STATE), (1, 0, 2))
            return jnp.pad(s, ((0, 0), (0, SSM_STREAMS - DEC_BATCH), (0, 0)))

        yssm, hf_r, hf_i = _ssm(u, h0(state_ssm_re), h0(state_ssm_im), tabs)

        yb_p = _attention(q, [k], [v], BATCH, SEQ, 0, [SEQ], [0])
        yb_s = _attention(q, [k_c, k], [v_c, v], DEC_BATCH, DEC_SEQ, N_PROMPT,
                          [PAST_LEN, DEC_SEQ], [0, N_PROMPT])
        yb = jnp.concatenate([yb_p, yb_s], axis=0)

        x1, h2 = _merge(x, mod_l, yssm, u, yb, gm, wl)
        x = _ffn(x1, mod_l, h2, wl)

        ckv_out.append(ckv[:N_PROMPT].reshape(BATCH, SEQ, KV_RANK))
        kr_out.append(kr[:N_PROMPT, ROPE_LANE0:ROPE_LANE0 + MLA_ROPE].reshape(BATCH, SEQ, MLA_ROPE))
        to_state = lambda s: jnp.transpose(s, (1, 0, 2)).reshape(BATCH, N_DIR, SSM_G, SSM_P)
        hre_out.append(to_state(hf_r))
        him_out.append(to_state(hf_i))

    y_prompt = x[:N_PROMPT].reshape(BATCH, SEQ, D_MODEL)
    y_sample = x[N_PROMPT:].reshape(DEC_BATCH, DEC_SEQ, D_MODEL)
    return (y_prompt, y_sample, jnp.stack(ckv_out, axis=1), jnp.stack(kr_out, axis=1),
            jnp.stack(hre_out, axis=1), jnp.stack(him_out, axis=1))
```

```python
import functools
import math

import jax
import jax.numpy as jnp
import numpy as np
from jax import lax
from jax.experimental import pallas as pl
from jax.experimental.pallas import tpu as pltpu

F32 = jnp.float32
BF16 = jnp.bfloat16

D_MODEL = 1024
BATCH = 16
SEQ = 256
DEPTH = 2
DEC_BATCH = 2
DEC_SEQ = 4096
PAST_LEN = 256
GRID_W = 64
EPS = 1e-6
N_MOD = 6
A_WIDTH = 256
B_WIDTH = 512
C_WIDTH = 256
SSM_CG = 16
SSM_G = 16
SSM_P = 64
N_DIR = 2
MLA_HEADS = 8
MLA_NOPE = 64
MLA_ROPE = 32
MLA_V = 64
MLA_QK = 96
Q_RANK = 256
KV_RANK = 128
ROPE_BASE = 10000.0
ATTN_SCALE = 1.0 / math.sqrt(MLA_QK)
Q_SCALE = ATTN_SCALE * math.log2(math.e)
GMLP_HEADS = 4
GMLP_CH = 64
CHUNK = 128
OFF_SSM = 0
OFF_Q = OFF_SSM + A_WIDTH
OFF_KV = OFF_Q + Q_RANK
OFF_KR = OFF_KV + KV_RANK
OFF_GM = OFF_KR + MLA_ROPE
D_FF = 2816

N_PROMPT = BATCH * SEQ
N_SAMPLE = DEC_BATCH * DEC_SEQ
N_TOK = N_PROMPT + N_SAMPLE
SSM_STATE = SSM_G * SSM_P

LANES = 128
HEAD_PAD = LANES
ROPE_LANE0 = MLA_NOPE

PROJ_TILE = 512
SSM_STREAMS = 8
SSM_STEPS = 256
SSM_UNIT = SSM_STREAMS * SSM_STEPS
ATTN_TQ = 256
ATTN_KC = 512
FFN_ROWS = SEQ
FFN_FC = 256
FFN_HALO = 16
VMEM_LIMIT = 48 * 1024 * 1024


def _mod_row(tile, tile_tokens):
    start = tile * tile_tokens
    return jnp.where(start < N_PROMPT, 0, 1 + (start - N_PROMPT) // DEC_SEQ)


def _layer_spec(a, l, **kw):
    return pl.BlockSpec((None,) + a.shape[1:], lambda *_: (l,) + (0,) * (a.ndim - 1), **kw)


def _split_specs(t, width, second_offset):
    n_a = N_PROMPT // t
    return (pl.BlockSpec((t, width), lambda i: (jnp.minimum(i, n_a - 1), 0)),
            pl.BlockSpec((t, width), lambda i: (jnp.maximum(i - n_a, 0) + second_offset, 0)))


def _split_args(x, t):
    if isinstance(x, tuple):
        return x, 0
    return (x, x), N_PROMPT // t


def _rms(x):
    return x * lax.rsqrt(jnp.mean(x * x, axis=-1, keepdims=True) + EPS)


def _sigmoid(x):
    return 1.0 / (1.0 + jnp.exp(-x))


def _gelu_tanh(x):
    return 0.5 * x * (1.0 + jnp.tanh(math.sqrt(2.0 / math.pi) * (x + 0.044715 * (x * x * x))))


def _dot(a, b):
    return jnp.dot(a, b, preferred_element_type=F32)


def _mod_kernel(cond_ref, w_ref, b_ref, o_ref):
    cond = cond_ref[...]
    s = (cond * _sigmoid(cond)).astype(BF16)
    o_ref[0] = _dot(s, w_ref[0].astype(BF16)) + b_ref[0]


def _modulation(cond8, w_mod, b_mod):
    tn = 1536
    n_cols = N_MOD * D_MODEL
    return pl.pallas_call(
        _mod_kernel,
        grid=(DEPTH, n_cols // tn),
        in_specs=[
            pl.BlockSpec((8, D_MODEL), lambda l, j: (0, 0)),
            pl.BlockSpec((1, D_MODEL, tn), lambda l, j: (l, 0, j)),
            pl.BlockSpec((1, 1, tn), lambda l, j: (l, 0, j)),
        ],
        out_specs=pl.BlockSpec((1, 8, tn), lambda l, j: (l, 0, j)),
        out_shape=jax.ShapeDtypeStruct((DEPTH, 8, n_cols), F32),
        compiler_params=pltpu.CompilerParams(vmem_limit_bytes=VMEM_LIMIT),
        name="modulation",
    )(cond8, w_mod, b_mod.reshape(DEPTH, 1, n_cols))


def _write_heads(allh, extra, gain, partner, out_ref):
    for h in range(MLA_HEADS):
        tile = slice(h * HEAD_PAD, (h + 1) * HEAD_PAD)
        t = allh[:, tile]
        if extra is not None:
            t = t + extra
        rs = lax.rsqrt(jnp.sum(t * t, axis=-1, keepdims=True) * (1.0 / MLA_QK) + EPS)
        o = t * gain
        if isinstance(partner, tuple):
            o = o + partner[0][:, tile] * partner[1]
        elif partner is not None:
            o = o + partner
        out_ref[h] = (o * rs).astype(BF16)


def _write_values(vall, v_ref):
    lane = lax.broadcasted_iota(jnp.int32, (1, HEAD_PAD), 1)
    ones = jnp.where(lane >= MLA_V, 1.0, 0.0)
    for h in range(MLA_HEADS):
        v_ref[h] = (vall[:, h * HEAD_PAD:(h + 1) * HEAD_PAD] + ones).astype(BF16)


def _proj_kernel(xp_ref, xs_ref, mod_ref, cos_ref, sin_ref, w_in_ref, w_kr_ref, w_krp_ref, qan_ref, kvn_ref,
                 w_uq_ref, w_uqp_ref, w_uk_ref, w_uv_ref, qn_ref, qnp_ref, kn_ref, knp_ref,
                 u_ref, gm_ref, ckv_ref, kr_ref, q_ref, k_ref, v_ref):
    latent = pl.program_id(0) >= N_PROMPT // PROJ_TILE
    x = jnp.where(latent, xs_ref[...], xp_ref[...])
    shift = mod_ref[0, 0:1, :]
    scale = mod_ref[0, 1:2, :]
    h = (_rms(x) * (1.0 + scale) + shift).astype(BF16)
    z = _dot(h, w_in_ref[...])
    kr = _dot(h, w_kr_ref[...])
    u_ref[...] = z[:, 0:A_WIDTH]
    gm_ref[...] = z[:, A_WIDTH + Q_RANK + KV_RANK:]
    ckv = _rms(z[:, A_WIDTH + Q_RANK:A_WIDTH + Q_RANK + KV_RANK]) * kvn_ref[...]
    cq =(_rms(z[:, A_WIDTH:A_WIDTH + Q_RANK]) * qan_ref[...]).astype(BF16)
    ckv_b = ckv.astype(BF16)
    qall = _dot(cq, w_uq_ref[...])
    kall = _dot(ckv_b, w_uk_ref[...])
    _write_values(_dot(ckv_b, w_uv_ref[...]), v_ref)

    @pl.when(jnp.logical_not(latent))
    def _():
        ckv_ref[...] = ckv
        kr_ref[...] = kr
        _write_heads(qall, None, qn_ref[...], None, q_ref)
        _write_heads(kall, kr, kn_ref[...], None, k_ref)

    @pl.when(latent)
    def _():
        cos = cos_ref[...]
        sin = sin_ref[...]
        _write_heads(qall, None, qn_ref[...] * cos, (_dot(cq, w_uqp_ref[...]), qnp_ref[...] * sin), q_ref)
        _write_heads(kall, kr, kn_ref[...] * cos, _dot(h, w_krp_ref[...]) * (knp_ref[...] * sin), k_ref)


def _projection(x, mod, rope_tabs, w, l):
    t = PROJ_TILE
    n_tiles = N_TOK // t
    n_ctx = N_PROMPT // t
    tok = lambda width: pl.BlockSpec((t, width), lambda i: (i, 0))
    ctx = lambda width: pl.BlockSpec((t, width), lambda i: (jnp.minimum(i, n_ctx - 1), 0))
    heads = pl.BlockSpec((MLA_HEADS, t, HEAD_PAD), lambda i: (0, i, 0))
    pos = pl.BlockSpec((t, HEAD_PAD), lambda i: (jnp.maximum(i - n_ctx, 0) % (DEC_SEQ // t), 0))
    weights = [w[n] for n in ('w_in_a', 'w_in_kr', 'w_in_krp', 'q_a_norm', 'kv_a_norm', 'w_uq', 'w_uqp',
                              'w_uk', 'w_uv', 'q_norm', 'q_norm_p', 'k_norm', 'k_norm_p')]
    xs, off = _split_args(x, t)
    head_shape = jax.ShapeDtypeStruct((MLA_HEADS, N_TOK, HEAD_PAD), BF16)
    return pl.pallas_call(
        _proj_kernel,
        grid=(n_tiles,),
        in_specs=[*_split_specs(t, D_MODEL, off),
                  pl.BlockSpec((None, 1, N_MOD, D_MODEL), lambda i: (l, _mod_row(i, t), 0, 0)),
                  pos, pos] + [_layer_spec(a, l) for a in weights],
        out_specs=[tok(A_WIDTH), tok(2 * C_WIDTH), ctx(KV_RANK), ctx(HEAD_PAD), heads, heads, heads],
        out_shape=[
            jax.ShapeDtypeStruct((N_TOK, A_WIDTH), F32),
            jax.ShapeDtypeStruct((N_TOK, 2 * C_WIDTH), F32),
            jax.ShapeDtypeStruct((N_PROMPT, KV_RANK), F32),
            jax.ShapeDtypeStruct((N_PROMPT, HEAD_PAD), F32),
            head_shape, head_shape, head_shape,
        ],
        compiler_params=pltpu.CompilerParams(vmem_limit_bytes=VMEM_LIMIT),
        name="projection",
    )(*xs, mod, *rope_tabs, *weights)


def _ctx_kv_kernel(ckv_ref, kr_ref, w_uk_ref, w_uv_ref, kn_ref, k_ref, v_ref):
    n = DEC_BATCH * PAST_LEN
    ckv_b = ckv_ref[...].reshape(n, KV_RANK).astype(BF16)
    _write_heads(_dot(ckv_b, w_uk_ref[...]), kr_ref[...].reshape(n, HEAD_PAD), kn_ref[...], None, k_ref)
    _write_values(_dot(ckv_b, w_uv_ref[...]), v_ref)


def _ctx_kv(cache_ckv, cache_kr, w, l):
    n = DEC_BATCH * PAST_LEN
    cache = pl.BlockSpec((DEC_BATCH, None, PAST_LEN, HEAD_PAD), lambda i: (0, l, 0, 0))
    weights = [w['w_uk'], w['w_uv'], w['k_norm']]
    heads = pl.BlockSpec((MLA_HEADS, n, HEAD_PAD), lambda i: (0, 0, 0))
    head_shape = jax.ShapeDtypeStruct((MLA_HEADS, n, HEAD_PAD), BF16)
    return pl.pallas_call(
        _ctx_kv_kernel,
        grid=(1,),
        in_specs=[cache, cache] + [_layer_spec(a, l) for a in weights],
        out_specs=[heads, heads],
        out_shape=[head_shape, head_shape],
        compiler_params=pltpu.CompilerParams(vmem_limit_bytes=VMEM_LIMIT),
        name="context_kv",
    )(cache_ckv, cache_kr, *weights)


def _ssm_block(d, k):
    return jnp.where((d == 1) & (k >= 2), k ^ 1, k)


def _ssm_kernel(u_ref, h0r_ref, h0i_ref, ar_ref, ai_ref, pwr_ref, pwi_ref, bm_ref, cr_ref, ci_ref,
                y_ref, hfr_ref, hfi_ref, sre, sim, hin_r, hin_i, car_r, car_i):
    d = pl.program_id(0)
    k = pl.program_id(1)
    rows = 256

    @pl.when(k == 0)
    def _():
        car_r[...] = jnp.zeros_like(car_r)
        car_i[...] = jnp.zeros_like(car_i)

    for c in range(SSM_UNIT // rows):
        sl = slice(c * rows, (c + 1) * rows)
        bu = _dot(u_ref[sl, :].astype(BF16), bm_ref[0])
        sre[sl, :] = bu[:, :SSM_STATE]
        sim[sl, :] = bu[:, SSM_STATE:]

    a_r = jnp.broadcast_to(ar_ref[0], (SSM_STREAMS, SSM_STATE))
    a_i = jnp.broadcast_to(ai_ref[0], (SSM_STREAMS, SSM_STATE))

    def stream_rows(tau):
        return pl.ds(pl.multiple_of(tau * SSM_STREAMS, SSM_STREAMS), SSM_STREAMS)

    def scan_step(i, carry):
        h_r, h_i = carry
        tau = jnp.where(d == 0, i, SSM_STEPS - 1 - i)
        n_r = a_r * h_r - a_i * h_i + sre[stream_rows(tau), :]
        n_i = a_r * h_i + a_i * h_r + sim[stream_rows(tau), :]
        sre[stream_rows(tau), :] = n_r
        sim[stream_rows(tau), :] = n_i
        return n_r, n_i

    zero = jnp.zeros((SSM_STREAMS, SSM_STATE), F32)
    end_r, end_i = lax.fori_loop(0, SSM_STEPS, scan_step, (zero, zero), unroll=2)

    @pl.when(k < 2)
    def _():
        hfr_ref[0] = end_r
        hfi_ref[0] = end_i

    @pl.when(k >= 2)
    def _():
        b = (k - 2) // 2
        first = (k % 2) == 0
        c_r = jnp.where(first, h0r_ref[0, pl.ds(b, 1), :], car_r[...])
        c_i = jnp.where(first, h0i_ref[0, pl.ds(b, 1), :], car_i[...])
        full_r = pwr_ref[0, SSM_STEPS - 1:SSM_STEPS, :]
        full_i = pwi_ref[0, SSM_STEPS - 1:SSM_STEPS, :]

        def chain(order):
            cr, ci = c_r, c_i
            for j in order:
                hin_r[j:j + 1, :] = cr
                hin_i[j:j + 1, :] = ci
                nr = end_r[j:j + 1, :] + full_r * cr - full_i * ci
                ni = end_i[j:j + 1, :] + full_r * ci + full_i * cr
                cr, ci = nr, ni
            car_r[...] = cr
            car_i[...] = ci

        @pl.when(d == 0)
        def _():
            chain(range(SSM_STREAMS))

        @pl.when(d == 1)
        def _():
            chain(range(SSM_STREAMS - 1, -1, -1))

        g_r = hin_r[...]
        g_i = hin_i[...]

        def fix_step(i, carry):
            p = jnp.where(d == 0, i, SSM_STEPS - 1 - i)
            p_r = jnp.broadcast_to(pwr_ref[0, pl.ds(p, 1), :], (SSM_STREAMS, SSM_STATE))
            p_i = jnp.broadcast_to(pwi_ref[0, pl.ds(p, 1), :], (SSM_STREAMS, SSM_STATE))
            sr = stream_rows(i)
            sre[sr, :] = sre[sr, :] + (p_r * g_r - p_i * g_i)
            sim[sr, :] = sim[sr, :] + (p_r * g_i + p_i * g_r)
            return carry

        lax.fori_loop(0, SSM_STEPS, fix_step, 0, unroll=2)

    for c in range(SSM_UNIT // rows):
        sl = slice(c * rows, (c + 1) * rows)
        y_ref[0, sl, :] = (_dot(sre[sl, :].astype(BF16), cr_ref[0])
                           + _dot(sim[sl, :].astype(BF16), ci_ref[0]))


def _step_major(x, inverse=False):
    lead = x.shape[:-2]
    a, b = (SSM_STEPS, SSM_STREAMS) if inverse else (SSM_STREAMS, SSM_STEPS)
    x = x.reshape(lead + (N_TOK // SSM_UNIT, a, b, x.shape[-1]))
    return jnp.swapaxes(x, -3, -2).reshape(lead + (N_TOK, x.shape[-1]))


def _ssm(u, h0r, h0i, tabs, l):
    n_units = N_TOK // SSM_UNIT
    u = _step_major(u)
    per_dir = lambda a: pl.BlockSpec((None, 1) + a.shape[2:], lambda d, k: (l, d) + (0,) * (a.ndim - 2))
    operands = [h0r, h0i] + [tabs[n] for n in ('a_r', 'a_i', 'pw_r', 'pw_i', 'b_blk', 'c_r', 'c_i')]
    y, hf_r, hf_i = pl.pallas_call(
        _ssm_kernel,
        grid=(N_DIR, n_units),
        in_specs=[pl.BlockSpec((SSM_UNIT, A_WIDTH), lambda d, k: (_ssm_block(d, k), 0))]
                 + [per_dir(a) for a in operands],
        out_specs=[pl.BlockSpec((1, SSM_UNIT, A_WIDTH), lambda d, k: (d, _ssm_block(d, k), 0)),
                   pl.BlockSpec((1, SSM_STREAMS, SSM_STATE), lambda d, k: (d, jnp.minimum(k, 1), 0)),
                   pl.BlockSpec((1, SSM_STREAMS, SSM_STATE), lambda d, k: (d, jnp.minimum(k, 1), 0))],
        out_shape=[jax.ShapeDtypeStruct((N_DIR, N_TOK, A_WIDTH), F32),
                   jax.ShapeDtypeStruct((N_DIR, BATCH, SSM_STATE), F32),
                   jax.ShapeDtypeStruct((N_DIR, BATCH, SSM_STATE), F32)],
        scratch_shapes=[pltpu.VMEM((SSM_UNIT, SSM_STATE), F32), pltpu.VMEM((SSM_UNIT, SSM_STATE), F32),
                        pltpu.VMEM((SSM_STREAMS, SSM_STATE), F32), pltpu.VMEM((SSM_STREAMS, SSM_STATE), F32),
                        pltpu.VMEM((1, SSM_STATE), F32), pltpu.VMEM((1, SSM_STATE), F32)],
        compiler_params=pltpu.CompilerParams(vmem_limit_bytes=VMEM_LIMIT,
                                             dimension_semantics=("arbitrary", "arbitrary")),
        name="ssm_scan",
    )(u, *operands)
    return _step_major(y, inverse=True), hf_r, hf_i


def _attn_kernel(*refs, key_lens):
    n_src = len(key_lens)
    q_ref = refs[0]
    k_refs = refs[1:1 + n_src]
    v_refs = refs[1 + n_src:1 + 2 * n_src]
    o_ref = refs[1 + 2 * n_src]
    s_ref = refs[2 + 2 * n_src]
    chunks = []
    off = 0
    for si, n in enumerate(key_lens):
        for c0 in range(0, n, ATTN_KC):
            cl = min(ATTN_KC, n - c0)
            chunks.append((si, c0, cl, off))
            off += cl
    outs = []
    for hh in range(2):
        q = q_ref[hh]
        mp = jnp.full((ATTN_TQ, LANES), -jnp.inf, F32)
        for si, c0, cl, o in chunks:
            s = lax.dot_general(q, k_refs[si][hh, c0:c0 + cl, :], (((1,), (1,)), ((), ())),
                                preferred_element_type=F32)
            s_ref[:, o:o + cl] = s
            for g in range(cl // LANES):
                mp = jnp.maximum(mp, s[:, g * LANES:(g + 1) * LANES])
        m = jnp.max(mp, axis=-1, keepdims=True)
        acc = jnp.zeros((ATTN_TQ, LANES), F32)
        for si, c0, cl, o in chunks:
            p = jnp.exp2(s_ref[:, o:o + cl] - m)
            acc = acc + _dot(p.astype(BF16), v_refs[si][hh, c0:c0 + cl, :])
        outs.append(acc / pltpu.roll(acc, MLA_V, 1))
    lane = lax.broadcasted_iota(jnp.int32, (ATTN_TQ, LANES), 1)
    o_ref[...] = jnp.where(lane < MLA_V, outs[0], pltpu.roll(outs[1], MLA_V, 1))


def _attention(q, ks, vs, n_seq, seq_len, tok0, key_lens, key_tok0):
    n_qt = seq_len // ATTN_TQ
    q0 = tok0 // ATTN_TQ
    in_specs = [pl.BlockSpec((2, ATTN_TQ, HEAD_PAD), lambda b, hp, qi: (hp, q0 + b * n_qt + qi, 0))]
    for n, t0 in zip(key_lens, key_tok0):
        in_specs.append(pl.BlockSpec((2, n, HEAD_PAD), lambda b, hp, qi, n=n, t0=t0: (hp, t0 // n + b, 0)))
    for n, t0 in zip(key_lens, key_tok0):
        in_specs.append(pl.BlockSpec((2, n, HEAD_PAD), lambda b, hp, qi, n=n, t0=t0: (hp, t0 // n + b, 0)))
    return pl.pallas_call(
        functools.partial(_attn_kernel, key_lens=tuple(key_lens)),
        grid=(n_seq, MLA_HEADS // 2, n_qt),
        in_specs=in_specs,
        out_specs=pl.BlockSpec((ATTN_TQ, LANES), lambda b, hp, qi: (b * n_qt + qi, hp)),
        out_shape=jax.ShapeDtypeStruct((n_seq * seq_len, B_WIDTH), F32),
        scratch_shapes=[pltpu.VMEM((ATTN_TQ, sum(key_lens)), F32)],
        compiler_params=pltpu.CompilerParams(vmem_limit_bytes=VMEM_LIMIT,
                                             dimension_semantics=("arbitrary", "arbitrary", "arbitrary")),
        name="attention",
    )(q, *ks, *vs)


def _merge_kernel(xp_ref, xs_ref, mod_ref, yssm_ref, u_ref, ybp_ref, ybs_ref, gm_ref, dskip_ref, w_glu_ref,
                  vn_ref, w_s_ref, bias_ref, g_ref, w_out_ref, x1_ref, h2_ref):
    t = xp_ref.shape[0]
    latent = pl.program_id(0) >= N_PROMPT // t
    u = u_ref[...]
    y = yssm_ref[0] + yssm_ref[1] + dskip_ref[...] * u
    y = _gelu_tanh(y)
    y_a = y * _sigmoid(_dot(y.astype(BF16), w_glu_ref[...]))
    gm = gm_ref[...]
    vn = (_rms(gm[:, C_WIDTH:]) * vn_ref[...]).astype(BF16)
    lane = lax.broadcasted_iota(jnp.int32, (CHUNK, C_WIDTH), 1)
    mixed = []
    for c in range(t // CHUNK):
        vc = vn[c * CHUNK:(c + 1) * CHUNK, :]
        m = _dot(w_s_ref[0], vc)
        for h in range(1, GMLP_HEADS):
            m = jnp.where(lane >= h * GMLP_CH, _dot(w_s_ref[h], vc), m)
        mixed.append(m + bias_ref[...])
    y_c = gm[:, :C_WIDTH] * jnp.concatenate(mixed, axis=0)
    g = g_ref[...]
    n_a = (_rms(y_a) * g[:, :A_WIDTH]).astype(BF16)
    y_b = jnp.where(latent, ybs_ref[...], ybp_ref[...])
    n_b = (_rms(y_b) * g[:, A_WIDTH:A_WIDTH + B_WIDTH]).astype(BF16)
    n_c = (_rms(y_c) * g[:, A_WIDTH + B_WIDTH:]).astype(BF16)
    o = (_dot(n_a, w_out_ref[0:A_WIDTH, :]) + _dot(n_b, w_out_ref[A_WIDTH:A_WIDTH + B_WIDTH, :])
         + _dot(n_c, w_out_ref[A_WIDTH + B_WIDTH:, :]))
    x1 = jnp.where(latent, xs_ref[...], xp_ref[...]) + mod_ref[0, 2:3, :] * o
    x1_ref[...] = x1
    h2_ref[...] = (_rms(x1) * (1.0 + mod_ref[0, 4:5, :]) + mod_ref[0, 3:4, :]).astype(BF16)


def _merge(x, mod, yssm, u, yb_ctx, yb_lat, gm, w, l):
    t = PROJ_TILE
    tok = lambda width: pl.BlockSpec((t, width), lambda i: (i, 0))
    weights = [w[n] for n in ('ssm_d', 'w_glu', 'gmlp_v_norm', 'gmlp_w_s', 'gmlp_bias', 'w_out_norm', 'w_out')]
    xs, off = _split_args(x, t)
    return pl.pallas_call(
        _merge_kernel,
        grid=(N_TOK // t,),
        in_specs=[*_split_specs(t, D_MODEL, off),
                  pl.BlockSpec((None, 1, N_MOD, D_MODEL), lambda i: (l, _mod_row(i, t), 0, 0)),
                  pl.BlockSpec((N_DIR, t, A_WIDTH), lambda i: (0, i, 0)),
                  tok(A_WIDTH), *_split_specs(t, B_WIDTH, 0), tok(2 * C_WIDTH)]
                 + [_layer_spec(a, l) for a in weights],
        out_specs=[tok(D_MODEL), tok(D_MODEL)],
        out_shape=[jax.ShapeDtypeStruct((N_TOK, D_MODEL), F32),
                   jax.ShapeDtypeStruct((N_TOK, D_MODEL), BF16)],
        compiler_params=pltpu.CompilerParams(vmem_limit_bytes=VMEM_LIMIT),
        name="merge",
    )(*xs, mod, yssm, u, yb_ctx, yb_lat, gm, *weights)


def _ffn_kernel(x1_ref, mod_ref, hp_ref, hm_ref, hn_ref, wup_ref, cw_ref, cb_ref, wd_ref, *rest):
    out_refs, a_scr = rest[:-1], rest[-1]
    seg =pl.program_id(0) - N_PROMPT // FFN_ROWS
    per_seq = DEC_SEQ // FFN_ROWS
    starts = (seg < 0) | ((seg & (per_seq - 1)) == 0)
    ends = (seg < 0) | ((seg & (per_seq - 1)) == per_seq - 1)
    h_prev = hp_ref[...]
    h_next = hn_ref[...]
    h_prev = jnp.where(starts, jnp.zeros_like(h_prev), h_prev)
    h_next = jnp.where(ends, jnp.zeros_like(h_next), h_next)
    he = jnp.concatenate([h_prev, hm_ref[...], h_next], axis=0)
    r0 = FFN_HALO

    def conv(u, c0):
        cols = slice(c0, c0 + FFN_FC)
        return (u[r0 - 1:r0 - 1 + FFN_ROWS] * cw_ref[0:1, cols] + u[r0:r0 + FFN_ROWS] * cw_ref[1:2, cols]
                + u[r0 + 1:r0 + 1 + FFN_ROWS] * cw_ref[2:3, cols] + cb_ref[:, cols])

    for j in range(D_FF // FFN_FC):
        cg = j * FFN_FC
        cv = D_FF + j * FFN_FC
        gate = conv(_dot(he, wup_ref[:, cg:cg + FFN_FC]), cg)
        val = conv(_dot(he, wup_ref[:, cv:cv + FFN_FC]), cv)
        a_scr[:, cg:cg + FFN_FC] = (gate * _sigmoid(gate) * val).astype(BF16)

    out = x1_ref[...] + mod_ref[0, 5:6, :] * _dot(a_scr[...], wd_ref[...])
    if len(out_refs) == 1:
        out_refs[0][...] = out
    else:
        @pl.when(seg < 0)
        def _():
            out_refs[0][...] = out

        @pl.when(seg >= 0)
        def _():
            out_refs[1][...] = out


def _ffn(x1, mod, h2, w, l, split_out):
    t = FFN_ROWS
    hb = t // FFN_HALO
    last_halo = N_TOK // FFN_HALO - 1
    weights = [w[n] for n in ('ffn_w_up', 'ffn_conv_w', 'ffn_conv_b', 'ffn_w_down')]
    if split_out:
        out_specs = list(_split_specs(t, D_MODEL, 0))
        out_shape = [jax.ShapeDtypeStruct((N_PROMPT, D_MODEL), F32), jax.ShapeDtypeStruct((N_SAMPLE, D_MODEL), F32)]
    else:
        out_specs = pl.BlockSpec((t, D_MODEL), lambda i: (i, 0))
        out_shape = jax.ShapeDtypeStruct((N_TOK, D_MODEL), F32)
    return pl.pallas_call(
        _ffn_kernel,
        grid=(N_TOK // t,),
        in_specs=[pl.BlockSpec((t, D_MODEL), lambda i: (i, 0)),
                  pl.BlockSpec((None, 1, N_MOD, D_MODEL), lambda i: (l, _mod_row(i, t), 0, 0)),
                  pl.BlockSpec((FFN_HALO, D_MODEL), lambda i: (jnp.maximum(i * hb - 1, 0), 0)),
                  pl.BlockSpec((t, D_MODEL), lambda i: (i, 0)),
                  pl.BlockSpec((FFN_HALO, D_MODEL), lambda i: (jnp.minimum((i + 1) * hb, last_halo), 0))]
                 + [_layer_spec(a, l, pipeline_mode=pl.Buffered(1)) for a in weights],
        out_specs=out_specs,
        out_shape=out_shape,
        scratch_shapes=[pltpu.VMEM((t, D_FF), BF16)],
        compiler_params=pltpu.CompilerParams(vmem_limit_bytes=VMEM_LIMIT,
                                             dimension_semantics=("arbitrary",)),
        name="conv_ffn",
    )(x1, mod, h2, h2, h2, *weights)


def _pad_heads(w, head_w, used, offset=0):
    lead = w.shape[:-1]
    w = w.reshape(lead + (MLA_HEADS, head_w))[..., :used]
    pad = [(0, 0)] * (len(lead) + 1) + [(offset, HEAD_PAD - used - offset)]
    return jnp.pad(w, pad).reshape(lead + (MLA_HEADS * HEAD_PAD,))


def _head_gain(g):
    return jnp.pad(g, ((0, 0), (0, HEAD_PAD - MLA_QK)))[:, None, :]


def _ssm_tables(a_re, a_im, b_re, b_im, c_re, c_im, log_dt):
    a = lax.complex(a_re, a_im)
    dt = jnp.exp(log_dt)[..., None]
    a_bar = jnp.exp(a * dt)
    b_bar = ((a_bar - 1.0) / a)[..., None] * lax.complex(b_re, b_im)
    steps = jnp.arange(1, SSM_STEPS + 1, dtype=F32)[None, None, :, None, None]
    pw = jnp.exp((a * dt)[:, :, None] * steps)
    eye = jnp.eye(SSM_G, dtype=F32)
    blk_b = lambda m: jnp.einsum('ldgpc,gh->ldgchp', m, eye).reshape(DEPTH, N_DIR, A_WIDTH, SSM_STATE)
    blk_c = lambda m: jnp.einsum('ldgcp,gh->ldgphc', m, eye).reshape(DEPTH, N_DIR, SSM_STATE, A_WIDTH)
    flat = lambda m: m.reshape(DEPTH, N_DIR, 1, SSM_STATE)
    return {
        'a_r': flat(a_bar.real), 'a_i': flat(a_bar.imag),
        'pw_r': pw.real.reshape(DEPTH, N_DIR, SSM_STEPS, SSM_STATE),
        'pw_i': pw.imag.reshape(DEPTH, N_DIR, SSM_STEPS, SSM_STATE),
        'b_blk': jnp.concatenate([blk_b(b_bar.real), blk_b(b_bar.imag)], axis=-1).astype(BF16),
        'c_r': blk_c(c_re).astype(BF16),
        'c_i': blk_c(-c_im).astype(BF16),
    }


def _rope_tables():
    rows = DEC_SEQ // GRID_W
    row = np.repeat(np.arange(rows, dtype=np.float32), GRID_W)
    col = np.tile(np.arange(GRID_W, dtype=np.float32), rows)
    n_freq = MLA_ROPE // 4
    inv = (np.float32(ROPE_BASE) ** (-np.arange(n_freq, dtype=np.float32) / np.float32(n_freq))).astype(np.float32)
    ang = np.concatenate([row[:, None] * inv, col[:, None] * inv], axis=-1).astype(np.float32)
    cos, sin = np.cos(ang), np.sin(ang)
    cos_t = np.ones((DEC_SEQ, HEAD_PAD), np.float32)
    sin_t = np.zeros((DEC_SEQ, HEAD_PAD), np.float32)
    cos_t[:, ROPE_LANE0:ROPE_LANE0 + MLA_ROPE] = np.concatenate([cos, cos], axis=-1)
    sin_t[:, ROPE_LANE0:ROPE_LANE0 + MLA_ROPE] = np.concatenate([sin, sin], axis=-1)
    return jnp.asarray(cos_t), jnp.asarray(sin_t)


def _rope_partner(r):
    half = MLA_ROPE // 2
    return jnp.concatenate([-r[..., half:], r[..., :half]], axis=-1)


def _rope_lanes(r):
    return jnp.pad(r, [(0, 0)] * (r.ndim - 1) + [(ROPE_LANE0, HEAD_PAD - ROPE_LANE0 - MLA_ROPE)])


def _head_gain_partner(g):
    half = MLA_ROPE // 2
    r = g[:, MLA_NOPE:]
    return _rope_lanes(jnp.concatenate([r[:, half:], r[:, :half]], axis=-1))[:, None, :]


def _prepare_weights(w_in, ssm_d, ssm_w_glu, q_a_norm, kv_a_norm, w_uq, w_ukv, q_norm, k_norm,
                     gmlp_v_norm, gmlp_w_s, gmlp_b_s, w_out_norm, w_out, ffn_w_up, ffn_conv_w,
                     ffn_conv_b, ffn_w_down):
    w_kr = w_in[:, :, OFF_KR:OFF_GM]
    ukv = w_ukv.reshape(DEPTH, KV_RANK, MLA_HEADS, MLA_NOPE + MLA_V)
    uq = w_uq.reshape(DEPTH, Q_RANK, MLA_HEADS, MLA_QK)
    q_gain = q_norm * Q_SCALE
    row = lambda a: a[:, None, :]
    return {
        'w_in_a': jnp.concatenate([w_in[:, :, OFF_SSM:OFF_KR], w_in[:, :, OFF_GM:]], axis=2).astype(BF16),
        'w_in_kr': _rope_lanes(w_kr).astype(BF16),
        'w_in_krp': _rope_lanes(_rope_partner(w_kr)).astype(BF16),
        'w_uqp': _rope_lanes(_rope_partner(uq[..., MLA_NOPE:])).reshape(DEPTH, Q_RANK, -1).astype(BF16),
        'q_a_norm': row(q_a_norm), 'kv_a_norm': row(kv_a_norm),
        'w_uq': _pad_heads(w_uq, MLA_QK, MLA_QK).astype(BF16),
        'w_uk': _pad_heads(ukv[..., :MLA_NOPE].reshape(DEPTH, KV_RANK, -1), MLA_NOPE, MLA_NOPE).astype(BF16),
        'w_uv': _pad_heads(ukv[..., MLA_NOPE:].reshape(DEPTH, KV_RANK, -1), MLA_V, MLA_V).astype(BF16),
        'q_norm': _head_gain(q_gain), 'q_norm_p': _head_gain_partner(q_gain),
        'k_norm': _head_gain(k_norm), 'k_norm_p': _head_gain_partner(k_norm),
        'ssm_d': row(ssm_d), 'w_glu': ssm_w_glu.astype(BF16),
        'gmlp_v_norm': row(gmlp_v_norm), 'gmlp_w_s': gmlp_w_s.astype(BF16),
        'gmlp_bias': jnp.repeat(jnp.swapaxes(gmlp_b_s, 1, 2), GMLP_CH, axis=2),
        'w_out_norm': row(w_out_norm), 'w_out': w_out.astype(BF16),
        'ffn_w_up': ffn_w_up.astype(BF16), 'ffn_conv_w': ffn_conv_w,
        'ffn_conv_b': row(ffn_conv_b), 'ffn_w_down': ffn_w_down.astype(BF16),
    }


def kernel(x_prompt, x_sample, cache_ckv, cache_krope, state_ssm_re, state_ssm_im, c, c_ctx, w_mod, b_mod, w_in, ssm_a_re, ssm_a_im, ssm_b_re, ssm_b_im, ssm_c_re, ssm_c_im, ssm_log_dt, ssm_d, ssm_w_glu, q_a_norm, kv_a_norm, w_uq, w_ukv, q_norm, k_norm, gmlp_v_norm, gmlp_w_s, gmlp_b_s, w_out_norm, w_out, ffn_w_up, ffn_conv_w, ffn_conv_b, ffn_w_down):
    cond8 = jnp.concatenate([c_ctx[None, :], c, jnp.zeros((8 - 1 - DEC_BATCH, D_MODEL), F32)], axis=0)
    mod = _modulation(cond8, w_mod, b_mod).reshape(DEPTH, 8, N_MOD, D_MODEL)
    rope_tabs = _rope_tables()
    w = _prepare_weights(w_in, ssm_d, ssm_w_glu, q_a_norm, kv_a_norm, w_uq, w_ukv, q_norm, k_norm,
                         gmlp_v_norm, gmlp_w_s, gmlp_b_s, w_out_norm, w_out, ffn_w_up, ffn_conv_w,
                         ffn_conv_b, ffn_w_down)
    tabs = _ssm_tables(ssm_a_re, ssm_a_im, ssm_b_re, ssm_b_im, ssm_c_re, ssm_c_im, ssm_log_dt)
    cache_kr = _rope_lanes(cache_krope)

    def h0(st):
        st = jnp.transpose(st.reshape(DEC_BATCH, DEPTH, N_DIR, SSM_STATE), (1, 2, 0, 3))
        return jnp.pad(st, ((0, 0), (0, 0), (0, SSM_STREAMS - DEC_BATCH), (0, 0)))

    h0r, h0i = h0(state_ssm_re), h0(state_ssm_im)

    x = (x_prompt.reshape(N_PROMPT, D_MODEL), x_sample.reshape(N_SAMPLE, D_MODEL))
    ckv_out, kr_out, hre_out, him_out = [], [], [], []
    for l in range(DEPTH):
        u, gm, ckv, kr, q, k, v = _projection(x, mod, rope_tabs, w, l)
        k_c, v_c = _ctx_kv(cache_ckv, cache_kr, w, l)
        yb_ctx = _attention(q, [k], [v], BATCH, SEQ, 0, [SEQ], [0])
        yssm, hf_r, hf_i = _ssm(u, h0r, h0i, tabs, l)
        yb_lat = _attention(q, [k_c, k], [v_c, v], DEC_BATCH, DEC_SEQ, N_PROMPT,
                            [PAST_LEN, DEC_SEQ], [0, N_PROMPT])
        x1, h2 = _merge(x, mod, yssm, u, yb_ctx, yb_lat, gm, w, l)
        x = _ffn(x1, mod, h2, w, l, split_out=(l == DEPTH - 1))

        ckv_out.append(ckv.reshape(BATCH, SEQ, KV_RANK))
        kr_out.append(kr[:, ROPE_LANE0:ROPE_LANE0 + MLA_ROPE].reshape(BATCH, SEQ, MLA_ROPE))
        to_state = lambda st: jnp.transpose(st, (1, 0, 2)).reshape(BATCH, N_DIR, SSM_G, SSM_P)
        hre_out.append(to_state(hf_r))
        him_out.append(to_state(hf_i))

    y_prompt = x[0].reshape(BATCH, SEQ, D_MODEL)
    y_sample = x[1].reshape(DEC_BATCH, DEC_SEQ, D_MODEL)
    return (y_prompt, y_sample, jnp.stack(ckv_out, axis=1), jnp.stack(kr_out, axis=1),
            jnp.stack(hre_out, axis=1), jnp.stack(him_out, axis=1))
```

```python
import functools
import math

import jax
import jax.numpy as jnp
import numpy as np
from jax import lax
from jax.experimental import pallas as pl
from jax.experimental.pallas import tpu as pltpu

F32 = jnp.float32
BF16 = jnp.bfloat16

D_MODEL = 1024
BATCH = 16
SEQ = 256
DEPTH = 2
DEC_BATCH = 2
DEC_SEQ = 4096
PAST_LEN = 256
GRID_W = 64
EPS = 1e-6
N_MOD = 6
A_WIDTH = 256
B_WIDTH = 512
C_WIDTH = 256
SSM_CG = 16
SSM_G = 16
SSM_P = 64
N_DIR = 2
MLA_HEADS = 8
MLA_NOPE = 64
MLA_ROPE = 32
MLA_V = 64
MLA_QK = 96
Q_RANK = 256
KV_RANK = 128
ROPE_BASE = 10000.0
ATTN_SCALE = 1.0 / math.sqrt(MLA_QK)
Q_SCALE = ATTN_SCALE * math.log2(math.e)
GMLP_HEADS = 4
GMLP_CH = 64
CHUNK = 128
OFF_SSM = 0
OFF_Q = OFF_SSM + A_WIDTH
OFF_KV = OFF_Q + Q_RANK
OFF_KR = OFF_KV + KV_RANK
OFF_GM = OFF_KR + MLA_ROPE
D_FF = 2816

N_PROMPT = BATCH * SEQ
N_SAMPLE = DEC_BATCH * DEC_SEQ
N_TOK = N_PROMPT + N_SAMPLE
SSM_STATE = SSM_G * SSM_P

LANES = 128
HEAD_PAD = LANES
ROPE_LANE0 = MLA_NOPE

PROJ_TILE = 512
SSM_STREAMS = 8
SSM_STEPS = 256
SSM_UNIT = SSM_STREAMS * SSM_STEPS
ATTN_TQ = 256
ATTN_KC = 1024
ATTN_HEADS = 4
FFN_ROWS = SEQ
FFN_FC = 256
FFN_HALO = 16
VMEM_LIMIT = 48 * 1024 * 1024


def _mod_row(tile, tile_tokens):
    start = tile * tile_tokens
    return jnp.where(start < N_PROMPT, 0, 1 + (start - N_PROMPT) // DEC_SEQ)


def _layer_spec(a, l, **kw):
    return pl.BlockSpec((None,) + a.shape[1:], lambda *_: (l,) + (0,) * (a.ndim - 1), **kw)


def _split_specs(t, width, second_offset):
    n_a = N_PROMPT // t
    return (pl.BlockSpec((t, width), lambda i: (jnp.minimum(i, n_a - 1), 0)),
            pl.BlockSpec((t, width), lambda i: (jnp.maximum(i - n_a, 0) + second_offset, 0)))


def _split_args(x, t):
    if isinstance(x, tuple):
        return x, 0
    return (x, x), N_PROMPT // t


def _rms(x):
    return x * lax.rsqrt(jnp.mean(x * x, axis=-1, keepdims=True) + EPS)


def _sigmoid(x):
    return 1.0 / (1.0 + jnp.exp(-x))


def _gelu_tanh(x):
    return 0.5 * x * (1.0 + jnp.tanh(math.sqrt(2.0 / math.pi) * (x + 0.044715 * (x * x * x))))


def _dot(a, b):
    return jnp.dot(a, b, preferred_element_type=F32)


def _mod_kernel(cond_ref, w_ref, b_ref, o_ref):
    cond = cond_ref[...]
    s = (cond * _sigmoid(cond)).astype(BF16)
    o_ref[0] = _dot(s, w_ref[0].astype(BF16)) + b_ref[0]


def _modulation(cond8, w_mod, b_mod):
    tn = 1536
    n_cols = N_MOD * D_MODEL
    return pl.pallas_call(
        _mod_kernel,
        grid=(DEPTH, n_cols // tn),
        in_specs=[
            pl.BlockSpec((8, D_MODEL), lambda l, j: (0, 0)),
            pl.BlockSpec((1, D_MODEL, tn), lambda l, j: (l, 0, j)),
            pl.BlockSpec((1, 1, tn), lambda l, j: (l, 0, j)),
        ],
        out_specs=pl.BlockSpec((1, 8, tn), lambda l, j: (l, 0, j)),
        out_shape=jax.ShapeDtypeStruct((DEPTH, 8, n_cols), F32),
        compiler_params=pltpu.CompilerParams(vmem_limit_bytes=VMEM_LIMIT),
        name="modulation",
    )(cond8, w_mod, b_mod.reshape(DEPTH, 1, n_cols))


def _write_heads(allh, extra, gain, partner, out_ref):
    for h in range(MLA_HEADS):
        tile = slice(h * HEAD_PAD, (h + 1) * HEAD_PAD)
        t = allh[:, tile]
        if extra is not None:
            t = t + extra
        rs = lax.rsqrt(jnp.sum(t * t, axis=-1, keepdims=True) * (1.0 / MLA_QK) + EPS)
        o = t * gain
        if isinstance(partner, tuple):
            o = o + partner[0][:, tile] * partner[1]
        elif partner is not None:
            o = o + partner
        out_ref[h] = (o * rs).astype(BF16)


def _write_values(vall, v_ref):
    lane = lax.broadcasted_iota(jnp.int32, (1, HEAD_PAD), 1)
    ones = jnp.where(lane >= MLA_V, 1.0, 0.0)
    for h in range(MLA_HEADS):
        v_ref[h] = (vall[:, h * HEAD_PAD:(h + 1) * HEAD_PAD] + ones).astype(BF16)


def _proj_kernel(xp_ref, xs_ref, mod_ref, cos_ref, sin_ref, w_in_ref, w_kr_ref, w_krp_ref, qan_ref, kvn_ref,
                 w_uq_ref, w_uqp_ref, w_uk_ref, w_uv_ref, qn_ref, qnp_ref, kn_ref, knp_ref,
                 u_ref, gm_ref, ckv_ref, kr_ref, q_ref, k_ref, v_ref):
    latent = pl.program_id(0) >= N_PROMPT // PROJ_TILE
    x = jnp.where(latent, xs_ref[...], xp_ref[...])
    shift = mod_ref[0, 0:1, :]
    scale = mod_ref[0, 1:2, :]
    h = (_rms(x) * (1.0 + scale) + shift).astype(BF16)
    z = _dot(h, w_in_ref[...])
    kr = _dot(h, w_kr_ref[...])
    u_ref[...] = z[:, 0:A_WIDTH]
    gm_ref[...] = z[:, A_WIDTH + Q_RANK + KV_RANK:]
    ckv = _rms(z[:, A_WIDTH + Q_RANK:A_WIDTH + Q_RANK + KV_RANK]) * kvn_ref[...]
    cq =(_rms(z[:, A_WIDTH:A_WIDTH + Q_RANK]) * qan_ref[...]).astype(BF16)
    ckv_b = ckv.astype(BF16)
    qall = _dot(cq, w_uq_ref[...])
    kall = _dot(ckv_b, w_uk_ref[...])
    _write_values(_dot(ckv_b, w_uv_ref[...]), v_ref)

    @pl.when(jnp.logical_not(latent))
    def _():
        ckv_ref[...] = ckv
        kr_ref[...] = kr
        _write_heads(qall, None, qn_ref[...], None, q_ref)
        _write_heads(kall, kr, kn_ref[...], None, k_ref)

    @pl.when(latent)
    def _():
        cos = cos_ref[...]
        sin = sin_ref[...]
        _write_heads(qall, None, qn_ref[...] * cos, (_dot(cq, w_uqp_ref[...]), qnp_ref[...] * sin), q_ref)
        _write_heads(kall, kr, kn_ref[...] * cos, _dot(h, w_krp_ref[...]) * (knp_ref[...] * sin), k_ref)


def _projection(x, mod, rope_tabs, w, l):
    t = PROJ_TILE
    n_tiles = N_TOK // t
    n_ctx = N_PROMPT // t
    tok = lambda width: pl.BlockSpec((t, width), lambda i: (i, 0))
    ctx = lambda width: pl.BlockSpec((t, width), lambda i: (jnp.minimum(i, n_ctx - 1), 0))
    heads = pl.BlockSpec((MLA_HEADS, t, HEAD_PAD), lambda i: (0, i, 0))
    pos = pl.BlockSpec((t, HEAD_PAD), lambda i: (jnp.maximum(i - n_ctx, 0) % (DEC_SEQ // t), 0))
    weights = [w[n] for n in ('w_in_a', 'w_in_kr', 'w_in_krp', 'q_a_norm', 'kv_a_norm', 'w_uq', 'w_uqp',
                              'w_uk', 'w_uv', 'q_norm', 'q_norm_p', 'k_norm', 'k_norm_p')]
    xs, off = _split_args(x, t)
    head_shape = jax.ShapeDtypeStruct((MLA_HEADS, N_TOK, HEAD_PAD), BF16)
    return pl.pallas_call(
        _proj_kernel,
        grid=(n_tiles,),
        in_specs=[*_split_specs(t, D_MODEL, off),
                  pl.BlockSpec((None, 1, N_MOD, D_MODEL), lambda i: (l, _mod_row(i, t), 0, 0)),
                  pos, pos] + [_layer_spec(a, l) for a in weights],
        out_specs=[tok(A_WIDTH), tok(2 * C_WIDTH), ctx(KV_RANK), ctx(HEAD_PAD), heads, heads, heads],
        out_shape=[
            jax.ShapeDtypeStruct((N_TOK, A_WIDTH), F32),
            jax.ShapeDtypeStruct((N_TOK, 2 * C_WIDTH), F32),
            jax.ShapeDtypeStruct((N_PROMPT, KV_RANK), F32),
            jax.ShapeDtypeStruct((N_PROMPT, HEAD_PAD), F32),
            head_shape, head_shape, head_shape,
        ],
        compiler_params=pltpu.CompilerParams(vmem_limit_bytes=VMEM_LIMIT),
        name="projection",
    )(*xs, mod, *rope_tabs, *weights)


def _ctx_kv_kernel(ckv_ref, kr_ref, w_uk_ref, w_uv_ref, kn_ref, k_ref, v_ref):
    n = DEC_BATCH * PAST_LEN
    ckv_b = ckv_ref[...].reshape(n, KV_RANK).astype(BF16)
    _write_heads(_dot(ckv_b, w_uk_ref[...]), kr_ref[...].reshape(n, HEAD_PAD), kn_ref[...], None, k_ref)
    _write_values(_dot(ckv_b, w_uv_ref[...]), v_ref)


def _ctx_kv(cache_ckv, cache_kr, w, l):
    n = DEC_BATCH * PAST_LEN
    cache = pl.BlockSpec((DEC_BATCH, None, PAST_LEN, HEAD_PAD), lambda i: (0, l, 0, 0))
    weights = [w['w_uk'], w['w_uv'], w['k_norm']]
    heads = pl.BlockSpec((MLA_HEADS, n, HEAD_PAD), lambda i: (0, 0, 0))
    head_shape = jax.ShapeDtypeStruct((MLA_HEADS, n, HEAD_PAD), BF16)
    return pl.pallas_call(
        _ctx_kv_kernel,
        grid=(1,),
        in_specs=[cache, cache] + [_layer_spec(a, l) for a in weights],
        out_specs=[heads, heads],
        out_shape=[head_shape, head_shape],
        compiler_params=pltpu.CompilerParams(vmem_limit_bytes=VMEM_LIMIT),
        name="context_kv",
    )(cache_ckv, cache_kr, *weights)


def _ssm_block(d, k):
    return jnp.where((d == 1) & (k >= 2), k ^ 1, k)


def _ssm_kernel(u_ref, h0r_ref, h0i_ref, ar_ref, ai_ref, pwr_ref, pwi_ref, bm_ref, cr_ref, ci_ref,
                y_ref, hfr_ref, hfi_ref, sre, sim, hin_r, hin_i, car_r, car_i):
    d = pl.program_id(0)
    k = pl.program_id(1)
    rows = 256

    @pl.when(k == 0)
    def _():
        car_r[...] = jnp.zeros_like(car_r)
        car_i[...] = jnp.zeros_like(car_i)

    for c in range(SSM_UNIT // rows):
        sl = slice(c * rows, (c + 1) * rows)
        bu = _dot(u_ref[sl, :].astype(BF16), bm_ref[0])
        sre[sl, :] = bu[:, :SSM_STATE]
        sim[sl, :] = bu[:, SSM_STATE:]

    a_r = jnp.broadcast_to(ar_ref[0], (SSM_STREAMS, SSM_STATE))
    a_i = jnp.broadcast_to(ai_ref[0], (SSM_STREAMS, SSM_STATE))

    def stream_rows(tau):
        return pl.ds(pl.multiple_of(tau * SSM_STREAMS, SSM_STREAMS), SSM_STREAMS)

    def scan_step(i, carry):
        h_r, h_i = carry
        tau = jnp.where(d == 0, i, SSM_STEPS - 1 - i)
        n_r = a_r * h_r - a_i * h_i + sre[stream_rows(tau), :]
        n_i = a_r * h_i + a_i * h_r + sim[stream_rows(tau), :]
        sre[stream_rows(tau), :] = n_r
        sim[stream_rows(tau), :] = n_i
        return n_r, n_i

    zero = jnp.zeros((SSM_STREAMS, SSM_STATE), F32)
    end_r, end_i = lax.fori_loop(0, SSM_STEPS, scan_step, (zero, zero), unroll=2)

    @pl.when(k < 2)
    def _():
        hfr_ref[0] = end_r
        hfi_ref[0] = end_i

    @pl.when(k >= 2)
    def _():
        b = (k - 2) // 2
        first = (k % 2) == 0
        c_r = jnp.where(first, h0r_ref[0, pl.ds(b, 1), :], car_r[...])
        c_i = jnp.where(first, h0i_ref[0, pl.ds(b, 1), :], car_i[...])
        full_r = pwr_ref[0, SSM_STEPS - 1:SSM_STEPS, :]
        full_i = pwi_ref[0, SSM_STEPS - 1:SSM_STEPS, :]

        def chain(order):
            cr, ci = c_r, c_i
            for j in order:
                hin_r[j:j + 1, :] = cr
                hin_i[j:j + 1, :] = ci
                nr = end_r[j:j + 1, :] + full_r * cr - full_i * ci
                ni = end_i[j:j + 1, :] + full_r * ci + full_i * cr
                cr, ci = nr, ni
            car_r[...] = cr
            car_i[...] = ci

        @pl.when(d == 0)
        def _():
            chain(range(SSM_STREAMS))

        @pl.when(d == 1)
        def _():
            chain(range(SSM_STREAMS - 1, -1, -1))

        g_r = hin_r[...]
        g_i = hin_i[...]

        def fix_step(i, carry):
            p = jnp.where(d == 0, i, SSM_STEPS - 1 - i)
            p_r = jnp.broadcast_to(pwr_ref[0, pl.ds(p, 1), :], (SSM_STREAMS, SSM_STATE))
            p_i = jnp.broadcast_to(pwi_ref[0, pl.ds(p, 1), :], (SSM_STREAMS, SSM_STATE))
            sr = stream_rows(i)
            sre[sr, :] = sre[sr, :] + (p_r * g_r - p_i * g_i)
            sim[sr, :] = sim[sr, :] + (p_r * g_i + p_i * g_r)
            return carry

        lax.fori_loop(0, SSM_STEPS, fix_step, 0, unroll=2)

    for c in range(SSM_UNIT // rows):
        sl = slice(c * rows, (c + 1) * rows)
        y_ref[0, sl, :] = (_dot(sre[sl, :].astype(BF16), cr_ref[0])
                           + _dot(sim[sl, :].astype(BF16), ci_ref[0]))


def _step_major(x, inverse=False):
    lead = x.shape[:-2]
    a, b = (SSM_STEPS, SSM_STREAMS) if inverse else (SSM_STREAMS, SSM_STEPS)
    x = x.reshape(lead + (N_TOK // SSM_UNIT, a, b, x.shape[-1]))
    return jnp.swapaxes(x, -3, -2).reshape(lead + (N_TOK, x.shape[-1]))


def _ssm(u, h0r, h0i, tabs, l):
    n_units = N_TOK // SSM_UNIT
    u = _step_major(u)
    per_dir = lambda a: pl.BlockSpec((None, 1) + a.shape[2:], lambda d, k: (l, d) + (0,) * (a.ndim - 2))
    operands = [h0r, h0i] + [tabs[n] for n in ('a_r', 'a_i', 'pw_r', 'pw_i', 'b_blk', 'c_r', 'c_i')]
    y, hf_r, hf_i = pl.pallas_call(
        _ssm_kernel,
        grid=(N_DIR, n_units),
        in_specs=[pl.BlockSpec((SSM_UNIT, A_WIDTH), lambda d, k: (_ssm_block(d, k), 0))]
                 + [per_dir(a) for a in operands],
        out_specs=[pl.BlockSpec((1, SSM_UNIT, A_WIDTH), lambda d, k: (d, _ssm_block(d, k), 0)),
                   pl.BlockSpec((1, SSM_STREAMS, SSM_STATE), lambda d, k: (d, jnp.minimum(k, 1), 0)),
                   pl.BlockSpec((1, SSM_STREAMS, SSM_STATE), lambda d, k: (d, jnp.minimum(k, 1), 0))],
        out_shape=[jax.ShapeDtypeStruct((N_DIR, N_TOK, A_WIDTH), F32),
                   jax.ShapeDtypeStruct((N_DIR, BATCH, SSM_STATE), F32),
                   jax.ShapeDtypeStruct((N_DIR, BATCH, SSM_STATE), F32)],
        scratch_shapes=[pltpu.VMEM((SSM_UNIT, SSM_STATE), F32), pltpu.VMEM((SSM_UNIT, SSM_STATE), F32),
                        pltpu.VMEM((SSM_STREAMS, SSM_STATE), F32), pltpu.VMEM((SSM_STREAMS, SSM_STATE), F32),
                        pltpu.VMEM((1, SSM_STATE), F32), pltpu.VMEM((1, SSM_STATE), F32)],
        compiler_params=pltpu.CompilerParams(vmem_limit_bytes=VMEM_LIMIT,
                                             dimension_semantics=("arbitrary", "arbitrary")),
        name="ssm_scan",
    )(u, *operands)
    return _step_major(y, inverse=True), hf_r, hf_i


def _attn_kernel(*refs, key_lens):
    n_src = len(key_lens)
    q_ref = refs[0]
    k_refs = refs[1:1 + n_src]
    v_refs = refs[1 + n_src:1 + 2 * n_src]
    o_ref = refs[1 + 2 * n_src]
    s_ref = refs[2 + 2 * n_src]
    chunks = []
    off = 0
    for si, n in enumerate(key_lens):
        for c0 in range(0, n, ATTN_KC):
            cl = min(ATTN_KC, n - c0)
            chunks.append((si, c0, cl, off))
            off += cl
    tq = q_ref.shape[1]
    row_max = []
    n_heads = q_ref.shape[0]
    for hh in range(n_heads):
        q = q_ref[hh]
        mp = jnp.full((tq, LANES), -jnp.inf, F32)
        for si, c0, cl, o in chunks:
            s = lax.dot_general(q, k_refs[si][hh, c0:c0 + cl, :], (((1,), (1,)), ((), ())),
                                preferred_element_type=F32)
            s_ref[hh, :, o:o + cl] = s
            for g in range(cl // LANES):
                mp = jnp.maximum(mp, s[:, g * LANES:(g + 1) * LANES])
        row_max.append(jnp.max(mp, axis=-1, keepdims=True))
    outs = []
    for hh in range(n_heads):
        acc = jnp.zeros((tq, LANES), F32)
        for si, c0, cl, o in chunks:
            p = jnp.exp2(s_ref[hh, :, o:o + cl] - row_max[hh])
            acc = acc + _dot(p.astype(BF16), v_refs[si][hh, c0:c0 + cl, :])
        outs.append(acc / pltpu.roll(acc, MLA_V, 1))
    lane = lax.broadcasted_iota(jnp.int32, (tq, LANES), 1)
    for pair in range(n_heads // 2):
        o_ref[:, pair * LANES:(pair + 1) * LANES] = jnp.where(
            lane < MLA_V, outs[2 * pair], pltpu.roll(outs[2 * pair + 1], MLA_V, 1))


def _attention(q, ks, vs, n_seq, seq_len, tok0, key_lens, key_tok0):
    tq = min(ATTN_TQ, seq_len)
    n_qt = seq_len // tq
    q0 = tok0 // tq
    in_specs = [pl.BlockSpec((ATTN_HEADS, tq, HEAD_PAD), lambda b, hp, qi: (hp, q0 + b * n_qt + qi, 0))]
    for n, t0 in zip(key_lens, key_tok0):
        in_specs.append(pl.BlockSpec((ATTN_HEADS, n, HEAD_PAD), lambda b, hp, qi, n=n, t0=t0: (hp, t0 // n + b, 0)))
    for n, t0 in zip(key_lens, key_tok0):
        in_specs.append(pl.BlockSpec((ATTN_HEADS, n, HEAD_PAD), lambda b, hp, qi, n=n, t0=t0: (hp, t0 // n + b, 0)))
    return pl.pallas_call(
        functools.partial(_attn_kernel, key_lens=tuple(key_lens)),
        grid=(n_seq, MLA_HEADS // ATTN_HEADS, n_qt),
        in_specs=in_specs,
        out_specs=pl.BlockSpec((tq, ATTN_HEADS * MLA_V), lambda b, hp, qi: (b * n_qt + qi, hp)),
        out_shape=jax.ShapeDtypeStruct((n_seq * seq_len, B_WIDTH), F32),
        scratch_shapes=[pltpu.VMEM((ATTN_HEADS, tq, sum(key_lens)), F32)],
        compiler_params=pltpu.CompilerParams(vmem_limit_bytes=VMEM_LIMIT,
                                             dimension_semantics=("arbitrary", "arbitrary", "arbitrary")),
        name="attention",
    )(q, *ks, *vs)


def _merge_kernel(xp_ref, xs_ref, mod_ref, yssm_ref, u_ref, ybp_ref, ybs_ref, gm_ref, dskip_ref, w_glu_ref,
                  vn_ref, w_s_ref, bias_ref, g_ref, w_out_ref, x1_ref, h2_ref):
    t = xp_ref.shape[0]
    latent = pl.program_id(0) >= N_PROMPT // t
    u = u_ref[...]
    y = yssm_ref[0] + yssm_ref[1] + dskip_ref[...] * u
    y = _gelu_tanh(y)
    y_a = y * _sigmoid(_dot(y.astype(BF16), w_glu_ref[...]))
    gm = gm_ref[...]
    vn = (_rms(gm[:, C_WIDTH:]) * vn_ref[...]).astype(BF16)
    lane = lax.broadcasted_iota(jnp.int32, (CHUNK, C_WIDTH), 1)
    mixed = []
    for c in range(t // CHUNK):
        vc = vn[c * CHUNK:(c + 1) * CHUNK, :]
        m = _dot(w_s_ref[0], vc)
        for h in range(1, GMLP_HEADS):
            m = jnp.where(lane >= h * GMLP_CH, _dot(w_s_ref[h], vc), m)
        mixed.append(m + bias_ref[...])
    y_c = gm[:, :C_WIDTH] * jnp.concatenate(mixed, axis=0)
    g = g_ref[...]
    n_a = (_rms(y_a) * g[:, :A_WIDTH]).astype(BF16)
    y_b = jnp.where(latent, ybs_ref[...], ybp_ref[...])
    n_b = (_rms(y_b) * g[:, A_WIDTH:A_WIDTH + B_WIDTH]).astype(BF16)
    n_c = (_rms(y_c) * g[:, A_WIDTH + B_WIDTH:]).astype(BF16)
    o = (_dot(n_a, w_out_ref[0:A_WIDTH, :]) + _dot(n_b, w_out_ref[A_WIDTH:A_WIDTH + B_WIDTH, :])
         + _dot(n_c, w_out_ref[A_WIDTH + B_WIDTH:, :]))
    x1 = jnp.where(latent, xs_ref[...], xp_ref[...]) + mod_ref[0, 2:3, :] * o
    x1_ref[...] = x1
    h2_ref[...] = (_rms(x1) * (1.0 + mod_ref[0, 4:5, :]) + mod_ref[0, 3:4, :]).astype(BF16)


def _merge(x, mod, yssm, u, yb_ctx, yb_lat, gm, w, l):
    t = PROJ_TILE
    tok = lambda width: pl.BlockSpec((t, width), lambda i: (i, 0))
    weights = [w[n] for n in ('ssm_d', 'w_glu', 'gmlp_v_norm', 'gmlp_w_s', 'gmlp_bias', 'w_out_norm', 'w_out')]
    xs, off = _split_args(x, t)
    return pl.pallas_call(
        _merge_kernel,
        grid=(N_TOK // t,),
        in_specs=[*_split_specs(t, D_MODEL, off),
                  pl.BlockSpec((None, 1, N_MOD, D_MODEL), lambda i: (l, _mod_row(i, t), 0, 0)),
                  pl.BlockSpec((N_DIR, t, A_WIDTH), lambda i: (0, i, 0)),
                  tok(A_WIDTH), *_split_specs(t, B_WIDTH, 0), tok(2 * C_WIDTH)]
                 + [_layer_spec(a, l) for a in weights],
        out_specs=[tok(D_MODEL), tok(D_MODEL)],
        out_shape=[jax.ShapeDtypeStruct((N_TOK, D_MODEL), F32),
                   jax.ShapeDtypeStruct((N_TOK, D_MODEL), BF16)],
        compiler_params=pltpu.CompilerParams(vmem_limit_bytes=VMEM_LIMIT),
        name="merge",
    )(*xs, mod, yssm, u, yb_ctx, yb_lat, gm, *weights)


def _ffn_kernel(x1_ref, mod_ref, hp_ref, hm_ref, hn_ref, wup_ref, cw_ref, cb_ref, wd_ref, *rest):
    out_refs, a_scr = rest[:-1], rest[-1]
    seg =pl.program_id(0) - N_PROMPT // FFN_ROWS
    per_seq = DEC_SEQ // FFN_ROWS
    starts = (seg < 0) | ((seg & (per_seq - 1)) == 0)
    ends = (seg < 0) | ((seg & (per_seq - 1)) == per_seq - 1)
    h_prev = hp_ref[...]
    h_next = hn_ref[...]
    h_prev = jnp.where(starts, jnp.zeros_like(h_prev), h_prev)
    h_next = jnp.where(ends, jnp.zeros_like(h_next), h_next)
    he = jnp.concatenate([h_prev, hm_ref[...], h_next], axis=0)
    r0 = FFN_HALO

    def conv(u, c0):
        cols = slice(c0, c0 + FFN_FC)
        return (u[r0 - 1:r0 - 1 + FFN_ROWS] * cw_ref[0:1, cols] + u[r0:r0 + FFN_ROWS] * cw_ref[1:2, cols]
                + u[r0 + 1:r0 + 1 + FFN_ROWS] * cw_ref[2:3, cols] + cb_ref[:, cols])

    for j in range(D_FF // FFN_FC):
        cg = j * FFN_FC
        cv = D_FF + j * FFN_FC
        gate = conv(_dot(he, wup_ref[:, cg:cg + FFN_FC]), cg)
        val = conv(_dot(he, wup_ref[:, cv:cv + FFN_FC]), cv)
        a_scr[:, cg:cg + FFN_FC] = (gate * _sigmoid(gate) * val).astype(BF16)

    out = x1_ref[...] + mod_ref[0, 5:6, :] * _dot(a_scr[...], wd_ref[...])
    if len(out_refs) == 1:
        out_refs[0][...] = out
    else:
        @pl.when(seg < 0)
        def _():
            out_refs[0][...] = out

        @pl.when(seg >= 0)
        def _():
            out_refs[1][...] = out


def _ffn(x1, mod, h2, w, l, split_out):
    t = FFN_ROWS
    hb = t // FFN_HALO
    last_halo = N_TOK // FFN_HALO - 1
    weights = [w[n] for n in ('ffn_w_up', 'ffn_conv_w', 'ffn_conv_b', 'ffn_w_down')]
    if split_out:
        out_specs = list(_split_specs(t, D_MODEL, 0))
        out_shape = [jax.ShapeDtypeStruct((N_PROMPT, D_MODEL), F32), jax.ShapeDtypeStruct((N_SAMPLE, D_MODEL), F32)]
    else:
        out_specs = pl.BlockSpec((t, D_MODEL), lambda i: (i, 0))
        out_shape = jax.ShapeDtypeStruct((N_TOK, D_MODEL), F32)
    return pl.pallas_call(
        _ffn_kernel,
        grid=(N_TOK // t,),
        in_specs=[pl.BlockSpec((t, D_MODEL), lambda i: (i, 0)),
                  pl.BlockSpec((None, 1, N_MOD, D_MODEL), lambda i: (l, _mod_row(i, t), 0, 0)),
                  pl.BlockSpec((FFN_HALO, D_MODEL), lambda i: (jnp.maximum(i * hb - 1, 0), 0)),
                  pl.BlockSpec((t, D_MODEL), lambda i: (i, 0)),
                  pl.BlockSpec((FFN_HALO, D_MODEL), lambda i: (jnp.minimum((i + 1) * hb, last_halo), 0))]
                 + [_layer_spec(a, l, pipeline_mode=pl.Buffered(1)) for a in weights],
        out_specs=out_specs,
        out_shape=out_shape,
        scratch_shapes=[pltpu.VMEM((t, D_FF), BF16)],
        compiler_params=pltpu.CompilerParams(vmem_limit_bytes=VMEM_LIMIT,
                                             dimension_semantics=("arbitrary",)),
        name="conv_ffn",
    )(x1, mod, h2, h2, h2, *weights)


def _pad_heads(w, head_w, used, offset=0):
    lead = w.shape[:-1]
    w = w.reshape(lead + (MLA_HEADS, head_w))[..., :used]
    pad = [(0, 0)] * (len(lead) + 1) + [(offset, HEAD_PAD - used - offset)]
    return jnp.pad(w, pad).reshape(lead + (MLA_HEADS * HEAD_PAD,))


def _head_gain(g):
    return jnp.pad(g, ((0, 0), (0, HEAD_PAD - MLA_QK)))[:, None, :]


def _ssm_tables(a_re, a_im, b_re, b_im, c_re, c_im, log_dt):
    a = lax.complex(a_re, a_im)
    dt = jnp.exp(log_dt)[..., None]
    a_bar = jnp.exp(a * dt)
    b_bar = ((a_bar - 1.0) / a)[..., None] * lax.complex(b_re, b_im)
    steps = jnp.arange(1, SSM_STEPS + 1, dtype=F32)[None, None, :, None, None]
    pw = jnp.exp((a * dt)[:, :, None] * steps)
    eye = jnp.eye(SSM_G, dtype=F32)
    blk_b = lambda m: jnp.einsum('ldgpc,gh->ldgchp', m, eye).reshape(DEPTH, N_DIR, A_WIDTH, SSM_STATE)
    blk_c = lambda m: jnp.einsum('ldgcp,gh->ldgphc', m, eye).reshape(DEPTH, N_DIR, SSM_STATE, A_WIDTH)
    flat = lambda m: m.reshape(DEPTH, N_DIR, 1, SSM_STATE)
    return {
        'a_r': flat(a_bar.real), 'a_i': flat(a_bar.imag),
        'pw_r': pw.real.reshape(DEPTH, N_DIR, SSM_STEPS, SSM_STATE),
        'pw_i': pw.imag.reshape(DEPTH, N_DIR, SSM_STEPS, SSM_STATE),
        'b_blk': jnp.concatenate([blk_b(b_bar.real), blk_b(b_bar.imag)], axis=-1).astype(BF16),
        'c_r': blk_c(c_re).astype(BF16),
        'c_i': blk_c(-c_im).astype(BF16),
    }


def _rope_tables():
    rows = DEC_SEQ // GRID_W
    row = np.repeat(np.arange(rows, dtype=np.float32), GRID_W)
    col = np.tile(np.arange(GRID_W, dtype=np.float32), rows)
    n_freq = MLA_ROPE // 4
    inv = (np.float32(ROPE_BASE) ** (-np.arange(n_freq, dtype=np.float32) / np.float32(n_freq))).astype(np.float32)
    ang = np.concatenate([row[:, None] * inv, col[:, None] * inv], axis=-1).astype(np.float32)
    cos, sin = np.cos(ang), np.sin(ang)
    cos_t = np.ones((DEC_SEQ, HEAD_PAD), np.float32)
    sin_t = np.zeros((DEC_SEQ, HEAD_PAD), np.float32)
    cos_t[:, ROPE_LANE0:ROPE_LANE0 + MLA_ROPE] = np.concatenate([cos, cos], axis=-1)
    sin_t[:, ROPE_LANE0:ROPE_LANE0 + MLA_ROPE] = np.concatenate([sin, sin], axis=-1)
    return jnp.asarray(cos_t), jnp.asarray(sin_t)


def _rope_partner(r):
    half = MLA_ROPE // 2
    return jnp.concatenate([-r[..., half:], r[..., :half]], axis=-1)


def _rope_lanes(r):
    return jnp.pad(r, [(0, 0)] * (r.ndim - 1) + [(ROPE_LANE0, HEAD_PAD - ROPE_LANE0 - MLA_ROPE)])


def _head_gain_partner(g):
    half = MLA_ROPE // 2
    r = g[:, MLA_NOPE:]
    return _rope_lanes(jnp.concatenate([r[:, half:], r[:, :half]], axis=-1))[:, None, :]


def _prepare_weights(w_in, ssm_d, ssm_w_glu, q_a_norm, kv_a_norm, w_uq, w_ukv, q_norm, k_norm,
                     gmlp_v_norm, gmlp_w_s, gmlp_b_s, w_out_norm, w_out, ffn_w_up, ffn_conv_w,
                     ffn_conv_b, ffn_w_down):
    w_kr = w_in[:, :, OFF_KR:OFF_GM]
    ukv = w_ukv.reshape(DEPTH, KV_RANK, MLA_HEADS, MLA_NOPE + MLA_V)
    uq = w_uq.reshape(DEPTH, Q_RANK, MLA_HEADS, MLA_QK)
    q_gain = q_norm * Q_SCALE
    row = lambda a: a[:, None, :]
    return {
        'w_in_a': jnp.concatenate([w_in[:, :, OFF_SSM:OFF_KR], w_in[:, :, OFF_GM:]], axis=2).astype(BF16),
        'w_in_kr': _rope_lanes(w_kr).astype(BF16),
        'w_in_krp': _rope_lanes(_rope_partner(w_kr)).astype(BF16),
        'w_uqp': _rope_lanes(_rope_partner(uq[..., MLA_NOPE:])).reshape(DEPTH, Q_RANK, -1).astype(BF16),
        'q_a_norm': row(q_a_norm), 'kv_a_norm': row(kv_a_norm),
        'w_uq': _pad_heads(w_uq, MLA_QK, MLA_QK).astype(BF16),
        'w_uk': _pad_heads(ukv[..., :MLA_NOPE].reshape(DEPTH, KV_RANK, -1), MLA_NOPE, MLA_NOPE).astype(BF16),
        'w_uv': _pad_heads(ukv[..., MLA_NOPE:].reshape(DEPTH, KV_RANK, -1), MLA_V, MLA_V).astype(BF16),
        'q_norm': _head_gain(q_gain), 'q_norm_p': _head_gain_partner(q_gain),
        'k_norm': _head_gain(k_norm), 'k_norm_p': _head_gain_partner(k_norm),
        'ssm_d': row(ssm_d), 'w_glu': ssm_w_glu.astype(BF16),
        'gmlp_v_norm': row(gmlp_v_norm), 'gmlp_w_s': gmlp_w_s.astype(BF16),
        'gmlp_bias': jnp.repeat(jnp.swapaxes(gmlp_b_s, 1, 2), GMLP_CH, axis=2),
        'w_out_norm': row(w_out_norm), 'w_out': w_out.astype(BF16),
        'ffn_w_up': ffn_w_up.astype(BF16), 'ffn_conv_w': ffn_conv_w,
        'ffn_conv_b': row(ffn_conv_b), 'ffn_w_down': ffn_w_down.astype(BF16),
    }


def kernel(x_prompt, x_sample, cache_ckv, cache_krope, state_ssm_re, state_ssm_im, c, c_ctx, w_mod, b_mod, w_in, ssm_a_re, ssm_a_im, ssm_b_re, ssm_b_im, ssm_c_re, ssm_c_im, ssm_log_dt, ssm_d, ssm_w_glu, q_a_norm, kv_a_norm, w_uq, w_ukv, q_norm, k_norm, gmlp_v_norm, gmlp_w_s, gmlp_b_s, w_out_norm, w_out, ffn_w_up, ffn_conv_w, ffn_conv_b, ffn_w_down):
    cond8 = jnp.concatenate([c_ctx[None, :], c, jnp.zeros((8 - 1 - DEC_BATCH, D_MODEL), F32)], axis=0)
    mod = _modulation(cond8, w_mod, b_mod).reshape(DEPTH, 8, N_MOD, D_MODEL)
    rope_tabs = _rope_tables()
    w = _prepare_weights(w_in, ssm_d, ssm_w_glu, q_a_norm, kv_a_norm, w_uq, w_ukv, q_norm, k_norm,
                         gmlp_v_norm, gmlp_w_s, gmlp_b_s, w_out_norm, w_out, ffn_w_up, ffn_conv_w,
                         ffn_conv_b, ffn_w_down)
    tabs = _ssm_tables(ssm_a_re, ssm_a_im, ssm_b_re, ssm_b_im, ssm_c_re, ssm_c_im, ssm_log_dt)
    cache_kr = _rope_lanes(cache_krope)

    def h0(st):
        st = jnp.transpose(st.reshape(DEC_BATCH, DEPTH, N_DIR, SSM_STATE), (1, 2, 0, 3))
        return jnp.pad(st, ((0, 0), (0, 0), (0, SSM_STREAMS - DEC_BATCH), (0, 0)))

    h0r, h0i = h0(state_ssm_re), h0(state_ssm_im)

    x = (x_prompt.reshape(N_PROMPT, D_MODEL), x_sample.reshape(N_SAMPLE, D_MODEL))
    ckv_out, kr_out, hre_out, him_out = [], [], [], []
    for l in range(DEPTH):
        u, gm, ckv, kr, q, k, v = _projection(x, mod, rope_tabs, w, l)
        k_c, v_c = _ctx_kv(cache_ckv, cache_kr, w, l)
        yb_ctx = _attention(q, [k], [v], BATCH, SEQ, 0, [SEQ], [0])
        yssm, hf_r, hf_i = _ssm(u, h0r, h0i, tabs, l)
        yb_lat = _attention(q, [k_c, k], [v_c, v], DEC_BATCH, DEC_SEQ, N_PROMPT,
                            [PAST_LEN, DEC_SEQ], [0, N_PROMPT])
        x1, h2 = _merge(x, mod, yssm, u, yb_ctx, yb_lat, gm, w, l)
        x = _ffn(x1, mod, h2, w, l, split_out=(l == DEPTH - 1))

        ckv_out.append(ckv.reshape(BATCH, SEQ, KV_RANK))
        kr_out.append(kr[:, ROPE_LANE0:ROPE_LANE0 + MLA_ROPE].reshape(BATCH, SEQ, MLA_ROPE))
        to_state = lambda st: jnp.transpose(st, (1, 0, 2)).reshape(BATCH, N_DIR, SSM_G, SSM_P)
        hre_out.append(to_state(hf_r))
        him_out.append(to_state(hf_i))

    y_prompt = x[0].reshape(BATCH, SEQ, D_MODEL)
    y_sample = x[1].reshape(DEC_BATCH, DEC_SEQ, D_MODEL)
    return (y_prompt, y_sample, jnp.stack(ckv_out, axis=1), jnp.stack(kr_out, axis=1),
            jnp.stack(hre_out, axis=1), jnp.stack(him_out, axis=1))
```

```python
import functools
import math

import jax
import jax.numpy as jnp
import numpy as np
from jax import lax
from jax.experimental import pallas as pl
from jax.experimental.pallas import tpu as pltpu

F32 = jnp.float32
BF16 = jnp.bfloat16

D_MODEL = 1024
BATCH = 16
SEQ = 256
DEPTH = 2
DEC_BATCH = 2
DEC_SEQ = 4096
PAST_LEN = 256
GRID_W = 64
EPS = 1e-6
N_MOD = 6
A_WIDTH = 256
B_WIDTH = 512
C_WIDTH = 256
SSM_CG = 16
SSM_G = 16
SSM_P = 64
N_DIR = 2
MLA_HEADS = 8
MLA_NOPE = 64
MLA_ROPE = 32
MLA_V = 64
MLA_QK = 96
Q_RANK = 256
KV_RANK = 128
ROPE_BASE = 10000.0
ATTN_SCALE = 1.0 / math.sqrt(MLA_QK)
Q_SCALE = ATTN_SCALE * math.log2(math.e)
GMLP_HEADS = 4
GMLP_CH = 64
CHUNK = 128
OFF_SSM = 0
OFF_Q = OFF_SSM + A_WIDTH
OFF_KV = OFF_Q + Q_RANK
OFF_KR = OFF_KV + KV_RANK
OFF_GM = OFF_KR + MLA_ROPE
D_FF = 2816

N_PROMPT = BATCH * SEQ
N_SAMPLE = DEC_BATCH * DEC_SEQ
N_TOK = N_PROMPT + N_SAMPLE
SSM_STATE = SSM_G * SSM_P

LANES = 128
HEAD_PAD = LANES
ROPE_LANE0 = MLA_NOPE

PROJ_TILE = 512
PROJ_SUB = 256
SSM_STREAMS = 8
SSM_STEPS = 256
SSM_UNIT = SSM_STREAMS * SSM_STEPS
ATTN_TQ = 256
ATTN_KC = 1024
ATTN_HEADS = 4
FFN_ROWS = SEQ
FFN_FC = 256
FFN_HALO = 16
VMEM_LIMIT = 48 * 1024 * 1024


def _mod_row(tile, tile_tokens):
    start = tile * tile_tokens
    return jnp.where(start < N_PROMPT, 0, 1 + (start - N_PROMPT) // DEC_SEQ)


def _layer_spec(a, l, **kw):
    return pl.BlockSpec((None,) + a.shape[1:], lambda *_: (l,) + (0,) * (a.ndim - 1), **kw)


def _split_specs(t, width, second_offset):
    n_a = N_PROMPT // t
    return (pl.BlockSpec((t, width), lambda i: (jnp.minimum(i, n_a - 1), 0)),
            pl.BlockSpec((t, width), lambda i: (jnp.maximum(i - n_a, 0) + second_offset, 0)))


def _split_args(x, t):
    if isinstance(x, tuple):
        return x, 0
    return (x, x), N_PROMPT // t


def _rms(x):
    return x * lax.rsqrt(jnp.mean(x * x, axis=-1, keepdims=True) + EPS)


def _sigmoid(x):
    return 1.0 / (1.0 + jnp.exp(-x))


def _gelu_tanh(x):
    return 0.5 * x * (1.0 + jnp.tanh(math.sqrt(2.0 / math.pi) * (x + 0.044715 * (x * x * x))))


def _dot(a, b):
    return jnp.dot(a, b, preferred_element_type=F32)


def _mod_kernel(cond_ref, w_ref, b_ref, o_ref):
    cond = cond_ref[...]
    s = (cond * _sigmoid(cond)).astype(BF16)
    o_ref[0] = _dot(s, w_ref[0].astype(BF16)) + b_ref[0]


def _modulation(cond8, w_mod, b_mod):
    tn = 1536
    n_cols = N_MOD * D_MODEL
    return pl.pallas_call(
        _mod_kernel,
        grid=(DEPTH, n_cols // tn),
        in_specs=[
            pl.BlockSpec((8, D_MODEL), lambda l, j: (0, 0)),
            pl.BlockSpec((1, D_MODEL, tn), lambda l, j: (l, 0, j)),
            pl.BlockSpec((1, 1, tn), lambda l, j: (l, 0, j)),
        ],
        out_specs=pl.BlockSpec((1, 8, tn), lambda l, j: (l, 0, j)),
        out_shape=jax.ShapeDtypeStruct((DEPTH, 8, n_cols), F32),
        compiler_params=pltpu.CompilerParams(vmem_limit_bytes=VMEM_LIMIT),
        name="modulation",
    )(cond8, w_mod, b_mod.reshape(DEPTH, 1, n_cols))


def _write_heads(allh, extra, gain, partner, out_ref, rows=slice(None)):
    for h in range(MLA_HEADS):
        tile = slice(h * HEAD_PAD, (h + 1) * HEAD_PAD)
        t = allh[:, tile]
        if extra is not None:
            t = t + extra
        rs = lax.rsqrt(jnp.sum(t * t, axis=-1, keepdims=True) * (1.0 / MLA_QK) + EPS)
        o = t * gain
        if isinstance(partner, tuple):
            o = o + partner[0][:, tile] * partner[1]
        elif partner is not None:
            o = o + partner
        out_ref[h, rows, :] = (o * rs).astype(BF16)


def _write_values(vall, v_ref, rows=slice(None)):
    lane = lax.broadcasted_iota(jnp.int32, (1, HEAD_PAD), 1)
    ones = jnp.where(lane >= MLA_V, 1.0, 0.0)
    for h in range(MLA_HEADS):
        v_ref[h, rows, :] = (vall[:, h * HEAD_PAD:(h + 1) * HEAD_PAD] + ones).astype(BF16)


def _proj_kernel(xp_ref, xs_ref, mod_ref, cos_ref, sin_ref, w_in_ref, w_kr_ref, w_krp_ref, qan_ref, kvn_ref,
                 w_uq_ref, w_uqp_ref, w_uk_ref, w_uv_ref, qn_ref, qnp_ref, kn_ref, knp_ref,
                 u_ref, gm_ref, ckv_ref, kr_ref, q_ref, k_ref, v_ref):
    latent = pl.program_id(0) >= N_PROMPT // PROJ_TILE
    shift = mod_ref[0, 0:1, :]
    scale = mod_ref[0, 1:2, :]

    def body(x_ref, rope):
        for sub in range(PROJ_TILE // PROJ_SUB):
            rows = slice(sub * PROJ_SUB, (sub + 1) * PROJ_SUB)
            h = (_rms(x_ref[rows, :]) * (1.0 + scale) + shift).astype(BF16)
            z = _dot(h, w_in_ref[...])
            kr = _dot(h, w_kr_ref[...])
            u_ref[rows, :] = z[:, 0:A_WIDTH]
            gm_ref[rows, :] = z[:, A_WIDTH + Q_RANK + KV_RANK:]
            ckv = _rms(z[:, A_WIDTH + Q_RANK:A_WIDTH + Q_RANK + KV_RANK]) * kvn_ref[...]
            cq = (_rms(z[:, A_WIDTH:A_WIDTH + Q_RANK]) * qan_ref[...]).astype(BF16)
            ckv_b = ckv.astype(BF16)
            qall = _dot(cq, w_uq_ref[...])
            kall = _dot(ckv_b, w_uk_ref[...])
            _write_values(_dot(ckv_b, w_uv_ref[...]), v_ref, rows)
            if rope:
                cos = cos_ref[rows, :]
                sin = sin_ref[rows, :]
                _write_heads(qall, None, qn_ref[...] * cos, (_dot(cq, w_uqp_ref[...]), qnp_ref[...] * sin),
                             q_ref, rows)
                _write_heads(kall, kr, kn_ref[...] * cos, _dot(h, w_krp_ref[...]) * (knp_ref[...] * sin),
                             k_ref, rows)
            else:
                ckv_ref[rows, :] = ckv
                kr_ref[rows, :] = kr
                _write_heads(qall, None, qn_ref[...], None, q_ref, rows)
                _write_heads(kall, kr, kn_ref[...], None, k_ref, rows)

    pl.when(latent)(lambda: body(xs_ref, True))
    pl.when(jnp.logical_not(latent))(lambda: body(xp_ref, False))


def _projection(x, mod, rope_tabs, w, l):
    t = PROJ_TILE
    n_tiles = N_TOK // t
    n_ctx = N_PROMPT // t
    tok = lambda width: pl.BlockSpec((t, width), lambda i: (i, 0))
    ctx = lambda width: pl.BlockSpec((t, width), lambda i: (jnp.minimum(i, n_ctx - 1), 0))
    heads = pl.BlockSpec((MLA_HEADS, t, HEAD_PAD), lambda i: (0, i, 0))
    pos = pl.BlockSpec((t, HEAD_PAD), lambda i: (jnp.maximum(i - n_ctx, 0) % (DEC_SEQ // t), 0))
    weights = [w[n] for n in ('w_in_a', 'w_in_kr', 'w_in_krp', 'q_a_norm', 'kv_a_norm', 'w_uq', 'w_uqp',
                              'w_uk', 'w_uv', 'q_norm', 'q_norm_p', 'k_norm', 'k_norm_p')]
    xs, off = _split_args(x, t)
    head_shape = jax.ShapeDtypeStruct((MLA_HEADS, N_TOK, HEAD_PAD), BF16)
    return pl.pallas_call(
        _proj_kernel,
        grid=(n_tiles,),
        in_specs=[*_split_specs(t, D_MODEL, off),
                  pl.BlockSpec((None, 1, N_MOD, D_MODEL), lambda i: (l, _mod_row(i, t), 0, 0)),
                  pos, pos] + [_layer_spec(a, l) for a in weights],
        out_specs=[tok(A_WIDTH), tok(2 * C_WIDTH), ctx(KV_RANK), ctx(HEAD_PAD), heads, heads, heads],
        out_shape=[
            jax.ShapeDtypeStruct((N_TOK, A_WIDTH), F32),
            jax.ShapeDtypeStruct((N_TOK, 2 * C_WIDTH), F32),
            jax.ShapeDtypeStruct((N_PROMPT, KV_RANK), F32),
            jax.ShapeDtypeStruct((N_PROMPT, HEAD_PAD), F32),
            head_shape, head_shape, head_shape,
        ],
        compiler_params=pltpu.CompilerParams(vmem_limit_bytes=VMEM_LIMIT),
        name="projection",
    )(*xs, mod, *rope_tabs, *weights)


def _ctx_kv_kernel(ckv_ref, kr_ref, w_uk_ref, w_uv_ref, kn_ref, k_ref, v_ref):
    n = DEC_BATCH * PAST_LEN
    ckv_b = ckv_ref[...].reshape(n, KV_RANK).astype(BF16)
    _write_heads(_dot(ckv_b, w_uk_ref[...]), kr_ref[...].reshape(n, HEAD_PAD), kn_ref[...], None, k_ref)
    _write_values(_dot(ckv_b, w_uv_ref[...]), v_ref)


def _ctx_kv(cache_ckv, cache_kr, w, l):
    n = DEC_BATCH * PAST_LEN
    cache = pl.BlockSpec((DEC_BATCH, None, PAST_LEN, HEAD_PAD), lambda i: (0, l, 0, 0))
    weights = [w['w_uk'], w['w_uv'], w['k_norm']]
    heads = pl.BlockSpec((MLA_HEADS, n, HEAD_PAD), lambda i: (0, 0, 0))
    head_shape = jax.ShapeDtypeStruct((MLA_HEADS, n, HEAD_PAD), BF16)
    return pl.pallas_call(
        _ctx_kv_kernel,
        grid=(1,),
        in_specs=[cache, cache] + [_layer_spec(a, l) for a in weights],
        out_specs=[heads, heads],
        out_shape=[head_shape, head_shape],
        compiler_params=pltpu.CompilerParams(vmem_limit_bytes=VMEM_LIMIT),
        name="context_kv",
    )(cache_ckv, cache_kr, *weights)


def _ssm_block(d, k):
    return jnp.where((d == 1) & (k >= 2), k ^ 1, k)


def _ssm_kernel(u_ref, h0r_ref, h0i_ref, ar_ref, ai_ref, pwr_ref, pwi_ref, bm_ref, cr_ref, ci_ref,
                y_ref, hfr_ref, hfi_ref, sre, sim, hin_r, hin_i, car_r, car_i):
    d = pl.program_id(0)
    k = pl.program_id(1)
    rows = 256

    @pl.when(k == 0)
    def _():
        car_r[...] = jnp.zeros_like(car_r)
        car_i[...] = jnp.zeros_like(car_i)

    for c in range(SSM_UNIT // rows):
        sl = slice(c * rows, (c + 1) * rows)
        bu = _dot(u_ref[sl, :].astype(BF16), bm_ref[0])
        sre[sl, :] = bu[:, :SSM_STATE]
        sim[sl, :] = bu[:, SSM_STATE:]

    a_r = jnp.broadcast_to(ar_ref[0], (SSM_STREAMS, SSM_STATE))
    a_i = jnp.broadcast_to(ai_ref[0], (SSM_STREAMS, SSM_STATE))

    def stream_rows(tau):
        return pl.ds(pl.multiple_of(tau * SSM_STREAMS, SSM_STREAMS), SSM_STREAMS)

    def scan_step(i, carry):
        h_r, h_i = carry
        tau = jnp.where(d == 0, i, SSM_STEPS - 1 - i)
        n_r = a_r * h_r - a_i * h_i + sre[stream_rows(tau), :]
        n_i = a_r * h_i + a_i * h_r + sim[stream_rows(tau), :]
        sre[stream_rows(tau), :] = n_r
        sim[stream_rows(tau), :] = n_i
        return n_r, n_i

    zero = jnp.zeros((SSM_STREAMS, SSM_STATE), F32)
    end_r, end_i = lax.fori_loop(0, SSM_STEPS, scan_step, (zero, zero), unroll=2)

    @pl.when(k < 2)
    def _():
        hfr_ref[0] = end_r
        hfi_ref[0] = end_i

    @pl.when(k >= 2)
    def _():
        b = (k - 2) // 2
        first = (k % 2) == 0
        c_r = jnp.where(first, h0r_ref[0, pl.ds(b, 1), :], car_r[...])
        c_i = jnp.where(first, h0i_ref[0, pl.ds(b, 1), :], car_i[...])
        full_r = pwr_ref[0, SSM_STEPS - 1:SSM_STEPS, :]
        full_i = pwi_ref[0, SSM_STEPS - 1:SSM_STEPS, :]

        def chain(order):
            cr, ci = c_r, c_i
            for j in order:
                hin_r[j:j + 1, :] = cr
                hin_i[j:j + 1, :] = ci
                nr = end_r[j:j + 1, :] + full_r * cr - full_i * ci
                ni = end_i[j:j + 1, :] + full_r * ci + full_i * cr
                cr, ci = nr, ni
            car_r[...] = cr
            car_i[...] = ci

        @pl.when(d == 0)
        def _():
            chain(range(SSM_STREAMS))

        @pl.when(d == 1)
        def _():
            chain(range(SSM_STREAMS - 1, -1, -1))

        g_r = hin_r[...]
        g_i = hin_i[...]

        def fix_step(i, carry):
            p = jnp.where(d == 0, i, SSM_STEPS - 1 - i)
            p_r = jnp.broadcast_to(pwr_ref[0, pl.ds(p, 1), :], (SSM_STREAMS, SSM_STATE))
            p_i = jnp.broadcast_to(pwi_ref[0, pl.ds(p, 1), :], (SSM_STREAMS, SSM_STATE))
            sr = stream_rows(i)
            sre[sr, :] = sre[sr, :] + (p_r * g_r - p_i * g_i)
            sim[sr, :] = sim[sr, :] + (p_r * g_i + p_i * g_r)
            return carry

        lax.fori_loop(0, SSM_STEPS, fix_step, 0, unroll=2)

    for c in range(SSM_UNIT // rows):
        sl = slice(c * rows, (c + 1) * rows)
        y_ref[0, sl, :] = (_dot(sre[sl, :].astype(BF16), cr_ref[0])
                           + _dot(sim[sl, :].astype(BF16), ci_ref[0]))


def _step_major(x, inverse=False):
    lead = x.shape[:-2]
    a, b = (SSM_STEPS, SSM_STREAMS) if inverse else (SSM_STREAMS, SSM_STEPS)
    x = x.reshape(lead + (N_TOK // SSM_UNIT, a, b, x.shape[-1]))
    return jnp.swapaxes(x, -3, -2).reshape(lead + (N_TOK, x.shape[-1]))


def _ssm(u, h0r, h0i, tabs, l):
    n_units = N_TOK // SSM_UNIT
    u = _step_major(u)
    per_dir = lambda a: pl.BlockSpec((None, 1) + a.shape[2:], lambda d, k: (l, d) + (0,) * (a.ndim - 2))
    operands = [h0r, h0i] + [tabs[n] for n in ('a_r', 'a_i', 'pw_r', 'pw_i', 'b_blk', 'c_r', 'c_i')]
    y, hf_r, hf_i = pl.pallas_call(
        _ssm_kernel,
        grid=(N_DIR, n_units),
        in_specs=[pl.BlockSpec((SSM_UNIT, A_WIDTH), lambda d, k: (_ssm_block(d, k), 0))]
                 + [per_dir(a) for a in operands],
        out_specs=[pl.BlockSpec((1, SSM_UNIT, A_WIDTH), lambda d, k: (d, _ssm_block(d, k), 0)),
                   pl.BlockSpec((1, SSM_STREAMS, SSM_STATE), lambda d, k: (d, jnp.minimum(k, 1), 0)),
                   pl.BlockSpec((1, SSM_STREAMS, SSM_STATE), lambda d, k: (d, jnp.minimum(k, 1), 0))],
        out_shape=[jax.ShapeDtypeStruct((N_DIR, N_TOK, A_WIDTH), F32),
                   jax.ShapeDtypeStruct((N_DIR, BATCH, SSM_STATE), F32),
                   jax.ShapeDtypeStruct((N_DIR, BATCH, SSM_STATE), F32)],
        scratch_shapes=[pltpu.VMEM((SSM_UNIT, SSM_STATE), F32), pltpu.VMEM((SSM_UNIT, SSM_STATE), F32),
                        pltpu.VMEM((SSM_STREAMS, SSM_STATE), F32), pltpu.VMEM((SSM_STREAMS, SSM_STATE), F32),
                        pltpu.VMEM((1, SSM_STATE), F32), pltpu.VMEM((1, SSM_STATE), F32)],
        compiler_params=pltpu.CompilerParams(vmem_limit_bytes=VMEM_LIMIT,
                                             dimension_semantics=("arbitrary", "arbitrary")),
        name="ssm_scan",
    )(u, *operands)
    return _step_major(y, inverse=True), hf_r, hf_i


def _attn_kernel(*refs, key_lens):
    n_src = len(key_lens)
    q_ref = refs[0]
    k_refs = refs[1:1 + n_src]
    v_refs = refs[1 + n_src:1 + 2 * n_src]
    o_ref = refs[1 + 2 * n_src]
    s_ref = refs[2 + 2 * n_src]
    chunks = []
    off = 0
    for si, n in enumerate(key_lens):
        for c0 in range(0, n, ATTN_KC):
            cl = min(ATTN_KC, n - c0)
            chunks.append((si, c0, cl, off))
            off += cl
    tq = q_ref.shape[1]
    row_max = []
    n_heads = q_ref.shape[0]
    for hh in range(n_heads):
        q = q_ref[hh]
        mp = jnp.full((tq, LANES), -jnp.inf, F32)
        for si, c0, cl, o in chunks:
            s = lax.dot_general(q, k_refs[si][hh, c0:c0 + cl, :], (((1,), (1,)), ((), ())),
                                preferred_element_type=F32)
            s_ref[hh, :, o:o + cl] = s
            for g in range(cl // LANES):
                mp = jnp.maximum(mp, s[:, g * LANES:(g + 1) * LANES])
        row_max.append(jnp.max(mp, axis=-1, keepdims=True))
    outs = []
    for hh in range(n_heads):
        acc = jnp.zeros((tq, LANES), F32)
        for si, c0, cl, o in chunks:
            p = jnp.exp2(s_ref[hh, :, o:o + cl] - row_max[hh])
            acc = acc + _dot(p.astype(BF16), v_refs[si][hh, c0:c0 + cl, :])
        outs.append(acc / pltpu.roll(acc, MLA_V, 1))
    lane = lax.broadcasted_iota(jnp.int32, (tq, LANES), 1)
    for pair in range(n_heads // 2):
        o_ref[:, pair * LANES:(pair + 1) * LANES] = jnp.where(
            lane < MLA_V, outs[2 * pair], pltpu.roll(outs[2 * pair + 1], MLA_V, 1))


def _attention(q, ks, vs, n_seq, seq_len, tok0, key_lens, key_tok0):
    tq = min(ATTN_TQ, seq_len)
    n_qt = seq_len // tq
    q0 = tok0 // tq
    in_specs = [pl.BlockSpec((ATTN_HEADS, tq, HEAD_PAD), lambda b, hp, qi: (hp, q0 + b * n_qt + qi, 0))]
    for n, t0 in zip(key_lens, key_tok0):
        in_specs.append(pl.BlockSpec((ATTN_HEADS, n, HEAD_PAD), lambda b, hp, qi, n=n, t0=t0: (hp, t0 // n + b, 0)))
    for n, t0 in zip(key_lens, key_tok0):
        in_specs.append(pl.BlockSpec((ATTN_HEADS, n, HEAD_PAD), lambda b, hp, qi, n=n, t0=t0: (hp, t0 // n + b, 0)))
    return pl.pallas_call(
        functools.partial(_attn_kernel, key_lens=tuple(key_lens)),
        grid=(n_seq, MLA_HEADS // ATTN_HEADS, n_qt),
        in_specs=in_specs,
        out_specs=pl.BlockSpec((tq, ATTN_HEADS * MLA_V), lambda b, hp, qi: (b * n_qt + qi, hp)),
        out_shape=jax.ShapeDtypeStruct((n_seq * seq_len, B_WIDTH), F32),
        scratch_shapes=[pltpu.VMEM((ATTN_HEADS, tq, sum(key_lens)), F32)],
        compiler_params=pltpu.CompilerParams(vmem_limit_bytes=VMEM_LIMIT,
                                             dimension_semantics=("arbitrary", "arbitrary", "arbitrary")),
        name="attention",
    )(q, *ks, *vs)


def _merge_kernel(xp_ref, xs_ref, mod_ref, yssm_ref, u_ref, ybp_ref, ybs_ref, gm_ref, dskip_ref, w_glu_ref,
                  vn_ref, w_s_ref, bias_ref, g_ref, w_out_ref, x1_ref, h2_ref):
    t = xp_ref.shape[0]
    latent = pl.program_id(0) >= N_PROMPT // t
    lane = lax.broadcasted_iota(jnp.int32, (CHUNK, C_WIDTH), 1)
    g = g_ref[...]

    def body(x_ref, yb_ref):
        for sub in range(t // PROJ_SUB):
            rows = slice(sub * PROJ_SUB, (sub + 1) * PROJ_SUB)
            y = yssm_ref[0, rows, :] + yssm_ref[1, rows, :] + dskip_ref[...] * u_ref[rows, :]
            y = _gelu_tanh(y)
            y_a = y * _sigmoid(_dot(y.astype(BF16), w_glu_ref[...]))
            gm = gm_ref[rows, :]
            vn = (_rms(gm[:, C_WIDTH:]) * vn_ref[...]).astype(BF16)
            mixed = []
            for c in range(PROJ_SUB // CHUNK):
                vc = vn[c * CHUNK:(c + 1) * CHUNK, :]
                m = _dot(w_s_ref[0], vc)
                for h in range(1, GMLP_HEADS):
                    m = jnp.where(lane >= h * GMLP_CH, _dot(w_s_ref[h], vc), m)
                mixed.append(m + bias_ref[...])
            y_c = gm[:, :C_WIDTH] * jnp.concatenate(mixed, axis=0)
            n_a = (_rms(y_a) * g[:, :A_WIDTH]).astype(BF16)
            n_b = (_rms(yb_ref[rows, :]) * g[:, A_WIDTH:A_WIDTH + B_WIDTH]).astype(BF16)
            n_c = (_rms(y_c) * g[:, A_WIDTH + B_WIDTH:]).astype(BF16)
            o = (_dot(n_a, w_out_ref[0:A_WIDTH, :]) + _dot(n_b, w_out_ref[A_WIDTH:A_WIDTH + B_WIDTH, :])
                 + _dot(n_c, w_out_ref[A_WIDTH + B_WIDTH:, :]))
            x1 = x_ref[rows, :] + mod_ref[0, 2:3, :] * o
            x1_ref[rows, :] = x1
            h2_ref[rows, :] = (_rms(x1) * (1.0 + mod_ref[0, 4:5, :]) + mod_ref[0, 3:4, :]).astype(BF16)

    pl.when(latent)(lambda: body(xs_ref, ybs_ref))
    pl.when(jnp.logical_not(latent))(lambda: body(xp_ref, ybp_ref))


def _merge(x, mod, yssm, u, yb_ctx, yb_lat, gm, w, l):
    t = PROJ_TILE
    tok = lambda width: pl.BlockSpec((t, width), lambda i: (i, 0))
    weights = [w[n] for n in ('ssm_d', 'w_glu', 'gmlp_v_norm', 'gmlp_w_s', 'gmlp_bias', 'w_out_norm', 'w_out')]
    xs, off = _split_args(x, t)
    return pl.pallas_call(
        _merge_kernel,
        grid=(N_TOK // t,),
        in_specs=[*_split_specs(t, D_MODEL, off),
                  pl.BlockSpec((None, 1, N_MOD, D_MODEL), lambda i: (l, _mod_row(i, t), 0, 0)),
                  pl.BlockSpec((N_DIR, t, A_WIDTH), lambda i: (0, i, 0)),
                  tok(A_WIDTH), *_split_specs(t, B_WIDTH, 0), tok(2 * C_WIDTH)]
                 + [_layer_spec(a, l) for a in weights],
        out_specs=[tok(D_MODEL), tok(D_MODEL)],
        out_shape=[jax.ShapeDtypeStruct((N_TOK, D_MODEL), F32),
                   jax.ShapeDtypeStruct((N_TOK, D_MODEL), BF16)],
        compiler_params=pltpu.CompilerParams(vmem_limit_bytes=VMEM_LIMIT),
        name="merge",
    )(*xs, mod, yssm, u, yb_ctx, yb_lat, gm, *weights)


def _ffn_kernel(x1_ref, mod_ref, hp_ref, hm_ref, hn_ref, wup_ref, cw_ref, cb_ref, wd_ref, *rest):
    out_refs, a_scr = rest[:-1], rest[-1]
    seg =pl.program_id(0) - N_PROMPT // FFN_ROWS
    per_seq = DEC_SEQ // FFN_ROWS
    starts = (seg < 0) | ((seg & (per_seq - 1)) == 0)
    ends = (seg < 0) | ((seg & (per_seq - 1)) == per_seq - 1)
    h_prev = hp_ref[...]
    h_next = hn_ref[...]
    h_prev = jnp.where(starts, jnp.zeros_like(h_prev), h_prev)
    h_next = jnp.where(ends, jnp.zeros_like(h_next), h_next)
    he = jnp.concatenate([h_prev, hm_ref[...], h_next], axis=0)
    r0 = FFN_HALO

    def conv(u, c0):
        cols = slice(c0, c0 + FFN_FC)
        return (u[r0 - 1:r0 - 1 + FFN_ROWS] * cw_ref[0:1, cols] + u[r0:r0 + FFN_ROWS] * cw_ref[1:2, cols]
                + u[r0 + 1:r0 + 1 + FFN_ROWS] * cw_ref[2:3, cols] + cb_ref[:, cols])

    for j in range(D_FF // FFN_FC):
        cg = j * FFN_FC
        cv = D_FF + j * FFN_FC
        gate = conv(_dot(he, wup_ref[:, cg:cg + FFN_FC]), cg)
        val = conv(_dot(he, wup_ref[:, cv:cv + FFN_FC]), cv)
        a_scr[:, cg:cg + FFN_FC] = (gate * _sigmoid(gate) * val).astype(BF16)

    out = x1_ref[...] + mod_ref[0, 5:6, :] * _dot(a_scr[...], wd_ref[...])
    if len(out_refs) == 1:
        out_refs[0][...] = out
    else:
        @pl.when(seg < 0)
        def _():
            out_refs[0][...] = out

        @pl.when(seg >= 0)
        def _():
            out_refs[1][...] = out


def _ffn(x1, mod, h2, w, l, split_out):
    t = FFN_ROWS
    hb = t // FFN_HALO
    last_halo = N_TOK // FFN_HALO - 1
    weights = [w[n] for n in ('ffn_w_up', 'ffn_conv_w', 'ffn_conv_b', 'ffn_w_down')]
    if split_out:
        out_specs = list(_split_specs(t, D_MODEL, 0))
        out_shape = [jax.ShapeDtypeStruct((N_PROMPT, D_MODEL), F32), jax.ShapeDtypeStruct((N_SAMPLE, D_MODEL), F32)]
    else:
        out_specs = pl.BlockSpec((t, D_MODEL), lambda i: (i, 0))
        out_shape = jax.ShapeDtypeStruct((N_TOK, D_MODEL), F32)
    return pl.pallas_call(
        _ffn_kernel,
        grid=(N_TOK // t,),
        in_specs=[pl.BlockSpec((t, D_MODEL), lambda i: (i, 0)),
                  pl.BlockSpec((None, 1, N_MOD, D_MODEL), lambda i: (l, _mod_row(i, t), 0, 0)),
                  pl.BlockSpec((FFN_HALO, D_MODEL), lambda i: (jnp.maximum(i * hb - 1, 0), 0)),
                  pl.BlockSpec((t, D_MODEL), lambda i: (i, 0)),
                  pl.BlockSpec((FFN_HALO, D_MODEL), lambda i: (jnp.minimum((i + 1) * hb, last_halo), 0))]
                 + [_layer_spec(a, l, pipeline_mode=pl.Buffered(1)) for a in weights],
        out_specs=out_specs,
        out_shape=out_shape,
        scratch_shapes=[pltpu.VMEM((t, D_FF), BF16)],
        compiler_params=pltpu.CompilerParams(vmem_limit_bytes=VMEM_LIMIT,
                                             dimension_semantics=("arbitrary",)),
        name="conv_ffn",
    )(x1, mod, h2, h2, h2, *weights)


def _pad_heads(w, head_w, used, offset=0):
    lead = w.shape[:-1]
    w = w.reshape(lead + (MLA_HEADS, head_w))[..., :used]
    pad = [(0, 0)] * (len(lead) + 1) + [(offset, HEAD_PAD - used - offset)]
    return jnp.pad(w, pad).reshape(lead + (MLA_HEADS * HEAD_PAD,))


def _head_gain(g):
    return jnp.pad(g, ((0, 0), (0, HEAD_PAD - MLA_QK)))[:, None, :]


def _ssm_tables(a_re, a_im, b_re, b_im, c_re, c_im, log_dt):
    a = lax.complex(a_re, a_im)
    dt = jnp.exp(log_dt)[..., None]
    a_bar = jnp.exp(a * dt)
    b_bar = ((a_bar - 1.0) / a)[..., None] * lax.complex(b_re, b_im)
    steps = jnp.arange(1, SSM_STEPS + 1, dtype=F32)[None, None, :, None, None]
    pw = jnp.exp((a * dt)[:, :, None] * steps)
    eye = jnp.eye(SSM_G, dtype=F32)
    blk_b = lambda m: jnp.einsum('ldgpc,gh->ldgchp', m, eye).reshape(DEPTH, N_DIR, A_WIDTH, SSM_STATE)
    blk_c = lambda m: jnp.einsum('ldgcp,gh->ldgphc', m, eye).reshape(DEPTH, N_DIR, SSM_STATE, A_WIDTH)
    flat = lambda m: m.reshape(DEPTH, N_DIR, 1, SSM_STATE)
    return {
        'a_r': flat(a_bar.real), 'a_i': flat(a_bar.imag),
        'pw_r': pw.real.reshape(DEPTH, N_DIR, SSM_STEPS, SSM_STATE),
        'pw_i': pw.imag.reshape(DEPTH, N_DIR, SSM_STEPS, SSM_STATE),
        'b_blk': jnp.concatenate([blk_b(b_bar.real), blk_b(b_bar.imag)], axis=-1).astype(BF16),
        'c_r': blk_c(c_re).astype(BF16),
        'c_i': blk_c(-c_im).astype(BF16),
    }


def _rope_tables():
    rows = DEC_SEQ // GRID_W
    row = np.repeat(np.arange(rows, dtype=np.float32), GRID_W)
    col = np.tile(np.arange(GRID_W, dtype=np.float32), rows)
    n_freq = MLA_ROPE // 4
    inv = (np.float32(ROPE_BASE) ** (-np.arange(n_freq, dtype=np.float32) / np.float32(n_freq))).astype(np.float32)
    ang = np.concatenate([row[:, None] * inv, col[:, None] * inv], axis=-1).astype(np.float32)
    cos, sin = np.cos(ang), np.sin(ang)
    cos_t = np.ones((DEC_SEQ, HEAD_PAD), np.float32)
    sin_t = np.zeros((DEC_SEQ, HEAD_PAD), np.float32)
    cos_t[:, ROPE_LANE0:ROPE_LANE0 + MLA_ROPE] = np.concatenate([cos, cos], axis=-1)
    sin_t[:, ROPE_LANE0:ROPE_LANE0 + MLA_ROPE] = np.concatenate([sin, sin], axis=-1)
    return jnp.asarray(cos_t), jnp.asarray(sin_t)


def _rope_partner(r):
    half = MLA_ROPE // 2
    return jnp.concatenate([-r[..., half:], r[..., :half]], axis=-1)


def _rope_lanes(r):
    return jnp.pad(r, [(0, 0)] * (r.ndim - 1) + [(ROPE_LANE0, HEAD_PAD - ROPE_LANE0 - MLA_ROPE)])


def _head_gain_partner(g):
    half = MLA_ROPE // 2
    r = g[:, MLA_NOPE:]
    return _rope_lanes(jnp.concatenate([r[:, half:], r[:, :half]], axis=-1))[:, None, :]


def _prepare_weights(w_in, ssm_d, ssm_w_glu, q_a_norm, kv_a_norm, w_uq, w_ukv, q_norm, k_norm,
                     gmlp_v_norm, gmlp_w_s, gmlp_b_s, w_out_norm, w_out, ffn_w_up, ffn_conv_w,
                     ffn_conv_b, ffn_w_down):
    w_kr = w_in[:, :, OFF_KR:OFF_GM]
    ukv = w_ukv.reshape(DEPTH, KV_RANK, MLA_HEADS, MLA_NOPE + MLA_V)
    uq = w_uq.reshape(DEPTH, Q_RANK, MLA_HEADS, MLA_QK)
    q_gain = q_norm * Q_SCALE
    row = lambda a: a[:, None, :]
    return {
        'w_in_a': jnp.concatenate([w_in[:, :, OFF_SSM:OFF_KR], w_in[:, :, OFF_GM:]], axis=2).astype(BF16),
        'w_in_kr': _rope_lanes(w_kr).astype(BF16),
        'w_in_krp': _rope_lanes(_rope_partner(w_kr)).astype(BF16),
        'w_uqp': _rope_lanes(_rope_partner(uq[..., MLA_NOPE:])).reshape(DEPTH, Q_RANK, -1).astype(BF16),
        'q_a_norm': row(q_a_norm), 'kv_a_norm': row(kv_a_norm),
        'w_uq': _pad_heads(w_uq, MLA_QK, MLA_QK).astype(BF16),
        'w_uk': _pad_heads(ukv[..., :MLA_NOPE].reshape(DEPTH, KV_RANK, -1), MLA_NOPE, MLA_NOPE).astype(BF16),
        'w_uv': _pad_heads(ukv[..., MLA_NOPE:].reshape(DEPTH, KV_RANK, -1), MLA_V, MLA_V).astype(BF16),
        'q_norm': _head_gain(q_gain), 'q_norm_p': _head_gain_partner(q_gain),
        'k_norm': _head_gain(k_norm), 'k_norm_p': _head_gain_partner(k_norm),
        'ssm_d': row(ssm_d), 'w_glu': ssm_w_glu.astype(BF16),
        'gmlp_v_norm': row(gmlp_v_norm), 'gmlp_w_s': gmlp_w_s.astype(BF16),
        'gmlp_bias': jnp.repeat(jnp.swapaxes(gmlp_b_s, 1, 2), GMLP_CH, axis=2),
        'w_out_norm': row(w_out_norm), 'w_out': w_out.astype(BF16),
        'ffn_w_up': ffn_w_up.astype(BF16), 'ffn_conv_w': ffn_conv_w,
        'ffn_conv_b': row(ffn_conv_b), 'ffn_w_down': ffn_w_down.astype(BF16),
    }


def kernel(x_prompt, x_sample, cache_ckv, cache_krope, state_ssm_re, state_ssm_im, c, c_ctx, w_mod, b_mod, w_in, ssm_a_re, ssm_a_im, ssm_b_re, ssm_b_im, ssm_c_re, ssm_c_im, ssm_log_dt, ssm_d, ssm_w_glu, q_a_norm, kv_a_norm, w_uq, w_ukv, q_norm, k_norm, gmlp_v_norm, gmlp_w_s, gmlp_b_s, w_out_norm, w_out, ffn_w_up, ffn_conv_w, ffn_conv_b, ffn_w_down):
    cond8 = jnp.concatenate([c_ctx[None, :], c, jnp.zeros((8 - 1 - DEC_BATCH, D_MODEL), F32)], axis=0)
    mod = _modulation(cond8, w_mod, b_mod).reshape(DEPTH, 8, N_MOD, D_MODEL)
    rope_tabs = _rope_tables()
    w = _prepare_weights(w_in, ssm_d, ssm_w_glu, q_a_norm, kv_a_norm, w_uq, w_ukv, q_norm, k_norm,
                         gmlp_v_norm, gmlp_w_s, gmlp_b_s, w_out_norm, w_out, ffn_w_up, ffn_conv_w,
                         ffn_conv_b, ffn_w_down)
    tabs = _ssm_tables(ssm_a_re, ssm_a_im, ssm_b_re, ssm_b_im, ssm_c_re, ssm_c_im, ssm_log_dt)
    cache_kr = _rope_lanes(cache_krope)

    def h0(st):
        st = jnp.transpose(st.reshape(DEC_BATCH, DEPTH, N_DIR, SSM_STATE), (1, 2, 0, 3))
        return jnp.pad(st, ((0, 0), (0, 0), (0, SSM_STREAMS - DEC_BATCH), (0, 0)))

    h0r, h0i = h0(state_ssm_re), h0(state_ssm_im)

    x = (x_prompt.reshape(N_PROMPT, D_MODEL), x_sample.reshape(N_SAMPLE, D_MODEL))
    ckv_out, kr_out, hre_out, him_out = [], [], [], []
    for l in range(DEPTH):
        u, gm, ckv, kr, q, k, v = _projection(x, mod, rope_tabs, w, l)
        k_c, v_c = _ctx_kv(cache_ckv, cache_kr, w, l)
        yb_ctx = _attention(q, [k], [v], BATCH, SEQ, 0, [SEQ], [0])
        yssm, hf_r, hf_i = _ssm(u, h0r, h0i, tabs, l)
        yb_lat = _attention(q, [k_c, k], [v_c, v], DEC_BATCH, DEC_SEQ, N_PROMPT,
                            [PAST_LEN, DEC_SEQ], [0, N_PROMPT])
        x1, h2 = _merge(x, mod, yssm, u, yb_ctx, yb_lat, gm, w, l)
        x = _ffn(x1, mod, h2, w, l, split_out=(l == DEPTH - 1))

        ckv_out.append(ckv.reshape(BATCH, SEQ, KV_RANK))
        kr_out.append(kr[:, ROPE_LANE0:ROPE_LANE0 + MLA_ROPE].reshape(BATCH, SEQ, MLA_ROPE))
        to_state = lambda st: jnp.transpose(st, (1, 0, 2)).reshape(BATCH, N_DIR, SSM_G, SSM_P)
        hre_out.append(to_state(hf_r))
        him_out.append(to_state(hf_i))

    y_prompt = x[0].reshape(BATCH, SEQ, D_MODEL)
    y_sample = x[1].reshape(DEC_BATCH, DEC_SEQ, D_MODEL)
    return (y_prompt, y_sample, jnp.stack(ckv_out, axis=1), jnp.stack(kr_out, axis=1),
            jnp.stack(hre_out, axis=1), jnp.stack(him_out, axis=1))
```

```python
import functools
import math

import jax
import jax.numpy as jnp
import numpy as np
from jax import lax
from jax.experimental import pallas as pl
from jax.experimental.pallas import tpu as pltpu

F32 = jnp.float32
BF16 = jnp.bfloat16

D_MODEL = 1024
BATCH = 16
SEQ = 256
DEPTH = 2
DEC_BATCH = 2
DEC_SEQ = 4096
PAST_LEN = 256
GRID_W = 64
EPS = 1e-6
N_MOD = 6
A_WIDTH = 256
B_WIDTH = 512
C_WIDTH = 256
SSM_CG = 16
SSM_G = 16
SSM_P = 64
N_DIR = 2
MLA_HEADS = 8
MLA_NOPE = 64
MLA_ROPE = 32
MLA_V = 64
MLA_QK = 96
Q_RANK = 256
KV_RANK = 128
ROPE_BASE = 10000.0
ATTN_SCALE = 1.0 / math.sqrt(MLA_QK)
Q_SCALE = ATTN_SCALE * math.log2(math.e)
GMLP_HEADS = 4
GMLP_CH = 64
CHUNK = 128
OFF_SSM = 0
OFF_Q = OFF_SSM + A_WIDTH
OFF_KV = OFF_Q + Q_RANK
OFF_KR = OFF_KV + KV_RANK
OFF_GM = OFF_KR + MLA_ROPE
D_FF = 2816

N_PROMPT = BATCH * SEQ
N_SAMPLE = DEC_BATCH * DEC_SEQ
N_TOK = N_PROMPT + N_SAMPLE
SSM_STATE = SSM_G * SSM_P

LANES = 128
HEAD_PAD = LANES
ROPE_LANE0 = MLA_NOPE

PROJ_TILE = 512
PROJ_SUB = 256
SSM_STREAMS = 8
SSM_STEPS = 256
SSM_UNIT = SSM_STREAMS * SSM_STEPS
ATTN_TQ = 256
ATTN_KC = 1024
ATTN_HEADS = 4
FFN_ROWS = SEQ
FFN_FC = 256
FFN_HALO = 16
VMEM_LIMIT = 48 * 1024 * 1024


def _mod_row(tile, tile_tokens):
    start = tile * tile_tokens
    return jnp.where(start < N_PROMPT, 0, 1 + (start - N_PROMPT) // DEC_SEQ)


def _layer_spec(a, l, **kw):
    return pl.BlockSpec((None,) + a.shape[1:], lambda *_: (l,) + (0,) * (a.ndim - 1), **kw)


def _split_specs(t, width, second_offset):
    n_a = N_PROMPT // t
    return (pl.BlockSpec((t, width), lambda i: (jnp.minimum(i, n_a - 1), 0)),
            pl.BlockSpec((t, width), lambda i: (jnp.maximum(i - n_a, 0) + second_offset, 0)))


def _split_args(x, t):
    if isinstance(x, tuple):
        return x, 0
    return (x, x), N_PROMPT // t


def _rms(x):
    return x * lax.rsqrt(jnp.mean(x * x, axis=-1, keepdims=True) + EPS)


def _sigmoid(x):
    return 1.0 / (1.0 + jnp.exp(-x))


def _gelu_tanh(x):
    return 0.5 * x * (1.0 + jnp.tanh(math.sqrt(2.0 / math.pi) * (x + 0.044715 * (x * x * x))))


def _dot(a, b):
    return jnp.dot(a, b, preferred_element_type=F32)


def _mod_kernel(cond_ref, w_ref, b_ref, o_ref):
    cond = cond_ref[...]
    s = (cond * _sigmoid(cond)).astype(BF16)
    o_ref[0] = _dot(s, w_ref[0].astype(BF16)) + b_ref[0]


def _modulation(cond8, w_mod, b_mod):
    tn = 1536
    n_cols = N_MOD * D_MODEL
    return pl.pallas_call(
        _mod_kernel,
        grid=(DEPTH, n_cols // tn),
        in_specs=[
            pl.BlockSpec((8, D_MODEL), lambda l, j: (0, 0)),
            pl.BlockSpec((1, D_MODEL, tn), lambda l, j: (l, 0, j)),
            pl.BlockSpec((1, 1, tn), lambda l, j: (l, 0, j)),
        ],
        out_specs=pl.BlockSpec((1, 8, tn), lambda l, j: (l, 0, j)),
        out_shape=jax.ShapeDtypeStruct((DEPTH, 8, n_cols), F32),
        compiler_params=pltpu.CompilerParams(vmem_limit_bytes=VMEM_LIMIT),
        name="modulation",
    )(cond8, w_mod, b_mod.reshape(DEPTH, 1, n_cols))


def _write_heads(allh, extra, gain, partner, out_ref, rows=slice(None)):
    for h in range(MLA_HEADS):
        tile = slice(h * HEAD_PAD, (h + 1) * HEAD_PAD)
        t = allh[:, tile]
        if extra is not None:
            t = t + extra
        rs = lax.rsqrt(jnp.sum(t * t, axis=-1, keepdims=True) * (1.0 / MLA_QK) + EPS)
        o = t * gain
        if isinstance(partner, tuple):
            o = o + partner[0][:, tile] * partner[1]
        elif partner is not None:
            o = o + partner
        out_ref[h, rows, :] = (o * rs).astype(BF16)


def _write_values(vall, v_ref, rows=slice(None)):
    lane = lax.broadcasted_iota(jnp.int32, (1, HEAD_PAD), 1)
    ones = jnp.where(lane >= MLA_V, 1.0, 0.0)
    for h in range(MLA_HEADS):
        v_ref[h, rows, :] = (vall[:, h * HEAD_PAD:(h + 1) * HEAD_PAD] + ones).astype(BF16)


def _proj_kernel(xp_ref, xs_ref, mod_ref, cos_ref, sin_ref, w_in_ref, w_kr_ref, w_krp_ref, qan_ref, kvn_ref,
                 w_uq_ref, w_uqp_ref, w_uk_ref, w_uv_ref, qn_ref, qnp_ref, kn_ref, knp_ref,
                 u_ref, gm_ref, ckv_ref, kr_ref, q_ref, k_ref, v_ref):
    latent = pl.program_id(0) >= N_PROMPT // PROJ_TILE
    shift = mod_ref[0, 0:1, :]
    scale = mod_ref[0, 1:2, :]

    def body(x_ref, rope):
        for sub in range(PROJ_TILE // PROJ_SUB):
            rows = slice(sub * PROJ_SUB, (sub + 1) * PROJ_SUB)
            h = (_rms(x_ref[rows, :]) * (1.0 + scale) + shift).astype(BF16)
            z = _dot(h, w_in_ref[...])
            kr = _dot(h, w_kr_ref[...])
            u_ref[rows, :] = z[:, 0:A_WIDTH]
            gm_ref[rows, :] = z[:, A_WIDTH + Q_RANK + KV_RANK:].astype(gm_ref.dtype)
            ckv = _rms(z[:, A_WIDTH + Q_RANK:A_WIDTH + Q_RANK + KV_RANK]) * kvn_ref[...]
            cq = (_rms(z[:, A_WIDTH:A_WIDTH + Q_RANK]) * qan_ref[...]).astype(BF16)
            ckv_b = ckv.astype(BF16)
            qall = _dot(cq, w_uq_ref[...])
            kall = _dot(ckv_b, w_uk_ref[...])
            _write_values(_dot(ckv_b, w_uv_ref[...]), v_ref, rows)
            if rope:
                cos = cos_ref[rows, :]
                sin = sin_ref[rows, :]
                _write_heads(qall, None, qn_ref[...] * cos, (_dot(cq, w_uqp_ref[...]), qnp_ref[...] * sin),
                             q_ref, rows)
                _write_heads(kall, kr, kn_ref[...] * cos, _dot(h, w_krp_ref[...]) * (knp_ref[...] * sin),
                             k_ref, rows)
            else:
                ckv_ref[rows, :] = ckv
                kr_ref[rows, :] = kr
                _write_heads(qall, None, qn_ref[...], None, q_ref, rows)
                _write_heads(kall, kr, kn_ref[...], None, k_ref, rows)

    pl.when(latent)(lambda: body(xs_ref, True))
    pl.when(jnp.logical_not(latent))(lambda: body(xp_ref, False))


def _projection(x, mod, rope_tabs, w, l):
    t = PROJ_TILE
    n_tiles = N_TOK // t
    n_ctx = N_PROMPT // t
    tok = lambda width: pl.BlockSpec((t, width), lambda i: (i, 0))
    ctx = lambda width: pl.BlockSpec((t, width), lambda i: (jnp.minimum(i, n_ctx - 1), 0))
    heads = pl.BlockSpec((MLA_HEADS, t, HEAD_PAD), lambda i: (0, i, 0))
    pos = pl.BlockSpec((t, HEAD_PAD), lambda i: (jnp.maximum(i - n_ctx, 0) % (DEC_SEQ // t), 0))
    weights = [w[n] for n in ('w_in_a', 'w_in_kr', 'w_in_krp', 'q_a_norm', 'kv_a_norm', 'w_uq', 'w_uqp',
                              'w_uk', 'w_uv', 'q_norm', 'q_norm_p', 'k_norm', 'k_norm_p')]
    xs, off = _split_args(x, t)
    head_shape = jax.ShapeDtypeStruct((MLA_HEADS, N_TOK, HEAD_PAD), BF16)
    return pl.pallas_call(
        _proj_kernel,
        grid=(n_tiles,),
        in_specs=[*_split_specs(t, D_MODEL, off),
                  pl.BlockSpec((None, 1, N_MOD, D_MODEL), lambda i: (l, _mod_row(i, t), 0, 0)),
                  pos, pos] + [_layer_spec(a, l) for a in weights],
        out_specs=[tok(A_WIDTH), tok(2 * C_WIDTH), ctx(KV_RANK), ctx(HEAD_PAD), heads, heads, heads],
        out_shape=[
            jax.ShapeDtypeStruct((N_TOK, A_WIDTH), F32),
            jax.ShapeDtypeStruct((N_TOK, 2 * C_WIDTH), BF16),
            jax.ShapeDtypeStruct((N_PROMPT, KV_RANK), F32),
            jax.ShapeDtypeStruct((N_PROMPT, HEAD_PAD), F32),
            head_shape, head_shape, head_shape,
        ],
        compiler_params=pltpu.CompilerParams(vmem_limit_bytes=VMEM_LIMIT),
        name="projection",
    )(*xs, mod, *rope_tabs, *weights)


def _ctx_kv_kernel(ckv_ref, kr_ref, w_uk_ref, w_uv_ref, kn_ref, k_ref, v_ref):
    n = DEC_BATCH * PAST_LEN
    ckv_b = ckv_ref[...].reshape(n, KV_RANK).astype(BF16)
    _write_heads(_dot(ckv_b, w_uk_ref[...]), kr_ref[...].reshape(n, HEAD_PAD), kn_ref[...], None, k_ref)
    _write_values(_dot(ckv_b, w_uv_ref[...]), v_ref)


def _ctx_kv(cache_ckv, cache_kr, w, l):
    n = DEC_BATCH * PAST_LEN
    cache = pl.BlockSpec((DEC_BATCH, None, PAST_LEN, HEAD_PAD), lambda i: (0, l, 0, 0))
    weights = [w['w_uk'], w['w_uv'], w['k_norm']]
    heads = pl.BlockSpec((MLA_HEADS, n, HEAD_PAD), lambda i: (0, 0, 0))
    head_shape = jax.ShapeDtypeStruct((MLA_HEADS, n, HEAD_PAD), BF16)
    return pl.pallas_call(
        _ctx_kv_kernel,
        grid=(1,),
        in_specs=[cache, cache] + [_layer_spec(a, l) for a in weights],
        out_specs=[heads, heads],
        out_shape=[head_shape, head_shape],
        compiler_params=pltpu.CompilerParams(vmem_limit_bytes=VMEM_LIMIT),
        name="context_kv",
    )(cache_ckv, cache_kr, *weights)


def _ssm_block(d, k):
    return jnp.where((d == 1) & (k >= 2), k ^ 1, k)


def _ssm_kernel(u_ref, h0r_ref, h0i_ref, ar_ref, ai_ref, pwr_ref, pwi_ref, bm_ref, cr_ref, ci_ref,
                y_ref, hfr_ref, hfi_ref, sre, sim, hin_r, hin_i, car_r, car_i):
    d = pl.program_id(0)
    k = pl.program_id(1)
    rows = 256

    @pl.when(k == 0)
    def _():
        car_r[...] = jnp.zeros_like(car_r)
        car_i[...] = jnp.zeros_like(car_i)

    for c in range(SSM_UNIT // rows):
        sl = slice(c * rows, (c + 1) * rows)
        bu = _dot(u_ref[sl, :].astype(BF16), bm_ref[0])
        sre[sl, :] = bu[:, :SSM_STATE]
        sim[sl, :] = bu[:, SSM_STATE:]

    a_r = jnp.broadcast_to(ar_ref[0], (SSM_STREAMS, SSM_STATE))
    a_i = jnp.broadcast_to(ai_ref[0], (SSM_STREAMS, SSM_STATE))

    def stream_rows(tau):
        return pl.ds(pl.multiple_of(tau * SSM_STREAMS, SSM_STREAMS), SSM_STREAMS)

    def scan_step(i, carry):
        h_r, h_i = carry
        tau = jnp.where(d == 0, i, SSM_STEPS - 1 - i)
        n_r = a_r * h_r - a_i * h_i + sre[stream_rows(tau), :]
        n_i = a_r * h_i + a_i * h_r + sim[stream_rows(tau), :]
        sre[stream_rows(tau), :] = n_r
        sim[stream_rows(tau), :] = n_i
        return n_r, n_i

    zero = jnp.zeros((SSM_STREAMS, SSM_STATE), F32)
    end_r, end_i = lax.fori_loop(0, SSM_STEPS, scan_step, (zero, zero), unroll=2)

    @pl.when(k < 2)
    def _():
        hfr_ref[0] = end_r
        hfi_ref[0] = end_i

    @pl.when(k >= 2)
    def _():
        b = (k - 2) // 2
        first = (k % 2) == 0
        c_r = jnp.where(first, h0r_ref[0, pl.ds(b, 1), :], car_r[...])
        c_i = jnp.where(first, h0i_ref[0, pl.ds(b, 1), :], car_i[...])
        full_r = pwr_ref[0, SSM_STEPS - 1:SSM_STEPS, :]
        full_i = pwi_ref[0, SSM_STEPS - 1:SSM_STEPS, :]

        def chain(order):
            cr, ci = c_r, c_i
            for j in order:
                hin_r[j:j + 1, :] = cr
                hin_i[j:j + 1, :] = ci
                nr = end_r[j:j + 1, :] + full_r * cr - full_i * ci
                ni = end_i[j:j + 1, :] + full_r * ci + full_i * cr
                cr, ci = nr, ni
            car_r[...] = cr
            car_i[...] = ci

        @pl.when(d == 0)
        def _():
            chain(range(SSM_STREAMS))

        @pl.when(d == 1)
        def _():
            chain(range(SSM_STREAMS - 1, -1, -1))

        g_r = hin_r[...]
        g_i = hin_i[...]

        def fix_step(i, carry):
            p = jnp.where(d == 0, i, SSM_STEPS - 1 - i)
            p_r = jnp.broadcast_to(pwr_ref[0, pl.ds(p, 1), :], (SSM_STREAMS, SSM_STATE))
            p_i = jnp.broadcast_to(pwi_ref[0, pl.ds(p, 1), :], (SSM_STREAMS, SSM_STATE))
            sr = stream_rows(i)
            sre[sr, :] = sre[sr, :] + (p_r * g_r - p_i * g_i)
            sim[sr, :] = sim[sr, :] + (p_r * g_i + p_i * g_r)
            return carry

        lax.fori_loop(0, SSM_STEPS, fix_step, 0, unroll=2)

    for c in range(SSM_UNIT // rows):
        sl = slice(c * rows, (c + 1) * rows)
        y_ref[0, sl, :] = (_dot(sre[sl, :].astype(BF16), cr_ref[0])
                           + _dot(sim[sl, :].astype(BF16), ci_ref[0])).astype(y_ref.dtype)


def _step_major(x, inverse=False):
    lead = x.shape[:-2]
    a, b = (SSM_STEPS, SSM_STREAMS) if inverse else (SSM_STREAMS, SSM_STEPS)
    x = x.reshape(lead + (N_TOK // SSM_UNIT, a, b, x.shape[-1]))
    return jnp.swapaxes(x, -3, -2).reshape(lead + (N_TOK, x.shape[-1]))


def _ssm(u, h0r, h0i, tabs, l):
    n_units = N_TOK // SSM_UNIT
    u = _step_major(u)
    per_dir = lambda a: pl.BlockSpec((None, 1) + a.shape[2:], lambda d, k: (l, d) + (0,) * (a.ndim - 2))
    operands = [h0r, h0i] + [tabs[n] for n in ('a_r', 'a_i', 'pw_r', 'pw_i', 'b_blk', 'c_r', 'c_i')]
    y, hf_r, hf_i = pl.pallas_call(
        _ssm_kernel,
        grid=(N_DIR, n_units),
        in_specs=[pl.BlockSpec((SSM_UNIT, A_WIDTH), lambda d, k: (_ssm_block(d, k), 0))]
                 + [per_dir(a) for a in operands],
        out_specs=[pl.BlockSpec((1, SSM_UNIT, A_WIDTH), lambda d, k: (d, _ssm_block(d, k), 0)),
                   pl.BlockSpec((1, SSM_STREAMS, SSM_STATE), lambda d, k: (d, jnp.minimum(k, 1), 0)),
                   pl.BlockSpec((1, SSM_STREAMS, SSM_STATE), lambda d, k: (d, jnp.minimum(k, 1), 0))],
        out_shape=[jax.ShapeDtypeStruct((N_DIR, N_TOK, A_WIDTH), BF16),
                   jax.ShapeDtypeStruct((N_DIR, BATCH, SSM_STATE), F32),
                   jax.ShapeDtypeStruct((N_DIR, BATCH, SSM_STATE), F32)],
        scratch_shapes=[pltpu.VMEM((SSM_UNIT, SSM_STATE), F32), pltpu.VMEM((SSM_UNIT, SSM_STATE), F32),
                        pltpu.VMEM((SSM_STREAMS, SSM_STATE), F32), pltpu.VMEM((SSM_STREAMS, SSM_STATE), F32),
                        pltpu.VMEM((1, SSM_STATE), F32), pltpu.VMEM((1, SSM_STATE), F32)],
        compiler_params=pltpu.CompilerParams(vmem_limit_bytes=VMEM_LIMIT,
                                             dimension_semantics=("arbitrary", "arbitrary")),
        name="ssm_scan",
    )(u, *operands)
    return _step_major(y, inverse=True), hf_r, hf_i


def _attn_kernel(*refs, key_lens):
    n_src = len(key_lens)
    q_ref = refs[0]
    k_refs = refs[1:1 + n_src]
    v_refs = refs[1 + n_src:1 + 2 * n_src]
    o_ref = refs[1 + 2 * n_src]
    s_ref = refs[2 + 2 * n_src]
    chunks = []
    off = 0
    for si, n in enumerate(key_lens):
        for c0 in range(0, n, ATTN_KC):
            cl = min(ATTN_KC, n - c0)
            chunks.append((si, c0, cl, off))
            off += cl
    tq = q_ref.shape[1]
    row_max = []
    n_heads = q_ref.shape[0]
    for hh in range(n_heads):
        q = q_ref[hh]
        mp = jnp.full((tq, LANES), -jnp.inf, F32)
        for si, c0, cl, o in chunks:
            s = lax.dot_general(q, k_refs[si][hh, c0:c0 + cl, :], (((1,), (1,)), ((), ())),
                                preferred_element_type=F32)
            s_ref[hh, :, o:o + cl] = s
            for g in range(cl // LANES):
                mp = jnp.maximum(mp, s[:, g * LANES:(g + 1) * LANES])
        row_max.append(jnp.max(mp, axis=-1, keepdims=True))
    outs = []
    for hh in range(n_heads):
        acc = jnp.zeros((tq, LANES), F32)
        for si, c0, cl, o in chunks:
            p = jnp.exp2(s_ref[hh, :, o:o + cl] - row_max[hh])
            acc = acc + _dot(p.astype(BF16), v_refs[si][hh, c0:c0 + cl, :])
        outs.append(acc / pltpu.roll(acc, MLA_V, 1))
    lane = lax.broadcasted_iota(jnp.int32, (tq, LANES), 1)
    for pair in range(n_heads // 2):
        o_ref[:, pair * LANES:(pair + 1) * LANES] = jnp.where(
            lane < MLA_V, outs[2 * pair], pltpu.roll(outs[2 * pair + 1], MLA_V, 1)).astype(o_ref.dtype)


def _attention(q, ks, vs, n_seq, seq_len, tok0, key_lens, key_tok0):
    tq = min(ATTN_TQ, seq_len)
    n_qt = seq_len // tq
    q0 = tok0 // tq
    in_specs = [pl.BlockSpec((ATTN_HEADS, tq, HEAD_PAD), lambda b, hp, qi: (hp, q0 + b * n_qt + qi, 0))]
    for n, t0 in zip(key_lens, key_tok0):
        in_specs.append(pl.BlockSpec((ATTN_HEADS, n, HEAD_PAD), lambda b, hp, qi, n=n, t0=t0: (hp, t0 // n + b, 0)))
    for n, t0 in zip(key_lens, key_tok0):
        in_specs.append(pl.BlockSpec((ATTN_HEADS, n, HEAD_PAD), lambda b, hp, qi, n=n, t0=t0: (hp, t0 // n + b, 0)))
    return pl.pallas_call(
        functools.partial(_attn_kernel, key_lens=tuple(key_lens)),
        grid=(n_seq, MLA_HEADS // ATTN_HEADS, n_qt),
        in_specs=in_specs,
        out_specs=pl.BlockSpec((tq, ATTN_HEADS * MLA_V), lambda b, hp, qi: (b * n_qt + qi, hp)),
        out_shape=jax.ShapeDtypeStruct((n_seq * seq_len, B_WIDTH), BF16),
        scratch_shapes=[pltpu.VMEM((ATTN_HEADS, tq, sum(key_lens)), F32)],
        compiler_params=pltpu.CompilerParams(vmem_limit_bytes=VMEM_LIMIT,
                                             dimension_semantics=("arbitrary", "arbitrary", "arbitrary")),
        name="attention",
    )(q, *ks, *vs)


def _merge_kernel(xp_ref, xs_ref, mod_ref, yssm_ref, u_ref, ybp_ref, ybs_ref, gm_ref, dskip_ref, w_glu_ref,
                  vn_ref, w_s_ref, bias_ref, g_ref, w_out_ref, x1_ref, h2_ref):
    t = xp_ref.shape[0]
    latent = pl.program_id(0) >= N_PROMPT // t
    lane = lax.broadcasted_iota(jnp.int32, (CHUNK, C_WIDTH), 1)
    g = g_ref[...]

    def body(x_ref, yb_ref):
        for sub in range(t // PROJ_SUB):
            rows = slice(sub * PROJ_SUB, (sub + 1) * PROJ_SUB)
            y = (yssm_ref[0, rows, :].astype(F32) + yssm_ref[1, rows, :].astype(F32)
                 + dskip_ref[...] * u_ref[rows, :])
            y = _gelu_tanh(y)
            y_a = y * _sigmoid(_dot(y.astype(BF16), w_glu_ref[...]))
            gm = gm_ref[rows, :].astype(F32)
            vn = (_rms(gm[:, C_WIDTH:]) * vn_ref[...]).astype(BF16)
            mixed = []
            for c in range(PROJ_SUB // CHUNK):
                vc = vn[c * CHUNK:(c + 1) * CHUNK, :]
                m = _dot(w_s_ref[0], vc)
                for h in range(1, GMLP_HEADS):
                    m = jnp.where(lane >= h * GMLP_CH, _dot(w_s_ref[h], vc), m)
                mixed.append(m + bias_ref[...])
            y_c = gm[:, :C_WIDTH] * jnp.concatenate(mixed, axis=0)
            n_a = (_rms(y_a) * g[:, :A_WIDTH]).astype(BF16)
            n_b = (_rms(yb_ref[rows, :].astype(F32)) * g[:, A_WIDTH:A_WIDTH + B_WIDTH]).astype(BF16)
            n_c = (_rms(y_c) * g[:, A_WIDTH + B_WIDTH:]).astype(BF16)
            o = (_dot(n_a, w_out_ref[0:A_WIDTH, :]) + _dot(n_b, w_out_ref[A_WIDTH:A_WIDTH + B_WIDTH, :])
                 + _dot(n_c, w_out_ref[A_WIDTH + B_WIDTH:, :]))
            x1 = x_ref[rows, :] + mod_ref[0, 2:3, :] * o
            x1_ref[rows, :] = x1
            h2_ref[rows, :] = (_rms(x1) * (1.0 + mod_ref[0, 4:5, :]) + mod_ref[0, 3:4, :]).astype(BF16)

    pl.when(latent)(lambda: body(xs_ref, ybs_ref))
    pl.when(jnp.logical_not(latent))(lambda: body(xp_ref, ybp_ref))


def _merge(x, mod, yssm, u, yb_ctx, yb_lat, gm, w, l):
    t = PROJ_TILE
    tok = lambda width: pl.BlockSpec((t, width), lambda i: (i, 0))
    weights = [w[n] for n in ('ssm_d', 'w_glu', 'gmlp_v_norm', 'gmlp_w_s', 'gmlp_bias', 'w_out_norm', 'w_out')]
    xs, off = _split_args(x, t)
    return pl.pallas_call(
        _merge_kernel,
        grid=(N_TOK // t,),
        in_specs=[*_split_specs(t, D_MODEL, off),
                  pl.BlockSpec((None, 1, N_MOD, D_MODEL), lambda i: (l, _mod_row(i, t), 0, 0)),
                  pl.BlockSpec((N_DIR, t, A_WIDTH), lambda i: (0, i, 0)),
                  tok(A_WIDTH), *_split_specs(t, B_WIDTH, 0), tok(2 * C_WIDTH)]
                 + [_layer_spec(a, l) for a in weights],
        out_specs=[tok(D_MODEL), tok(D_MODEL)],
        out_shape=[jax.ShapeDtypeStruct((N_TOK, D_MODEL), F32),
                   jax.ShapeDtypeStruct((N_TOK, D_MODEL), BF16)],
        compiler_params=pltpu.CompilerParams(vmem_limit_bytes=VMEM_LIMIT),
        name="merge",
    )(*xs, mod, yssm, u, yb_ctx, yb_lat, gm, *weights)


def _ffn_kernel(x1_ref, mod_ref, hp_ref, hm_ref, hn_ref, wup_ref, cw_ref, cb_ref, wd_ref, *rest):
    out_refs, a_scr = rest[:-1], rest[-1]
    seg =pl.program_id(0) - N_PROMPT // FFN_ROWS
    per_seq = DEC_SEQ // FFN_ROWS
    starts = (seg < 0) | ((seg & (per_seq - 1)) == 0)
    ends = (seg < 0) | ((seg & (per_seq - 1)) == per_seq - 1)
    h_prev = hp_ref[...]
    h_next = hn_ref[...]
    h_prev = jnp.where(starts, jnp.zeros_like(h_prev), h_prev)
    h_next = jnp.where(ends, jnp.zeros_like(h_next), h_next)
    he = jnp.concatenate([h_prev, hm_ref[...], h_next], axis=0)
    r0 = FFN_HALO

    def conv(u, c0):
        cols = slice(c0, c0 + FFN_FC)
        return (u[r0 - 1:r0 - 1 + FFN_ROWS] * cw_ref[0:1, cols] + u[r0:r0 + FFN_ROWS] * cw_ref[1:2, cols]
                + u[r0 + 1:r0 + 1 + FFN_ROWS] * cw_ref[2:3, cols] + cb_ref[:, cols])

    for j in range(D_FF // FFN_FC):
        cg = j * FFN_FC
        cv = D_FF + j * FFN_FC
        gate = conv(_dot(he, wup_ref[:, cg:cg + FFN_FC]), cg)
        val = conv(_dot(he, wup_ref[:, cv:cv + FFN_FC]), cv)
        a_scr[:, cg:cg + FFN_FC] = (gate * _sigmoid(gate) * val).astype(BF16)

    out = x1_ref[...] + mod_ref[0, 5:6, :] * _dot(a_scr[...], wd_ref[...])
    if len(out_refs) == 1:
        out_refs[0][...] = out
    else:
        @pl.when(seg < 0)
        def _():
            out_refs[0][...] = out

        @pl.when(seg >= 0)
        def _():
            out_refs[1][...] = out


def _ffn(x1, mod, h2, w, l, split_out):
    t = FFN_ROWS
    hb = t // FFN_HALO
    last_halo = N_TOK // FFN_HALO - 1
    weights = [w[n] for n in ('ffn_w_up', 'ffn_conv_w', 'ffn_conv_b', 'ffn_w_down')]
    if split_out:
        out_specs = list(_split_specs(t, D_MODEL, 0))
        out_shape = [jax.ShapeDtypeStruct((N_PROMPT, D_MODEL), F32), jax.ShapeDtypeStruct((N_SAMPLE, D_MODEL), F32)]
    else:
        out_specs = pl.BlockSpec((t, D_MODEL), lambda i: (i, 0))
        out_shape = jax.ShapeDtypeStruct((N_TOK, D_MODEL), F32)
    return pl.pallas_call(
        _ffn_kernel,
        grid=(N_TOK // t,),
        in_specs=[pl.BlockSpec((t, D_MODEL), lambda i: (i, 0)),
                  pl.BlockSpec((None, 1, N_MOD, D_MODEL), lambda i: (l, _mod_row(i, t), 0, 0)),
                  pl.BlockSpec((FFN_HALO, D_MODEL), lambda i: (jnp.maximum(i * hb - 1, 0), 0)),
                  pl.BlockSpec((t, D_MODEL), lambda i: (i, 0)),
                  pl.BlockSpec((FFN_HALO, D_MODEL), lambda i: (jnp.minimum((i + 1) * hb, last_halo), 0))]
                 + [_layer_spec(a, l, pipeline_mode=pl.Buffered(1)) for a in weights],
        out_specs=out_specs,
        out_shape=out_shape,
        scratch_shapes=[pltpu.VMEM((t, D_FF), BF16)],
        compiler_params=pltpu.CompilerParams(vmem_limit_bytes=VMEM_LIMIT,
                                             dimension_semantics=("arbitrary",)),
        name="conv_ffn",
    )(x1, mod, h2, h2, h2, *weights)


def _pad_heads(w, head_w, used, offset=0):
    lead = w.shape[:-1]
    w = w.reshape(lead + (MLA_HEADS, head_w))[..., :used]
    pad = [(0, 0)] * (len(lead) + 1) + [(offset, HEAD_PAD - used - offset)]
    return jnp.pad(w, pad).reshape(lead + (MLA_HEADS * HEAD_PAD,))


def _head_gain(g):
    return jnp.pad(g, ((0, 0), (0, HEAD_PAD - MLA_QK)))[:, None, :]


def _ssm_tables(a_re, a_im, b_re, b_im, c_re, c_im, log_dt):
    a = lax.complex(a_re, a_im)
    dt = jnp.exp(log_dt)[..., None]
    a_bar = jnp.exp(a * dt)
    b_bar = ((a_bar - 1.0) / a)[..., None] * lax.complex(b_re, b_im)
    steps = jnp.arange(1, SSM_STEPS + 1, dtype=F32)[None, None, :, None, None]
    pw = jnp.exp((a * dt)[:, :, None] * steps)
    eye = jnp.eye(SSM_G, dtype=F32)
    blk_b = lambda m: jnp.einsum('ldgpc,gh->ldgchp', m, eye).reshape(DEPTH, N_DIR, A_WIDTH, SSM_STATE)
    blk_c = lambda m: jnp.einsum('ldgcp,gh->ldgphc', m, eye).reshape(DEPTH, N_DIR, SSM_STATE, A_WIDTH)
    flat = lambda m: m.reshape(DEPTH, N_DIR, 1, SSM_STATE)
    return {
        'a_r': flat(a_bar.real), 'a_i': flat(a_bar.imag),
        'pw_r': pw.real.reshape(DEPTH, N_DIR, SSM_STEPS, SSM_STATE),
        'pw_i': pw.imag.reshape(DEPTH, N_DIR, SSM_STEPS, SSM_STATE),
        'b_blk': jnp.concatenate([blk_b(b_bar.real), blk_b(b_bar.imag)], axis=-1).astype(BF16),
        'c_r': blk_c(c_re).astype(BF16),
        'c_i': blk_c(-c_im).astype(BF16),
    }


def _rope_tables():
    rows = DEC_SEQ // GRID_W
    row = np.repeat(np.arange(rows, dtype=np.float32), GRID_W)
    col = np.tile(np.arange(GRID_W, dtype=np.float32), rows)
    n_freq = MLA_ROPE // 4
    inv = (np.float32(ROPE_BASE) ** (-np.arange(n_freq, dtype=np.float32) / np.float32(n_freq))).astype(np.float32)
    ang = np.concatenate([row[:, None] * inv, col[:, None] * inv], axis=-1).astype(np.float32)
    cos, sin = np.cos(ang), np.sin(ang)
    cos_t = np.ones((DEC_SEQ, HEAD_PAD), np.float32)
    sin_t = np.zeros((DEC_SEQ, HEAD_PAD), np.float32)
    cos_t[:, ROPE_LANE0:ROPE_LANE0 + MLA_ROPE] = np.concatenate([cos, cos], axis=-1)
    sin_t[:, ROPE_LANE0:ROPE_LANE0 + MLA_ROPE] = np.concatenate([sin, sin], axis=-1)
    return jnp.asarray(cos_t), jnp.asarray(sin_t)


def _rope_partner(r):
    half = MLA_ROPE // 2
    return jnp.concatenate([-r[..., half:], r[..., :half]], axis=-1)


def _rope_lanes(r):
    return jnp.pad(r, [(0, 0)] * (r.ndim - 1) + [(ROPE_LANE0, HEAD_PAD - ROPE_LANE0 - MLA_ROPE)])


def _head_gain_partner(g):
    half = MLA_ROPE // 2
    r = g[:, MLA_NOPE:]
    return _rope_lanes(jnp.concatenate([r[:, half:], r[:, :half]], axis=-1))[:, None, :]


def _prepare_weights(w_in, ssm_d, ssm_w_glu, q_a_norm, kv_a_norm, w_uq, w_ukv, q_norm, k_norm,
                     gmlp_v_norm, gmlp_w_s, gmlp_b_s, w_out_norm, w_out, ffn_w_up, ffn_conv_w,
                     ffn_conv_b, ffn_w_down):
    w_kr = w_in[:, :, OFF_KR:OFF_GM]
    ukv = w_ukv.reshape(DEPTH, KV_RANK, MLA_HEADS, MLA_NOPE + MLA_V)
    uq = w_uq.reshape(DEPTH, Q_RANK, MLA_HEADS, MLA_QK)
    q_gain = q_norm * Q_SCALE
    row = lambda a: a[:, None, :]
    return {
        'w_in_a': jnp.concatenate([w_in[:, :, OFF_SSM:OFF_KR], w_in[:, :, OFF_GM:]], axis=2).astype(BF16),
        'w_in_kr': _rope_lanes(w_kr).astype(BF16),
        'w_in_krp': _rope_lanes(_rope_partner(w_kr)).astype(BF16),
        'w_uqp': _rope_lanes(_rope_partner(uq[..., MLA_NOPE:])).reshape(DEPTH, Q_RANK, -1).astype(BF16),
        'q_a_norm': row(q_a_norm), 'kv_a_norm': row(kv_a_norm),
        'w_uq': _pad_heads(w_uq, MLA_QK, MLA_QK).astype(BF16),
        'w_uk': _pad_heads(ukv[..., :MLA_NOPE].reshape(DEPTH, KV_RANK, -1), MLA_NOPE, MLA_NOPE).astype(BF16),
        'w_uv': _pad_heads(ukv[..., MLA_NOPE:].reshape(DEPTH, KV_RANK, -1), MLA_V, MLA_V).astype(BF16),
        'q_norm': _head_gain(q_gain), 'q_norm_p': _head_gain_partner(q_gain),
        'k_norm': _head_gain(k_norm), 'k_norm_p': _head_gain_partner(k_norm),
        'ssm_d': row(ssm_d), 'w_glu': ssm_w_glu.astype(BF16),
        'gmlp_v_norm': row(gmlp_v_norm), 'gmlp_w_s': gmlp_w_s.astype(BF16),
        'gmlp_bias': jnp.repeat(jnp.swapaxes(gmlp_b_s, 1, 2), GMLP_CH, axis=2),
        'w_out_norm': row(w_out_norm), 'w_out': w_out.astype(BF16),
        'ffn_w_up': ffn_w_up.astype(BF16), 'ffn_conv_w': ffn_conv_w,
        'ffn_conv_b': row(ffn_conv_b), 'ffn_w_down': ffn_w_down.astype(BF16),
    }


def kernel(x_prompt, x_sample, cache_ckv, cache_krope, state_ssm_re, state_ssm_im, c, c_ctx, w_mod, b_mod, w_in, ssm_a_re, ssm_a_im, ssm_b_re, ssm_b_im, ssm_c_re, ssm_c_im, ssm_log_dt, ssm_d, ssm_w_glu, q_a_norm, kv_a_norm, w_uq, w_ukv, q_norm, k_norm, gmlp_v_norm, gmlp_w_s, gmlp_b_s, w_out_norm, w_out, ffn_w_up, ffn_conv_w, ffn_conv_b, ffn_w_down):
    cond8 = jnp.concatenate([c_ctx[None, :], c, jnp.zeros((8 - 1 - DEC_BATCH, D_MODEL), F32)], axis=0)
    mod = _modulation(cond8, w_mod, b_mod).reshape(DEPTH, 8, N_MOD, D_MODEL)
    rope_tabs = _rope_tables()
    w = _prepare_weights(w_in, ssm_d, ssm_w_glu, q_a_norm, kv_a_norm, w_uq, w_ukv, q_norm, k_norm,
                         gmlp_v_norm, gmlp_w_s, gmlp_b_s, w_out_norm, w_out, ffn_w_up, ffn_conv_w,
                         ffn_conv_b, ffn_w_down)
    tabs = _ssm_tables(ssm_a_re, ssm_a_im, ssm_b_re, ssm_b_im, ssm_c_re, ssm_c_im, ssm_log_dt)
    cache_kr = _rope_lanes(cache_krope)

    def h0(st):
        st = jnp.transpose(st.reshape(DEC_BATCH, DEPTH, N_DIR, SSM_STATE), (1, 2, 0, 3))
        return jnp.pad(st, ((0, 0), (0, 0), (0, SSM_STREAMS - DEC_BATCH), (0, 0)))

    h0r, h0i = h0(state_ssm_re), h0(state_ssm_im)

    x = (x_prompt.reshape(N_PROMPT, D_MODEL), x_sample.reshape(N_SAMPLE, D_MODEL))
    ckv_out, kr_out, hre_out, him_out = [], [], [], []
    for l in range(DEPTH):
        u, gm, ckv, kr, q, k, v = _projection(x, mod, rope_tabs, w, l)
        k_c, v_c = _ctx_kv(cache_ckv, cache_kr, w, l)
        yb_ctx = _attention(q, [k], [v], BATCH, SEQ, 0, [SEQ], [0])
        yssm, hf_r, hf_i = _ssm(u, h0r, h0i, tabs, l)
        yb_lat = _attention(q, [k_c, k], [v_c, v], DEC_BATCH, DEC_SEQ, N_PROMPT,
                            [PAST_LEN, DEC_SEQ], [0, N_PROMPT])
        x1, h2 = _merge(x, mod, yssm, u, yb_ctx, yb_lat, gm, w, l)
        x = _ffn(x1, mod, h2, w, l, split_out=(l == DEPTH - 1))

        ckv_out.append(ckv.reshape(BATCH, SEQ, KV_RANK))
        kr_out.append(kr[:, ROPE_LANE0:ROPE_LANE0 + MLA_ROPE].reshape(BATCH, SEQ, MLA_ROPE))
        to_state = lambda st: jnp.transpose(st, (1, 0, 2)).reshape(BATCH, N_DIR, SSM_G, SSM_P)
        hre_out.append(to_state(hf_r))
        him_out.append(to_state(hf_i))

    y_prompt = x[0].reshape(BATCH, SEQ, D_MODEL)
    y_sample = x[1].reshape(DEC_BATCH, DEC_SEQ, D_MODEL)
    return (y_prompt, y_sample, jnp.stack(ckv_out, axis=1), jnp.stack(kr_out, axis=1),
            jnp.stack(hre_out, axis=1), jnp.stack(him_out, axis=1))
```

```python
import functools
import math

import jax
import jax.numpy as jnp
import numpy as np
from jax import lax
from jax.experimental import pallas as pl
from jax.experimental.pallas import tpu as pltpu

F32 = jnp.float32
BF16 = jnp.bfloat16

D_MODEL = 1024
BATCH = 16
SEQ = 256
DEPTH = 2
DEC_BATCH = 2
DEC_SEQ = 4096
PAST_LEN = 256
GRID_W = 64
EPS = 1e-6
N_MOD = 6
A_WIDTH = 256
B_WIDTH = 512
C_WIDTH = 256
SSM_CG = 16
SSM_G = 16
SSM_P = 64
N_DIR = 2
MLA_HEADS = 8
MLA_NOPE = 64
MLA_ROPE = 32
MLA_V = 64
MLA_QK = 96
Q_RANK = 256
KV_RANK = 128
ROPE_BASE = 10000.0
ATTN_SCALE = 1.0 / math.sqrt(MLA_QK)
Q_SCALE = ATTN_SCALE * math.log2(math.e)
GMLP_HEADS = 4
GMLP_CH = 64
CHUNK = 128
OFF_SSM = 0
OFF_Q = OFF_SSM + A_WIDTH
OFF_KV = OFF_Q + Q_RANK
OFF_KR = OFF_KV + KV_RANK
OFF_GM = OFF_KR + MLA_ROPE
D_FF = 2816

N_PROMPT = BATCH * SEQ
N_SAMPLE = DEC_BATCH * DEC_SEQ
N_TOK = N_PROMPT + N_SAMPLE
SSM_STATE = SSM_G * SSM_P

LANES = 128
HEAD_PAD = LANES
ROPE_LANE0 = MLA_NOPE

PROJ_TILE = 512
PROJ_SUB = 256
SSM_STREAMS = 8
SSM_STEPS = 256
SSM_UNIT = SSM_STREAMS * SSM_STEPS
ATTN_TQ = 256
ATTN_KC = 1024
ATTN_HEADS = 4
FFN_ROWS = SEQ
FFN_FC = 256
FFN_HALO = 16
VMEM_LIMIT = 48 * 1024 * 1024


def _mod_row(tile, tile_tokens):
    start = tile * tile_tokens
    return jnp.where(start < N_PROMPT, 0, 1 + (start - N_PROMPT) // DEC_SEQ)


def _layer_spec(a, l, **kw):
    return pl.BlockSpec((None,) + a.shape[1:], lambda *_: (l,) + (0,) * (a.ndim - 1), **kw)


def _split_specs(t, width, second_offset):
    n_a = N_PROMPT // t
    return (pl.BlockSpec((t, width), lambda i: (jnp.minimum(i, n_a - 1), 0)),
            pl.BlockSpec((t, width), lambda i: (jnp.maximum(i - n_a, 0) + second_offset, 0)))


def _split_args(x, t):
    if isinstance(x, tuple):
        return x, 0
    return (x, x), N_PROMPT // t


def _rms(x):
    return x * lax.rsqrt(jnp.mean(x * x, axis=-1, keepdims=True) + EPS)


def _sigmoid(x):
    return 1.0 / (1.0 + jnp.exp(-x))


def _gelu_tanh(x):
    return 0.5 * x * (1.0 + jnp.tanh(math.sqrt(2.0 / math.pi) * (x + 0.044715 * (x * x * x))))


def _dot(a, b):
    return jnp.dot(a, b, preferred_element_type=F32)


def _mod_kernel(cond_ref, w_ref, b_ref, o_ref):
    cond = cond_ref[...]
    s = (cond * _sigmoid(cond)).astype(BF16)
    o_ref[0] = _dot(s, w_ref[0].astype(BF16)) + b_ref[0]


def _modulation(cond8, w_mod, b_mod):
    tn = 1536
    n_cols = N_MOD * D_MODEL
    return pl.pallas_call(
        _mod_kernel,
        grid=(DEPTH, n_cols // tn),
        in_specs=[
            pl.BlockSpec((8, D_MODEL), lambda l, j: (0, 0)),
            pl.BlockSpec((1, D_MODEL, tn), lambda l, j: (l, 0, j)),
            pl.BlockSpec((1, 1, tn), lambda l, j: (l, 0, j)),
        ],
        out_specs=pl.BlockSpec((1, 8, tn), lambda l, j: (l, 0, j)),
        out_shape=jax.ShapeDtypeStruct((DEPTH, 8, n_cols), F32),
        compiler_params=pltpu.CompilerParams(vmem_limit_bytes=VMEM_LIMIT),
        name="modulation",
    )(cond8, w_mod, b_mod.reshape(DEPTH, 1, n_cols))


def _write_heads(allh, extra, gain, partner, out_ref, rows=slice(None)):
    for h in range(MLA_HEADS):
        tile = slice(h * HEAD_PAD, (h + 1) * HEAD_PAD)
        t = allh[:, tile]
        if extra is not None:
            t = t + extra
        rs = lax.rsqrt(jnp.sum(t * t, axis=-1, keepdims=True) * (1.0 / MLA_QK) + EPS)
        o = t * gain
        if isinstance(partner, tuple):
            o = o + partner[0][:, tile] * partner[1]
        elif partner is not None:
            o = o + partner
        out_ref[h, rows, :] = (o * rs).astype(BF16)


def _write_values(vall, v_ref, rows=slice(None)):
    lane = lax.broadcasted_iota(jnp.int32, (1, HEAD_PAD), 1)
    ones = jnp.where(lane >= MLA_V, 1.0, 0.0)
    for h in range(MLA_HEADS):
        v_ref[h, rows, :] = (vall[:, h * HEAD_PAD:(h + 1) * HEAD_PAD] + ones).astype(BF16)


def _proj_kernel(xp_ref, xs_ref, mod_ref, cos_ref, sin_ref, w_in_ref, w_kr_ref, w_krp_ref, qan_ref, kvn_ref,
                 w_uq_ref, w_uqp_ref, w_uk_ref, w_uv_ref, qn_ref, qnp_ref, kn_ref, knp_ref,
                 u_ref, gm_ref, ckv_ref, kr_ref, q_ref, k_ref, v_ref):
    latent = pl.program_id(0) >= N_PROMPT // PROJ_TILE
    shift = mod_ref[0, 0:1, :]
    scale = mod_ref[0, 1:2, :]

    def body(x_ref, rope):
        for sub in range(PROJ_TILE // PROJ_SUB):
            rows = slice(sub * PROJ_SUB, (sub + 1) * PROJ_SUB)
            h = (_rms(x_ref[rows, :]) * (1.0 + scale) + shift).astype(BF16)
            z = _dot(h, w_in_ref[...])
            kr = _dot(h, w_kr_ref[...])
            u_ref[rows, :] = z[:, 0:A_WIDTH]
            gm_ref[rows, :] = z[:, A_WIDTH + Q_RANK + KV_RANK:].astype(gm_ref.dtype)
            ckv = _rms(z[:, A_WIDTH + Q_RANK:A_WIDTH + Q_RANK + KV_RANK]) * kvn_ref[...]
            cq = (_rms(z[:, A_WIDTH:A_WIDTH + Q_RANK]) * qan_ref[...]).astype(BF16)
            ckv_b = ckv.astype(BF16)
            qall = _dot(cq, w_uq_ref[...])
            kall = _dot(ckv_b, w_uk_ref[...])
            _write_values(_dot(ckv_b, w_uv_ref[...]), v_ref, rows)
            if rope:
                cos = cos_ref[rows, :]
                sin = sin_ref[rows, :]
                _write_heads(qall, None, qn_ref[...] * cos, (_dot(cq, w_uqp_ref[...]), qnp_ref[...] * sin),
                             q_ref, rows)
                _write_heads(kall, kr, kn_ref[...] * cos, _dot(h, w_krp_ref[...]) * (knp_ref[...] * sin),
                             k_ref, rows)
            else:
                ckv_ref[rows, :] = ckv
                kr_ref[rows, :] = kr
                _write_heads(qall, None, qn_ref[...], None, q_ref, rows)
                _write_heads(kall, kr, kn_ref[...], None, k_ref, rows)

    pl.when(latent)(lambda: body(xs_ref, True))
    pl.when(jnp.logical_not(latent))(lambda: body(xp_ref, False))


def _projection(x, mod, rope_tabs, w, l):
    t = PROJ_TILE
    n_tiles = N_TOK // t
    n_ctx = N_PROMPT // t
    tok = lambda width: pl.BlockSpec((t, width), lambda i: (i, 0))
    ctx = lambda width: pl.BlockSpec((t, width), lambda i: (jnp.minimum(i, n_ctx - 1), 0))
    heads = pl.BlockSpec((MLA_HEADS, t, HEAD_PAD), lambda i: (0, i, 0))
    pos = pl.BlockSpec((t, HEAD_PAD), lambda i: (jnp.maximum(i - n_ctx, 0) % (DEC_SEQ // t), 0))
    weights = [w[n] for n in ('w_in_a', 'w_in_kr', 'w_in_krp', 'q_a_norm', 'kv_a_norm', 'w_uq', 'w_uqp',
                              'w_uk', 'w_uv', 'q_norm', 'q_norm_p', 'k_norm', 'k_norm_p')]
    xs, off = _split_args(x, t)
    head_shape = jax.ShapeDtypeStruct((MLA_HEADS, N_TOK, HEAD_PAD), BF16)
    return pl.pallas_call(
        _proj_kernel,
        grid=(n_tiles,),
        in_specs=[*_split_specs(t, D_MODEL, off),
                  pl.BlockSpec((None, 1, N_MOD, D_MODEL), lambda i: (l, _mod_row(i, t), 0, 0)),
                  pos, pos] + [_layer_spec(a, l) for a in weights],
        out_specs=[tok(A_WIDTH), tok(2 * C_WIDTH), ctx(KV_RANK), ctx(HEAD_PAD), heads, heads, heads],
        out_shape=[
            jax.ShapeDtypeStruct((N_TOK, A_WIDTH), F32),
            jax.ShapeDtypeStruct((N_TOK, 2 * C_WIDTH), BF16),
            jax.ShapeDtypeStruct((N_PROMPT, KV_RANK), F32),
            jax.ShapeDtypeStruct((N_PROMPT, HEAD_PAD), F32),
            head_shape, head_shape, head_shape,
        ],
        compiler_params=pltpu.CompilerParams(vmem_limit_bytes=VMEM_LIMIT),
        name="projection",
    )(*xs, mod, *rope_tabs, *weights)


def _ctx_kv_kernel(ckv_ref, kr_ref, w_uk_ref, w_uv_ref, kn_ref, k_ref, v_ref):
    n = DEC_BATCH * PAST_LEN
    ckv_b = ckv_ref[...].reshape(n, KV_RANK).astype(BF16)
    _write_heads(_dot(ckv_b, w_uk_ref[...]), kr_ref[...].reshape(n, HEAD_PAD), kn_ref[...], None, k_ref)
    _write_values(_dot(ckv_b, w_uv_ref[...]), v_ref)


def _ctx_kv(cache_ckv, cache_kr, w, l):
    n = DEC_BATCH * PAST_LEN
    cache = pl.BlockSpec((DEC_BATCH, None, PAST_LEN, HEAD_PAD), lambda i: (0, l, 0, 0))
    weights = [w['w_uk'], w['w_uv'], w['k_norm']]
    heads = pl.BlockSpec((MLA_HEADS, n, HEAD_PAD), lambda i: (0, 0, 0))
    head_shape = jax.ShapeDtypeStruct((MLA_HEADS, n, HEAD_PAD), BF16)
    return pl.pallas_call(
        _ctx_kv_kernel,
        grid=(1,),
        in_specs=[cache, cache] + [_layer_spec(a, l) for a in weights],
        out_specs=[heads, heads],
        out_shape=[head_shape, head_shape],
        compiler_params=pltpu.CompilerParams(vmem_limit_bytes=VMEM_LIMIT),
        name="context_kv",
    )(cache_ckv, cache_kr, *weights)


def _ssm_block(d, k):
    return jnp.where((d == 1) & (k >= 2), k ^ 1, k)


def _ssm_kernel(u_ref, h0r_ref, h0i_ref, ar_ref, ai_ref, pwr_ref, pwi_ref, bm_ref, cr_ref, ci_ref,
                y_ref, hfr_ref, hfi_ref, sre, sim, hin_r, hin_i, car_r, car_i):
    d = pl.program_id(0)
    k = pl.program_id(1)
    rows = 256

    @pl.when(k == 0)
    def _():
        car_r[...] = jnp.zeros_like(car_r)
        car_i[...] = jnp.zeros_like(car_i)

    for c in range(SSM_UNIT // rows):
        sl = slice(c * rows, (c + 1) * rows)
        bu = _dot(u_ref[sl, :].astype(BF16), bm_ref[0])
        sre[sl, :] = bu[:, :SSM_STATE]
        sim[sl, :] = bu[:, SSM_STATE:]

    a_r = jnp.broadcast_to(ar_ref[0], (SSM_STREAMS, SSM_STATE))
    a_i = jnp.broadcast_to(ai_ref[0], (SSM_STREAMS, SSM_STATE))

    def stream_rows(tau):
        return pl.ds(pl.multiple_of(tau * SSM_STREAMS, SSM_STREAMS), SSM_STREAMS)

    def scan_step(i, carry):
        h_r, h_i = carry
        tau = jnp.where(d == 0, i, SSM_STEPS - 1 - i)
        n_r = a_r * h_r - a_i * h_i + sre[stream_rows(tau), :]
        n_i = a_r * h_i + a_i * h_r + sim[stream_rows(tau), :]
        sre[stream_rows(tau), :] = n_r
        sim[stream_rows(tau), :] = n_i
        return n_r, n_i

    zero = jnp.zeros((SSM_STREAMS, SSM_STATE), F32)
    end_r, end_i = lax.fori_loop(0, SSM_STEPS, scan_step, (zero, zero), unroll=8)

    @pl.when(k < 2)
    def _():
        hfr_ref[0] = end_r
        hfi_ref[0] = end_i

    @pl.when(k >= 2)
    def _():
        b = (k - 2) // 2
        first = (k % 2) == 0
        c_r = jnp.where(first, h0r_ref[0, pl.ds(b, 1), :], car_r[...])
        c_i = jnp.where(first, h0i_ref[0, pl.ds(b, 1), :], car_i[...])
        full_r = pwr_ref[0, SSM_STEPS - 1:SSM_STEPS, :]
        full_i = pwi_ref[0, SSM_STEPS - 1:SSM_STEPS, :]

        def chain(order):
            cr, ci = c_r, c_i
            for j in order:
                hin_r[j:j + 1, :] = cr
                hin_i[j:j + 1, :] = ci
                nr = end_r[j:j + 1, :] + full_r * cr - full_i * ci
                ni = end_i[j:j + 1, :] + full_r * ci + full_i * cr
                cr, ci = nr, ni
            car_r[...] = cr
            car_i[...] = ci

        @pl.when(d == 0)
        def _():
            chain(range(SSM_STREAMS))

        @pl.when(d == 1)
        def _():
            chain(range(SSM_STREAMS - 1, -1, -1))

        g_r = hin_r[...]
        g_i = hin_i[...]

        def fix_step(i, carry):
            p = jnp.where(d == 0, i, SSM_STEPS - 1 - i)
            p_r = jnp.broadcast_to(pwr_ref[0, pl.ds(p, 1), :], (SSM_STREAMS, SSM_STATE))
            p_i = jnp.broadcast_to(pwi_ref[0, pl.ds(p, 1), :], (SSM_STREAMS, SSM_STATE))
            sr = stream_rows(i)
            sre[sr, :] = sre[sr, :] + (p_r * g_r - p_i * g_i)
            sim[sr, :] = sim[sr, :] + (p_r * g_i + p_i * g_r)
            return carry

        lax.fori_loop(0, SSM_STEPS, fix_step, 0, unroll=4)

    for c in range(SSM_UNIT // rows):
        sl = slice(c * rows, (c + 1) * rows)
        y_ref[0, sl, :] = (_dot(sre[sl, :].astype(BF16), cr_ref[0])
                           + _dot(sim[sl, :].astype(BF16), ci_ref[0])).astype(y_ref.dtype)


def _step_major(x, inverse=False):
    lead = x.shape[:-2]
    a, b = (SSM_STEPS, SSM_STREAMS) if inverse else (SSM_STREAMS, SSM_STEPS)
    x = x.reshape(lead + (N_TOK // SSM_UNIT, a, b, x.shape[-1]))
    return jnp.swapaxes(x, -3, -2).reshape(lead + (N_TOK, x.shape[-1]))


def _ssm(u, h0r, h0i, tabs, l):
    n_units = N_TOK // SSM_UNIT
    u = _step_major(u)
    per_dir = lambda a: pl.BlockSpec((None, 1) + a.shape[2:], lambda d, k: (l, d) + (0,) * (a.ndim - 2))
    operands = [h0r, h0i] + [tabs[n] for n in ('a_r', 'a_i', 'pw_r', 'pw_i', 'b_blk', 'c_r', 'c_i')]
    y, hf_r, hf_i = pl.pallas_call(
        _ssm_kernel,
        grid=(N_DIR, n_units),
        in_specs=[pl.BlockSpec((SSM_UNIT, A_WIDTH), lambda d, k: (_ssm_block(d, k), 0))]
                 + [per_dir(a) for a in operands],
        out_specs=[pl.BlockSpec((1, SSM_UNIT, A_WIDTH), lambda d, k: (d, _ssm_block(d, k), 0)),
                   pl.BlockSpec((1, SSM_STREAMS, SSM_STATE), lambda d, k: (d, jnp.minimum(k, 1), 0)),
                   pl.BlockSpec((1, SSM_STREAMS, SSM_STATE), lambda d, k: (d, jnp.minimum(k, 1), 0))],
        out_shape=[jax.ShapeDtypeStruct((N_DIR, N_TOK, A_WIDTH), BF16),
                   jax.ShapeDtypeStruct((N_DIR, BATCH, SSM_STATE), F32),
                   jax.ShapeDtypeStruct((N_DIR, BATCH, SSM_STATE), F32)],
        scratch_shapes=[pltpu.VMEM((SSM_UNIT, SSM_STATE), F32), pltpu.VMEM((SSM_UNIT, SSM_STATE), F32),
                        pltpu.VMEM((SSM_STREAMS, SSM_STATE), F32), pltpu.VMEM((SSM_STREAMS, SSM_STATE), F32),
                        pltpu.VMEM((1, SSM_STATE), F32), pltpu.VMEM((1, SSM_STATE), F32)],
        compiler_params=pltpu.CompilerParams(vmem_limit_bytes=VMEM_LIMIT,
                                             dimension_semantics=("arbitrary", "arbitrary")),
        name="ssm_scan",
    )(u, *operands)
    return _step_major(y, inverse=True), hf_r, hf_i


def _attn_kernel(*refs, key_lens):
    n_src = len(key_lens)
    q_ref = refs[0]
    k_refs = refs[1:1 + n_src]
    v_refs = refs[1 + n_src:1 + 2 * n_src]
    o_ref = refs[1 + 2 * n_src]
    s_ref = refs[2 + 2 * n_src]
    chunks = []
    off = 0
    for si, n in enumerate(key_lens):
        for c0 in range(0, n, ATTN_KC):
            cl = min(ATTN_KC, n - c0)
            chunks.append((si, c0, cl, off))
            off += cl
    tq = q_ref.shape[1]
    row_max = []
    n_heads = q_ref.shape[0]
    for hh in range(n_heads):
        q = q_ref[hh]
        mp = jnp.full((tq, LANES), -jnp.inf, F32)
        for si, c0, cl, o in chunks:
            s = lax.dot_general(q, k_refs[si][hh, c0:c0 + cl, :], (((1,), (1,)), ((), ())),
                                preferred_element_type=F32)
            s_ref[hh, :, o:o + cl] = s
            for g in range(cl // LANES):
                mp = jnp.maximum(mp, s[:, g * LANES:(g + 1) * LANES])
        row_max.append(jnp.max(mp, axis=-1, keepdims=True))
    outs = []
    for hh in range(n_heads):
        acc = jnp.zeros((tq, LANES), F32)
        for si, c0, cl, o in chunks:
            p = jnp.exp2(s_ref[hh, :, o:o + cl] - row_max[hh])
            acc = acc + _dot(p.astype(BF16), v_refs[si][hh, c0:c0 + cl, :])
        outs.append(acc / pltpu.roll(acc, MLA_V, 1))
    lane = lax.broadcasted_iota(jnp.int32, (tq, LANES), 1)
    for pair in range(n_heads // 2):
        o_ref[:, pair * LANES:(pair + 1) * LANES] = jnp.where(
            lane < MLA_V, outs[2 * pair], pltpu.roll(outs[2 * pair + 1], MLA_V, 1)).astype(o_ref.dtype)


def _attention(q, ks, vs, n_seq, seq_len, tok0, key_lens, key_tok0):
    tq = min(ATTN_TQ, seq_len)
    n_qt = seq_len // tq
    q0 = tok0 // tq
    in_specs = [pl.BlockSpec((ATTN_HEADS, tq, HEAD_PAD), lambda b, hp, qi: (hp, q0 + b * n_qt + qi, 0))]
    for n, t0 in zip(key_lens, key_tok0):
        in_specs.append(pl.BlockSpec((ATTN_HEADS, n, HEAD_PAD), lambda b, hp, qi, n=n, t0=t0: (hp, t0 // n + b, 0)))
    for n, t0 in zip(key_lens, key_tok0):
        in_specs.append(pl.BlockSpec((ATTN_HEADS, n, HEAD_PAD), lambda b, hp, qi, n=n, t0=t0: (hp, t0 // n + b, 0)))
    return pl.pallas_call(
        functools.partial(_attn_kernel, key_lens=tuple(key_lens)),
        grid=(n_seq, MLA_HEADS // ATTN_HEADS, n_qt),
        in_specs=in_specs,
        out_specs=pl.BlockSpec((tq, ATTN_HEADS * MLA_V), lambda b, hp, qi: (b * n_qt + qi, hp)),
        out_shape=jax.ShapeDtypeStruct((n_seq * seq_len, B_WIDTH), BF16),
        scratch_shapes=[pltpu.VMEM((ATTN_HEADS, tq, sum(key_lens)), F32)],
        compiler_params=pltpu.CompilerParams(vmem_limit_bytes=VMEM_LIMIT,
                                             dimension_semantics=("arbitrary", "arbitrary", "arbitrary")),
        name="attention",
    )(q, *ks, *vs)


def _merge_kernel(xp_ref, xs_ref, mod_ref, yssm_ref, u_ref, ybp_ref, ybs_ref, gm_ref, dskip_ref, w_glu_ref,
                  vn_ref, w_s_ref, bias_ref, g_ref, w_out_ref, x1_ref, h2_ref):
    t = xp_ref.shape[0]
    latent = pl.program_id(0) >= N_PROMPT // t
    lane = lax.broadcasted_iota(jnp.int32, (CHUNK, C_WIDTH), 1)
    g = g_ref[...]

    def body(x_ref, yb_ref):
        for sub in range(t // PROJ_SUB):
            rows = slice(sub * PROJ_SUB, (sub + 1) * PROJ_SUB)
            y = (yssm_ref[0, rows, :].astype(F32) + yssm_ref[1, rows, :].astype(F32)
                 + dskip_ref[...] * u_ref[rows, :])
            y = _gelu_tanh(y)
            y_a = y * _sigmoid(_dot(y.astype(BF16), w_glu_ref[...]))
            gm = gm_ref[rows, :].astype(F32)
            vn = (_rms(gm[:, C_WIDTH:]) * vn_ref[...]).astype(BF16)
            mixed = []
            for c in range(PROJ_SUB // CHUNK):
                vc = vn[c * CHUNK:(c + 1) * CHUNK, :]
                m = _dot(w_s_ref[0], vc)
                for h in range(1, GMLP_HEADS):
                    m = jnp.where(lane >= h * GMLP_CH, _dot(w_s_ref[h], vc), m)
                mixed.append(m + bias_ref[...])
            y_c = gm[:, :C_WIDTH] * jnp.concatenate(mixed, axis=0)
            n_a = (_rms(y_a) * g[:, :A_WIDTH]).astype(BF16)
            n_b = (_rms(yb_ref[rows, :].astype(F32)) * g[:, A_WIDTH:A_WIDTH + B_WIDTH]).astype(BF16)
            n_c = (_rms(y_c) * g[:, A_WIDTH + B_WIDTH:]).astype(BF16)
            o = (_dot(n_a, w_out_ref[0:A_WIDTH, :]) + _dot(n_b, w_out_ref[A_WIDTH:A_WIDTH + B_WIDTH, :])
                 + _dot(n_c, w_out_ref[A_WIDTH + B_WIDTH:, :]))
            x1 = x_ref[rows, :] + mod_ref[0, 2:3, :] * o
            x1_ref[rows, :] = x1
            h2_ref[rows, :] = (_rms(x1) * (1.0 + mod_ref[0, 4:5, :]) + mod_ref[0, 3:4, :]).astype(BF16)

    pl.when(latent)(lambda: body(xs_ref, ybs_ref))
    pl.when(jnp.logical_not(latent))(lambda: body(xp_ref, ybp_ref))


def _merge(x, mod, yssm, u, yb_ctx, yb_lat, gm, w, l):
    t = PROJ_TILE
    tok = lambda width: pl.BlockSpec((t, width), lambda i: (i, 0))
    weights = [w[n] for n in ('ssm_d', 'w_glu', 'gmlp_v_norm', 'gmlp_w_s', 'gmlp_bias', 'w_out_norm', 'w_out')]
    xs, off = _split_args(x, t)
    return pl.pallas_call(
        _merge_kernel,
        grid=(N_TOK // t,),
        in_specs=[*_split_specs(t, D_MODEL, off),
                  pl.BlockSpec((None, 1, N_MOD, D_MODEL), lambda i: (l, _mod_row(i, t), 0, 0)),
                  pl.BlockSpec((N_DIR, t, A_WIDTH), lambda i: (0, i, 0)),
                  tok(A_WIDTH), *_split_specs(t, B_WIDTH, 0), tok(2 * C_WIDTH)]
                 + [_layer_spec(a, l) for a in weights],
        out_specs=[tok(D_MODEL), tok(D_MODEL)],
        out_shape=[jax.ShapeDtypeStruct((N_TOK, D_MODEL), F32),
                   jax.ShapeDtypeStruct((N_TOK, D_MODEL), BF16)],
        compiler_params=pltpu.CompilerParams(vmem_limit_bytes=VMEM_LIMIT),
        name="merge",
    )(*xs, mod, yssm, u, yb_ctx, yb_lat, gm, *weights)


def _ffn_kernel(x1_ref, mod_ref, hp_ref, hm_ref, hn_ref, wup_ref, cw_ref, cb_ref, wd_ref, *rest):
    out_refs, a_scr = rest[:-1], rest[-1]
    seg = pl.program_id(0) - N_PROMPT // FFN_ROWS
    per_seq = DEC_SEQ // FFN_ROWS
    starts = (seg < 0) | ((seg & (per_seq - 1)) == 0)
    ends = (seg < 0) | ((seg & (per_seq - 1)) == per_seq - 1)
    prev_row = hp_ref[...].astype(F32)[FFN_HALO - 1:FFN_HALO, :] * jnp.where(starts, 0.0, 1.0)
    next_row = hn_ref[...].astype(F32)[0:1, :] * jnp.where(ends, 0.0, 1.0)
    edge_row = lax.broadcasted_iota(jnp.int32, (FFN_HALO, D_MODEL), 0)
    edge = jnp.where(edge_row == 0, prev_row, jnp.where(edge_row == 1, next_row, 0.0)).astype(BF16)
    he = jnp.concatenate([hm_ref[...], edge], axis=0)
    tok_row = lax.broadcasted_iota(jnp.int32, (FFN_ROWS, FFN_FC), 0)

    def conv(u, c0):
        cols = slice(c0, c0 + FFN_FC)
        cur = u[:FFN_ROWS]
        prev = jnp.where(tok_row == 0, u[FFN_ROWS:FFN_ROWS + 1], pltpu.roll(cur, 1, 0))
        nxt = jnp.where(tok_row == FFN_ROWS - 1, u[FFN_ROWS + 1:FFN_ROWS + 2], pltpu.roll(cur, FFN_ROWS - 1, 0))
        return prev * cw_ref[0:1, cols] + cur * cw_ref[1:2, cols] + nxt * cw_ref[2:3, cols] + cb_ref[:, cols]

    for j in range(D_FF // FFN_FC):
        cg = j * FFN_FC
        cv = D_FF + j * FFN_FC
        gate = conv(_dot(he, wup_ref[:, cg:cg + FFN_FC]), cg)
        val = conv(_dot(he, wup_ref[:, cv:cv + FFN_FC]), cv)
        a_scr[:, cg:cg + FFN_FC] = (gate * _sigmoid(gate) * val).astype(BF16)

    out = x1_ref[...] + mod_ref[0, 5:6, :] * _dot(a_scr[...], wd_ref[...])
    if len(out_refs) == 1:
        out_refs[0][...] = out
    else:
        @pl.when(seg < 0)
        def _():
            out_refs[0][...] = out

        @pl.when(seg >= 0)
        def _():
            out_refs[1][...] = out


def _ffn(x1, mod, h2, w, l, split_out):
    t = FFN_ROWS
    hb = t // FFN_HALO
    last_halo = N_TOK // FFN_HALO - 1
    weights = [w[n] for n in ('ffn_w_up', 'ffn_conv_w', 'ffn_conv_b', 'ffn_w_down')]
    if split_out:
        out_specs = list(_split_specs(t, D_MODEL, 0))
        out_shape = [jax.ShapeDtypeStruct((N_PROMPT, D_MODEL), F32), jax.ShapeDtypeStruct((N_SAMPLE, D_MODEL), F32)]
    else:
        out_specs = pl.BlockSpec((t, D_MODEL), lambda i: (i, 0))
        out_shape = jax.ShapeDtypeStruct((N_TOK, D_MODEL), F32)
    return pl.pallas_call(
        _ffn_kernel,
        grid=(N_TOK // t,),
        in_specs=[pl.BlockSpec((t, D_MODEL), lambda i: (i, 0)),
                  pl.BlockSpec((None, 1, N_MOD, D_MODEL), lambda i: (l, _mod_row(i, t), 0, 0)),
                  pl.BlockSpec((FFN_HALO, D_MODEL), lambda i: (jnp.maximum(i * hb - 1, 0), 0)),
                  pl.BlockSpec((t, D_MODEL), lambda i: (i, 0)),
                  pl.BlockSpec((FFN_HALO, D_MODEL), lambda i: (jnp.minimum((i + 1) * hb, last_halo), 0))]
                 + [_layer_spec(a, l, pipeline_mode=pl.Buffered(1)) for a in weights],
        out_specs=out_specs,
        out_shape=out_shape,
        scratch_shapes=[pltpu.VMEM((t, D_FF), BF16)],
        compiler_params=pltpu.CompilerParams(vmem_limit_bytes=VMEM_LIMIT,
                                             dimension_semantics=("arbitrary",)),
        name="conv_ffn",
    )(x1, mod, h2, h2, h2, *weights)


def _pad_heads(w, head_w, used, offset=0):
    lead = w.shape[:-1]
    w = w.reshape(lead + (MLA_HEADS, head_w))[..., :used]
    pad = [(0, 0)] * (len(lead) + 1) + [(offset, HEAD_PAD - used - offset)]
    return jnp.pad(w, pad).reshape(lead + (MLA_HEADS * HEAD_PAD,))


def _head_gain(g):
    return jnp.pad(g, ((0, 0), (0, HEAD_PAD - MLA_QK)))[:, None, :]


def _ssm_tables(a_re, a_im, b_re, b_im, c_re, c_im, log_dt):
    a = lax.complex(a_re, a_im)
    dt = jnp.exp(log_dt)[..., None]
    a_bar = jnp.exp(a * dt)
    b_bar = ((a_bar - 1.0) / a)[..., None] * lax.complex(b_re, b_im)
    steps = jnp.arange(1, SSM_STEPS + 1, dtype=F32)[None, None, :, None, None]
    pw = jnp.exp((a * dt)[:, :, None] * steps)
    eye = jnp.eye(SSM_G, dtype=F32)
    blk_b = lambda m: jnp.einsum('ldgpc,gh->ldgchp', m, eye).reshape(DEPTH, N_DIR, A_WIDTH, SSM_STATE)
    blk_c = lambda m: jnp.einsum('ldgcp,gh->ldgphc', m, eye).reshape(DEPTH, N_DIR, SSM_STATE, A_WIDTH)
    flat = lambda m: m.reshape(DEPTH, N_DIR, 1, SSM_STATE)
    return {
        'a_r': flat(a_bar.real), 'a_i': flat(a_bar.imag),
        'pw_r': pw.real.reshape(DEPTH, N_DIR, SSM_STEPS, SSM_STATE),
        'pw_i': pw.imag.reshape(DEPTH, N_DIR, SSM_STEPS, SSM_STATE),
        'b_blk': jnp.concatenate([blk_b(b_bar.real), blk_b(b_bar.imag)], axis=-1).astype(BF16),
        'c_r': blk_c(c_re).astype(BF16),
        'c_i': blk_c(-c_im).astype(BF16),
    }


def _rope_tables():
    rows = DEC_SEQ // GRID_W
    row = np.repeat(np.arange(rows, dtype=np.float32), GRID_W)
    col = np.tile(np.arange(GRID_W, dtype=np.float32), rows)
    n_freq = MLA_ROPE // 4
    inv = (np.float32(ROPE_BASE) ** (-np.arange(n_freq, dtype=np.float32) / np.float32(n_freq))).astype(np.float32)
    ang = np.concatenate([row[:, None] * inv, col[:, None] * inv], axis=-1).astype(np.float32)
    cos, sin = np.cos(ang), np.sin(ang)
    cos_t = np.ones((DEC_SEQ, HEAD_PAD), np.float32)
    sin_t = np.zeros((DEC_SEQ, HEAD_PAD), np.float32)
    cos_t[:, ROPE_LANE0:ROPE_LANE0 + MLA_ROPE] = np.concatenate([cos, cos], axis=-1)
    sin_t[:, ROPE_LANE0:ROPE_LANE0 + MLA_ROPE] = np.concatenate([sin, sin], axis=-1)
    return jnp.asarray(cos_t), jnp.asarray(sin_t)


def _rope_partner(r):
    half = MLA_ROPE // 2
    return jnp.concatenate([-r[..., half:], r[..., :half]], axis=-1)


def _rope_lanes(r):
    return jnp.pad(r, [(0, 0)] * (r.ndim - 1) + [(ROPE_LANE0, HEAD_PAD - ROPE_LANE0 - MLA_ROPE)])


def _head_gain_partner(g):
    half = MLA_ROPE // 2
    r = g[:, MLA_NOPE:]
    return _rope_lanes(jnp.concatenate([r[:, half:], r[:, :half]], axis=-1))[:, None, :]


def _prepare_weights(w_in, ssm_d, ssm_w_glu, q_a_norm, kv_a_norm, w_uq, w_ukv, q_norm, k_norm,
                     gmlp_v_norm, gmlp_w_s, gmlp_b_s, w_out_norm, w_out, ffn_w_up, ffn_conv_w,
                     ffn_conv_b, ffn_w_down):
    w_kr = w_in[:, :, OFF_KR:OFF_GM]
    ukv = w_ukv.reshape(DEPTH, KV_RANK, MLA_HEADS, MLA_NOPE + MLA_V)
    uq = w_uq.reshape(DEPTH, Q_RANK, MLA_HEADS, MLA_QK)
    q_gain = q_norm * Q_SCALE
    row = lambda a: a[:, None, :]
    return {
        'w_in_a': jnp.concatenate([w_in[:, :, OFF_SSM:OFF_KR], w_in[:, :, OFF_GM:]], axis=2).astype(BF16),
        'w_in_kr': _rope_lanes(w_kr).astype(BF16),
        'w_in_krp': _rope_lanes(_rope_partner(w_kr)).astype(BF16),
        'w_uqp': _rope_lanes(_rope_partner(uq[..., MLA_NOPE:])).reshape(DEPTH, Q_RANK, -1).astype(BF16),
        'q_a_norm': row(q_a_norm), 'kv_a_norm': row(kv_a_norm),
        'w_uq': _pad_heads(w_uq, MLA_QK, MLA_QK).astype(BF16),
        'w_uk': _pad_heads(ukv[..., :MLA_NOPE].reshape(DEPTH, KV_RANK, -1), MLA_NOPE, MLA_NOPE).astype(BF16),
        'w_uv': _pad_heads(ukv[..., MLA_NOPE:].reshape(DEPTH, KV_RANK, -1), MLA_V, MLA_V).astype(BF16),
        'q_norm': _head_gain(q_gain), 'q_norm_p': _head_gain_partner(q_gain),
        'k_norm': _head_gain(k_norm), 'k_norm_p': _head_gain_partner(k_norm),
        'ssm_d': row(ssm_d), 'w_glu': ssm_w_glu.astype(BF16),
        'gmlp_v_norm': row(gmlp_v_norm), 'gmlp_w_s': gmlp_w_s.astype(BF16),
        'gmlp_bias': jnp.repeat(jnp.swapaxes(gmlp_b_s, 1, 2), GMLP_CH, axis=2),
        'w_out_norm': row(w_out_norm), 'w_out': w_out.astype(BF16),
        'ffn_w_up': ffn_w_up.astype(BF16), 'ffn_conv_w': ffn_conv_w,
        'ffn_conv_b': row(ffn_conv_b), 'ffn_w_down': ffn_w_down.astype(BF16),
    }


def kernel(x_prompt, x_sample, cache_ckv, cache_krope, state_ssm_re, state_ssm_im, c, c_ctx, w_mod, b_mod, w_in, ssm_a_re, ssm_a_im, ssm_b_re, ssm_b_im, ssm_c_re, ssm_c_im, ssm_log_dt, ssm_d, ssm_w_glu, q_a_norm, kv_a_norm, w_uq, w_ukv, q_norm, k_norm, gmlp_v_norm, gmlp_w_s, gmlp_b_s, w_out_norm, w_out, ffn_w_up, ffn_conv_w, ffn_conv_b, ffn_w_down):
    cond8 = jnp.concatenate([c_ctx[None, :], c, jnp.zeros((8 - 1 - DEC_BATCH, D_MODEL), F32)], axis=0)
    mod = _modulation(cond8, w_mod, b_mod).reshape(DEPTH, 8, N_MOD, D_MODEL)
    rope_tabs = _rope_tables()
    w = _prepare_weights(w_in, ssm_d, ssm_w_glu, q_a_norm, kv_a_norm, w_uq, w_ukv, q_norm, k_norm,
                         gmlp_v_norm, gmlp_w_s, gmlp_b_s, w_out_norm, w_out, ffn_w_up, ffn_conv_w,
                         ffn_conv_b, ffn_w_down)
    tabs = _ssm_tables(ssm_a_re, ssm_a_im, ssm_b_re, ssm_b_im, ssm_c_re, ssm_c_im, ssm_log_dt)
    cache_kr = _rope_lanes(cache_krope)

    def h0(st):
        st = jnp.transpose(st.reshape(DEC_BATCH, DEPTH, N_DIR, SSM_STATE), (1, 2, 0, 3))
        return jnp.pad(st, ((0, 0), (0, 0), (0, SSM_STREAMS - DEC_BATCH), (0, 0)))

    h0r, h0i = h0(state_ssm_re), h0(state_ssm_im)

    x = (x_prompt.reshape(N_PROMPT, D_MODEL), x_sample.reshape(N_SAMPLE, D_MODEL))
    ckv_out, kr_out, hre_out, him_out = [], [], [], []
    for l in range(DEPTH):
        u, gm, ckv, kr, q, k, v = _projection(x, mod, rope_tabs, w, l)
        k_c, v_c = _ctx_kv(cache_ckv, cache_kr, w, l)
        yb_ctx = _attention(q, [k], [v], BATCH, SEQ, 0, [SEQ], [0])
        yssm, hf_r, hf_i = _ssm(u, h0r, h0i, tabs, l)
        yb_lat = _attention(q, [k_c, k], [v_c, v], DEC_BATCH, DEC_SEQ, N_PROMPT,
                            [PAST_LEN, DEC_SEQ], [0, N_PROMPT])
        x1, h2 = _merge(x, mod, yssm, u, yb_ctx, yb_lat, gm, w, l)
        x = _ffn(x1, mod, h2, w, l, split_out=(l == DEPTH - 1))

        ckv_out.append(ckv.reshape(BATCH, SEQ, KV_RANK))
        kr_out.append(kr[:, ROPE_LANE0:ROPE_LANE0 + MLA_ROPE].reshape(BATCH, SEQ, MLA_ROPE))
        to_state = lambda st: jnp.transpose(st, (1, 0, 2)).reshape(BATCH, N_DIR, SSM_G, SSM_P)
        hre_out.append(to_state(hf_r))
        him_out.append(to_state(hf_i))

    y_prompt = x[0].reshape(BATCH, SEQ, D_MODEL)
    y_sample = x[1].reshape(DEC_BATCH, DEC_SEQ, D_MODEL)
    return (y_prompt, y_sample, jnp.stack(ckv_out, axis=1), jnp.stack(kr_out, axis=1),
            jnp.stack(hre_out, axis=1), jnp.stack(him_out, axis=1))
```

```python
import functools
import math

import jax
import jax.numpy as jnp
import numpy as np
from jax import lax
from jax.experimental import pallas as pl
from jax.experimental.pallas import tpu as pltpu

F32 = jnp.float32
BF16 = jnp.bfloat16

D_MODEL = 1024
BATCH = 16
SEQ = 256
DEPTH = 2
DEC_BATCH = 2
DEC_SEQ = 4096
PAST_LEN = 256
GRID_W = 64
EPS = 1e-6
N_MOD = 6
A_WIDTH = 256
B_WIDTH = 512
C_WIDTH = 256
SSM_CG = 16
SSM_G = 16
SSM_P = 64
N_DIR = 2
MLA_HEADS = 8
MLA_NOPE = 64
MLA_ROPE = 32
MLA_V = 64
MLA_QK = 96
Q_RANK = 256
KV_RANK = 128
ROPE_BASE = 10000.0
ATTN_SCALE = 1.0 / math.sqrt(MLA_QK)
Q_SCALE = ATTN_SCALE * math.log2(math.e)
GMLP_HEADS = 4
GMLP_CH = 64
CHUNK = 128
OFF_SSM = 0
OFF_Q = OFF_SSM + A_WIDTH
OFF_KV = OFF_Q + Q_RANK
OFF_KR = OFF_KV + KV_RANK
OFF_GM = OFF_KR + MLA_ROPE
D_FF = 2816

N_PROMPT = BATCH * SEQ
N_SAMPLE = DEC_BATCH * DEC_SEQ
N_TOK = N_PROMPT + N_SAMPLE
SSM_STATE = SSM_G * SSM_P

LANES = 128
HEAD_PAD = LANES
ROPE_LANE0 = MLA_NOPE

PROJ_TILE = 512
PROJ_SUB = 256
SSM_STREAMS = 8
SSM_STEPS = 256
SSM_UNIT = SSM_STREAMS * SSM_STEPS
ATTN_TQ = 256
ATTN_KC = 1024
ATTN_HEADS = 4
FFN_ROWS = SEQ
FFN_SEGS = 2
FFN_FC = 256
FFN_HALO = 16
VMEM_LIMIT = 48 * 1024 * 1024


def _mod_row(tile, tile_tokens):
    start = tile * tile_tokens
    return jnp.where(start < N_PROMPT, 0, 1 + (start - N_PROMPT) // DEC_SEQ)


def _layer_spec(a, l, **kw):
    return pl.BlockSpec((None,) + a.shape[1:], lambda *_: (l,) + (0,) * (a.ndim - 1), **kw)


def _split_specs(t, width, second_offset):
    n_a = N_PROMPT // t
    return (pl.BlockSpec((t, width), lambda i: (jnp.minimum(i, n_a - 1), 0)),
            pl.BlockSpec((t, width), lambda i: (jnp.maximum(i - n_a, 0) + second_offset, 0)))


def _split_args(x, t):
    if isinstance(x, tuple):
        return x, 0
    return (x, x), N_PROMPT // t


def _rms(x):
    return x * lax.rsqrt(jnp.mean(x * x, axis=-1, keepdims=True) + EPS)


def _sigmoid(x):
    return 1.0 / (1.0 + jnp.exp(-x))


def _gelu_tanh(x):
    return 0.5 * x * (1.0 + jnp.tanh(math.sqrt(2.0 / math.pi) * (x + 0.044715 * (x * x * x))))


def _dot(a, b):
    return jnp.dot(a, b, preferred_element_type=F32)


def _mod_kernel(cond_ref, w_ref, b_ref, o_ref):
    cond = cond_ref[...]
    s = (cond * _sigmoid(cond)).astype(BF16)
    o_ref[0] = _dot(s, w_ref[0].astype(BF16)) + b_ref[0]


def _modulation(cond8, w_mod, b_mod):
    tn = 1536
    n_cols = N_MOD * D_MODEL
    return pl.pallas_call(
        _mod_kernel,
        grid=(DEPTH, n_cols // tn),
        in_specs=[
            pl.BlockSpec((8, D_MODEL), lambda l, j: (0, 0)),
            pl.BlockSpec((1, D_MODEL, tn), lambda l, j: (l, 0, j)),
            pl.BlockSpec((1, 1, tn), lambda l, j: (l, 0, j)),
        ],
        out_specs=pl.BlockSpec((1, 8, tn), lambda l, j: (l, 0, j)),
        out_shape=jax.ShapeDtypeStruct((DEPTH, 8, n_cols), F32),
        compiler_params=pltpu.CompilerParams(vmem_limit_bytes=VMEM_LIMIT),
        name="modulation",
    )(cond8, w_mod, b_mod.reshape(DEPTH, 1, n_cols))


def _write_heads(allh, extra, gain, partner, out_ref, rows=slice(None)):
    for h in range(MLA_HEADS):
        tile = slice(h * HEAD_PAD, (h + 1) * HEAD_PAD)
        t = allh[:, tile]
        if extra is not None:
            t = t + extra
        rs = lax.rsqrt(jnp.sum(t * t, axis=-1, keepdims=True) * (1.0 / MLA_QK) + EPS)
        o = t * gain
        if isinstance(partner, tuple):
            o = o + partner[0][:, tile] * partner[1]
        elif partner is not None:
            o = o + partner
        out_ref[h, rows, :] = (o * rs).astype(BF16)


def _write_values(vall, v_ref, rows=slice(None)):
    lane = lax.broadcasted_iota(jnp.int32, (1, HEAD_PAD), 1)
    ones = jnp.where(lane >= MLA_V, 1.0, 0.0)
    for h in range(MLA_HEADS):
        v_ref[h, rows, :] = (vall[:, h * HEAD_PAD:(h + 1) * HEAD_PAD] + ones).astype(BF16)


def _proj_kernel(xp_ref, xs_ref, mod_ref, cos_ref, sin_ref, w_in_ref, w_kr_ref, w_krp_ref, qan_ref, kvn_ref,
                 w_uq_ref, w_uqp_ref, w_uk_ref, w_uv_ref, qn_ref, qnp_ref, kn_ref, knp_ref,
                 u_ref, gm_ref, ckv_ref, kr_ref, q_ref, k_ref, v_ref):
    latent = pl.program_id(0) >= N_PROMPT // PROJ_TILE
    shift = mod_ref[0, 0:1, :]
    scale = mod_ref[0, 1:2, :]

    def body(x_ref, rope):
        for sub in range(PROJ_TILE // PROJ_SUB):
            rows = slice(sub * PROJ_SUB, (sub + 1) * PROJ_SUB)
            h = (_rms(x_ref[rows, :]) * (1.0 + scale) + shift).astype(BF16)
            z = _dot(h, w_in_ref[...])
            kr = _dot(h, w_kr_ref[...])
            u_ref[rows, :] = z[:, 0:A_WIDTH]
            gm_ref[rows, :] = z[:, A_WIDTH + Q_RANK + KV_RANK:].astype(gm_ref.dtype)
            ckv = _rms(z[:, A_WIDTH + Q_RANK:A_WIDTH + Q_RANK + KV_RANK]) * kvn_ref[...]
            cq = (_rms(z[:, A_WIDTH:A_WIDTH + Q_RANK]) * qan_ref[...]).astype(BF16)
            ckv_b = ckv.astype(BF16)
            qall = _dot(cq, w_uq_ref[...])
            kall = _dot(ckv_b, w_uk_ref[...])
            _write_values(_dot(ckv_b, w_uv_ref[...]), v_ref, rows)
            if rope:
                cos = cos_ref[rows, :]
                sin = sin_ref[rows, :]
                _write_heads(qall, None, qn_ref[...] * cos, (_dot(cq, w_uqp_ref[...]), qnp_ref[...] * sin),
                             q_ref, rows)
                _write_heads(kall, kr, kn_ref[...] * cos, _dot(h, w_krp_ref[...]) * (knp_ref[...] * sin),
                             k_ref, rows)
            else:
                ckv_ref[rows, :] = ckv
                kr_ref[rows, :] = kr
                _write_heads(qall, None, qn_ref[...], None, q_ref, rows)
                _write_heads(kall, kr, kn_ref[...], None, k_ref, rows)

    pl.when(latent)(lambda: body(xs_ref, True))
    pl.when(jnp.logical_not(latent))(lambda: body(xp_ref, False))


def _projection(x, mod, rope_tabs, w, l):
    t = PROJ_TILE
    n_tiles = N_TOK // t
    n_ctx = N_PROMPT // t
    tok = lambda width: pl.BlockSpec((t, width), lambda i: (i, 0))
    ctx = lambda width: pl.BlockSpec((t, width), lambda i: (jnp.minimum(i, n_ctx - 1), 0))
    heads = pl.BlockSpec((MLA_HEADS, t, HEAD_PAD), lambda i: (0, i, 0))
    pos = pl.BlockSpec((t, HEAD_PAD), lambda i: (jnp.maximum(i - n_ctx, 0) % (DEC_SEQ // t), 0))
    weights = [w[n] for n in ('w_in_a', 'w_in_kr', 'w_in_krp', 'q_a_norm', 'kv_a_norm', 'w_uq', 'w_uqp',
                              'w_uk', 'w_uv', 'q_norm', 'q_norm_p', 'k_norm', 'k_norm_p')]
    xs, off = _split_args(x, t)
    head_shape = jax.ShapeDtypeStruct((MLA_HEADS, N_TOK, HEAD_PAD), BF16)
    return pl.pallas_call(
        _proj_kernel,
        grid=(n_tiles,),
        in_specs=[*_split_specs(t, D_MODEL, off),
                  pl.BlockSpec((None, 1, N_MOD, D_MODEL), lambda i: (l, _mod_row(i, t), 0, 0)),
                  pos, pos] + [_layer_spec(a, l) for a in weights],
        out_specs=[tok(A_WIDTH), tok(2 * C_WIDTH), ctx(KV_RANK), ctx(HEAD_PAD), heads, heads, heads],
        out_shape=[
            jax.ShapeDtypeStruct((N_TOK, A_WIDTH), F32),
            jax.ShapeDtypeStruct((N_TOK, 2 * C_WIDTH), BF16),
            jax.ShapeDtypeStruct((N_PROMPT, KV_RANK), F32),
            jax.ShapeDtypeStruct((N_PROMPT, HEAD_PAD), F32),
            head_shape, head_shape, head_shape,
        ],
        compiler_params=pltpu.CompilerParams(vmem_limit_bytes=VMEM_LIMIT),
        name="projection",
    )(*xs, mod, *rope_tabs, *weights)


def _ctx_kv_kernel(ckv_ref, kr_ref, w_uk_ref, w_uv_ref, kn_ref, k_ref, v_ref):
    n = DEC_BATCH * PAST_LEN
    ckv_b = ckv_ref[...].reshape(n, KV_RANK).astype(BF16)
    _write_heads(_dot(ckv_b, w_uk_ref[...]), kr_ref[...].reshape(n, HEAD_PAD), kn_ref[...], None, k_ref)
    _write_values(_dot(ckv_b, w_uv_ref[...]), v_ref)


def _ctx_kv(cache_ckv, cache_kr, w, l):
    n = DEC_BATCH * PAST_LEN
    cache = pl.BlockSpec((DEC_BATCH, None, PAST_LEN, HEAD_PAD), lambda i: (0, l, 0, 0))
    weights = [w['w_uk'], w['w_uv'], w['k_norm']]
    heads = pl.BlockSpec((MLA_HEADS, n, HEAD_PAD), lambda i: (0, 0, 0))
    head_shape = jax.ShapeDtypeStruct((MLA_HEADS, n, HEAD_PAD), BF16)
    return pl.pallas_call(
        _ctx_kv_kernel,
        grid=(1,),
        in_specs=[cache, cache] + [_layer_spec(a, l) for a in weights],
        out_specs=[heads, heads],
        out_shape=[head_shape, head_shape],
        compiler_params=pltpu.CompilerParams(vmem_limit_bytes=VMEM_LIMIT),
        name="context_kv",
    )(cache_ckv, cache_kr, *weights)


def _ssm_block(d, k):
    return jnp.where((d == 1) & (k >= 2), k ^ 1, k)


def _ssm_kernel(u_ref, h0r_ref, h0i_ref, ar_ref, ai_ref, pwr_ref, pwi_ref, bm_ref, cr_ref, ci_ref,
                y_ref, hfr_ref, hfi_ref, sre, sim, hin_r, hin_i, car_r, car_i):
    d = pl.program_id(0)
    k = pl.program_id(1)
    rows = 256

    @pl.when(k == 0)
    def _():
        car_r[...] = jnp.zeros_like(car_r)
        car_i[...] = jnp.zeros_like(car_i)

    for c in range(SSM_UNIT // rows):
        sl = slice(c * rows, (c + 1) * rows)
        bu = _dot(u_ref[sl, :].astype(BF16), bm_ref[0])
        sre[sl, :] = bu[:, :SSM_STATE]
        sim[sl, :] = bu[:, SSM_STATE:]

    a_r = jnp.broadcast_to(ar_ref[0], (SSM_STREAMS, SSM_STATE))
    a_i = jnp.broadcast_to(ai_ref[0], (SSM_STREAMS, SSM_STATE))

    def stream_rows(tau):
        return pl.ds(pl.multiple_of(tau * SSM_STREAMS, SSM_STREAMS), SSM_STREAMS)

    def scan_step(i, carry):
        h_r, h_i = carry
        tau = jnp.where(d == 0, i, SSM_STEPS - 1 - i)
        n_r = a_r * h_r - a_i * h_i + sre[stream_rows(tau), :]
        n_i = a_r * h_i + a_i * h_r + sim[stream_rows(tau), :]
        sre[stream_rows(tau), :] = n_r
        sim[stream_rows(tau), :] = n_i
        return n_r, n_i

    zero = jnp.zeros((SSM_STREAMS, SSM_STATE), F32)
    end_r, end_i = lax.fori_loop(0, SSM_STEPS, scan_step, (zero, zero), unroll=8)

    @pl.when(k < 2)
    def _():
        hfr_ref[0] = end_r
        hfi_ref[0] = end_i

    @pl.when(k >= 2)
    def _():
        b = (k - 2) // 2
        first = (k % 2) == 0
        c_r = jnp.where(first, h0r_ref[0, pl.ds(b, 1), :], car_r[...])
        c_i = jnp.where(first, h0i_ref[0, pl.ds(b, 1), :], car_i[...])
        full_r = pwr_ref[0, SSM_STEPS - 1:SSM_STEPS, :]
        full_i = pwi_ref[0, SSM_STEPS - 1:SSM_STEPS, :]

        def chain(order):
            cr, ci = c_r, c_i
            for j in order:
                hin_r[j:j + 1, :] = cr
                hin_i[j:j + 1, :] = ci
                nr = end_r[j:j + 1, :] + full_r * cr - full_i * ci
                ni = end_i[j:j + 1, :] + full_r * ci + full_i * cr
                cr, ci = nr, ni
            car_r[...] = cr
            car_i[...] = ci

        @pl.when(d == 0)
        def _():
            chain(range(SSM_STREAMS))

        @pl.when(d == 1)
        def _():
            chain(range(SSM_STREAMS - 1, -1, -1))

        g_r = hin_r[...]
        g_i = hin_i[...]

        def fix_step(i, carry):
            p = jnp.where(d == 0, i, SSM_STEPS - 1 - i)
            p_r = jnp.broadcast_to(pwr_ref[0, pl.ds(p, 1), :], (SSM_STREAMS, SSM_STATE))
            p_i = jnp.broadcast_to(pwi_ref[0, pl.ds(p, 1), :], (SSM_STREAMS, SSM_STATE))
            sr = stream_rows(i)
            sre[sr, :] = sre[sr, :] + (p_r * g_r - p_i * g_i)
            sim[sr, :] = sim[sr, :] + (p_r * g_i + p_i * g_r)
            return carry

        lax.fori_loop(0, SSM_STEPS, fix_step, 0, unroll=4)

    for c in range(SSM_UNIT // rows):
        sl = slice(c * rows, (c + 1) * rows)
        y_ref[0, sl, :] = (_dot(sre[sl, :].astype(BF16), cr_ref[0])
                           + _dot(sim[sl, :].astype(BF16), ci_ref[0])).astype(y_ref.dtype)


def _step_major(x, inverse=False):
    lead = x.shape[:-2]
    a, b = (SSM_STEPS, SSM_STREAMS) if inverse else (SSM_STREAMS, SSM_STEPS)
    x = x.reshape(lead + (N_TOK // SSM_UNIT, a, b, x.shape[-1]))
    return jnp.swapaxes(x, -3, -2).reshape(lead + (N_TOK, x.shape[-1]))


def _ssm(u, h0r, h0i, tabs, l):
    n_units = N_TOK // SSM_UNIT
    u = _step_major(u)
    per_dir = lambda a: pl.BlockSpec((None, 1) + a.shape[2:], lambda d, k: (l, d) + (0,) * (a.ndim - 2))
    operands = [h0r, h0i] + [tabs[n] for n in ('a_r', 'a_i', 'pw_r', 'pw_i', 'b_blk', 'c_r', 'c_i')]
    y, hf_r, hf_i = pl.pallas_call(
        _ssm_kernel,
        grid=(N_DIR, n_units),
        in_specs=[pl.BlockSpec((SSM_UNIT, A_WIDTH), lambda d, k: (_ssm_block(d, k), 0))]
                 + [per_dir(a) for a in operands],
        out_specs=[pl.BlockSpec((1, SSM_UNIT, A_WIDTH), lambda d, k: (d, _ssm_block(d, k), 0)),
                   pl.BlockSpec((1, SSM_STREAMS, SSM_STATE), lambda d, k: (d, jnp.minimum(k, 1), 0)),
                   pl.BlockSpec((1, SSM_STREAMS, SSM_STATE), lambda d, k: (d, jnp.minimum(k, 1), 0))],
        out_shape=[jax.ShapeDtypeStruct((N_DIR, N_TOK, A_WIDTH), BF16),
                   jax.ShapeDtypeStruct((N_DIR, BATCH, SSM_STATE), F32),
                   jax.ShapeDtypeStruct((N_DIR, BATCH, SSM_STATE), F32)],
        scratch_shapes=[pltpu.VMEM((SSM_UNIT, SSM_STATE), F32), pltpu.VMEM((SSM_UNIT, SSM_STATE), F32),
                        pltpu.VMEM((SSM_STREAMS, SSM_STATE), F32), pltpu.VMEM((SSM_STREAMS, SSM_STATE), F32),
                        pltpu.VMEM((1, SSM_STATE), F32), pltpu.VMEM((1, SSM_STATE), F32)],
        compiler_params=pltpu.CompilerParams(vmem_limit_bytes=VMEM_LIMIT,
                                             dimension_semantics=("arbitrary", "arbitrary")),
        name="ssm_scan",
    )(u, *operands)
    return _step_major(y, inverse=True), hf_r, hf_i


def _attn_kernel(*refs, key_lens):
    n_src = len(key_lens)
    q_ref = refs[0]
    k_refs = refs[1:1 + n_src]
    v_refs = refs[1 + n_src:1 + 2 * n_src]
    o_ref = refs[1 + 2 * n_src]
    s_ref = refs[2 + 2 * n_src]
    chunks = []
    off = 0
    for si, n in enumerate(key_lens):
        for c0 in range(0, n, ATTN_KC):
            cl = min(ATTN_KC, n - c0)
            chunks.append((si, c0, cl, off))
            off += cl
    tq = q_ref.shape[1]
    row_max = []
    n_heads = q_ref.shape[0]
    for hh in range(n_heads):
        q = q_ref[hh]
        mp = jnp.full((tq, LANES), -jnp.inf, F32)
        for si, c0, cl, o in chunks:
            s = lax.dot_general(q, k_refs[si][hh, c0:c0 + cl, :], (((1,), (1,)), ((), ())),
                                preferred_element_type=F32)
            s_ref[hh, :, o:o + cl] = s
            for g in range(cl // LANES):
                mp = jnp.maximum(mp, s[:, g * LANES:(g + 1) * LANES])
        row_max.append(jnp.max(mp, axis=-1, keepdims=True))
    outs = []
    for hh in range(n_heads):
        acc = jnp.zeros((tq, LANES), F32)
        for si, c0, cl, o in chunks:
            p = jnp.exp2(s_ref[hh, :, o:o + cl] - row_max[hh])
            acc = acc + _dot(p.astype(BF16), v_refs[si][hh, c0:c0 + cl, :])
        outs.append(acc / pltpu.roll(acc, MLA_V, 1))
    lane = lax.broadcasted_iota(jnp.int32, (tq, LANES), 1)
    for pair in range(n_heads // 2):
        o_ref[:, pair * LANES:(pair + 1) * LANES] = jnp.where(
            lane < MLA_V, outs[2 * pair], pltpu.roll(outs[2 * pair + 1], MLA_V, 1)).astype(o_ref.dtype)


def _attention(q, ks, vs, n_seq, seq_len, tok0, key_lens, key_tok0):
    tq = min(ATTN_TQ, seq_len)
    n_qt = seq_len // tq
    q0 = tok0 // tq
    in_specs = [pl.BlockSpec((ATTN_HEADS, tq, HEAD_PAD), lambda b, hp, qi: (hp, q0 + b * n_qt + qi, 0))]
    for n, t0 in zip(key_lens, key_tok0):
        in_specs.append(pl.BlockSpec((ATTN_HEADS, n, HEAD_PAD), lambda b, hp, qi, n=n, t0=t0: (hp, t0 // n + b, 0)))
    for n, t0 in zip(key_lens, key_tok0):
        in_specs.append(pl.BlockSpec((ATTN_HEADS, n, HEAD_PAD), lambda b, hp, qi, n=n, t0=t0: (hp, t0 // n + b, 0)))
    return pl.pallas_call(
        functools.partial(_attn_kernel, key_lens=tuple(key_lens)),
        grid=(n_seq, MLA_HEADS // ATTN_HEADS, n_qt),
        in_specs=in_specs,
        out_specs=pl.BlockSpec((tq, ATTN_HEADS * MLA_V), lambda b, hp, qi: (b * n_qt + qi, hp)),
        out_shape=jax.ShapeDtypeStruct((n_seq * seq_len, B_WIDTH), BF16),
        scratch_shapes=[pltpu.VMEM((ATTN_HEADS, tq, sum(key_lens)), F32)],
        compiler_params=pltpu.CompilerParams(vmem_limit_bytes=VMEM_LIMIT,
                                             dimension_semantics=("arbitrary", "arbitrary", "arbitrary")),
        name="attention",
    )(q, *ks, *vs)


def _merge_kernel(xp_ref, xs_ref, mod_ref, yssm_ref, u_ref, ybp_ref, ybs_ref, gm_ref, dskip_ref, w_glu_ref,
                  vn_ref, w_s_ref, bias_ref, g_ref, w_out_ref, x1_ref, h2_ref):
    t = xp_ref.shape[0]
    latent = pl.program_id(0) >= N_PROMPT // t
    lane = lax.broadcasted_iota(jnp.int32, (CHUNK, C_WIDTH), 1)
    g = g_ref[...]

    def body(x_ref, yb_ref):
        for sub in range(t // PROJ_SUB):
            rows = slice(sub * PROJ_SUB, (sub + 1) * PROJ_SUB)
            y = (yssm_ref[0, rows, :].astype(F32) + yssm_ref[1, rows, :].astype(F32)
                 + dskip_ref[...] * u_ref[rows, :])
            y = _gelu_tanh(y)
            y_a = y * _sigmoid(_dot(y.astype(BF16), w_glu_ref[...]))
            gm = gm_ref[rows, :].astype(F32)
            vn = (_rms(gm[:, C_WIDTH:]) * vn_ref[...]).astype(BF16)
            mixed = []
            for c in range(PROJ_SUB // CHUNK):
                vc = vn[c * CHUNK:(c + 1) * CHUNK, :]
                m = _dot(w_s_ref[0], vc)
                for h in range(1, GMLP_HEADS):
                    m = jnp.where(lane >= h * GMLP_CH, _dot(w_s_ref[h], vc), m)
                mixed.append(m + bias_ref[...])
            y_c = gm[:, :C_WIDTH] * jnp.concatenate(mixed, axis=0)
            n_a = (_rms(y_a) * g[:, :A_WIDTH]).astype(BF16)
            n_b = (_rms(yb_ref[rows, :].astype(F32)) * g[:, A_WIDTH:A_WIDTH + B_WIDTH]).astype(BF16)
            n_c = (_rms(y_c) * g[:, A_WIDTH + B_WIDTH:]).astype(BF16)
            o = (_dot(n_a, w_out_ref[0:A_WIDTH, :]) + _dot(n_b, w_out_ref[A_WIDTH:A_WIDTH + B_WIDTH, :])
                 + _dot(n_c, w_out_ref[A_WIDTH + B_WIDTH:, :]))
            x1 = x_ref[rows, :] + mod_ref[0, 2:3, :] * o
            x1_ref[rows, :] = x1
            h2_ref[rows, :] = (_rms(x1) * (1.0 + mod_ref[0, 4:5, :]) + mod_ref[0, 3:4, :]).astype(BF16)

    pl.when(latent)(lambda: body(xs_ref, ybs_ref))
    pl.when(jnp.logical_not(latent))(lambda: body(xp_ref, ybp_ref))


def _merge(x, mod, yssm, u, yb_ctx, yb_lat, gm, w, l):
    t = PROJ_TILE
    tok = lambda width: pl.BlockSpec((t, width), lambda i: (i, 0))
    weights = [w[n] for n in ('ssm_d', 'w_glu', 'gmlp_v_norm', 'gmlp_w_s', 'gmlp_bias', 'w_out_norm', 'w_out')]
    xs, off = _split_args(x, t)
    return pl.pallas_call(
        _merge_kernel,
        grid=(N_TOK // t,),
        in_specs=[*_split_specs(t, D_MODEL, off),
                  pl.BlockSpec((None, 1, N_MOD, D_MODEL), lambda i: (l, _mod_row(i, t), 0, 0)),
                  pl.BlockSpec((N_DIR, t, A_WIDTH), lambda i: (0, i, 0)),
                  tok(A_WIDTH), *_split_specs(t, B_WIDTH, 0), tok(2 * C_WIDTH)]
                 + [_layer_spec(a, l) for a in weights],
        out_specs=[tok(D_MODEL), tok(D_MODEL)],
        out_shape=[jax.ShapeDtypeStruct((N_TOK, D_MODEL), F32),
                   jax.ShapeDtypeStruct((N_TOK, D_MODEL), BF16)],
        compiler_params=pltpu.CompilerParams(vmem_limit_bytes=VMEM_LIMIT),
        name="merge",
    )(*xs, mod, yssm, u, yb_ctx, yb_lat, gm, *weights)


def _ffn_kernel(x1_ref, mod_ref, hp_ref, hm_ref, hn_ref, wup_ref, cw_ref, cb_ref, wd_ref, *rest):
    out_refs, a_scr = rest[:-1], rest[-1]
    first_seg = pl.program_id(0) * FFN_SEGS - N_PROMPT // FFN_ROWS
    per_seq = DEC_SEQ // FFN_ROWS
    edge_row = lax.broadcasted_iota(jnp.int32, (FFN_HALO, D_MODEL), 0)
    tok_row = lax.broadcasted_iota(jnp.int32, (FFN_ROWS, FFN_FC), 0)

    def conv(u, c0):
        cols = slice(c0, c0 + FFN_FC)
        cur = u[:FFN_ROWS]
        prev = jnp.where(tok_row == 0, u[FFN_ROWS:FFN_ROWS + 1], pltpu.roll(cur, 1, 0))
        nxt = jnp.where(tok_row == FFN_ROWS - 1, u[FFN_ROWS + 1:FFN_ROWS + 2], pltpu.roll(cur, FFN_ROWS - 1, 0))
        return prev * cw_ref[0:1, cols] + cur * cw_ref[1:2, cols] + nxt * cw_ref[2:3, cols] + cb_ref[:, cols]

    for s in range(FFN_SEGS):
        seg = first_seg + s
        starts = (seg < 0) | ((seg & (per_seq - 1)) == 0)
        ends = (seg < 0) | ((seg & (per_seq - 1)) == per_seq - 1)
        rows = slice(s * FFN_ROWS, (s + 1) * FFN_ROWS)
        before = hp_ref[...] if s == 0 else hm_ref[s * FFN_ROWS - FFN_HALO:s * FFN_ROWS, :]
        after = hn_ref[...] if s == FFN_SEGS - 1 else hm_ref[(s + 1) * FFN_ROWS:(s + 1) * FFN_ROWS + FFN_HALO, :]
        prev_row = before.astype(F32)[FFN_HALO - 1:FFN_HALO, :] * jnp.where(starts, 0.0, 1.0)
        next_row = after.astype(F32)[0:1, :] * jnp.where(ends, 0.0, 1.0)
        edge = jnp.where(edge_row == 0, prev_row, jnp.where(edge_row == 1, next_row, 0.0)).astype(BF16)
        he = jnp.concatenate([hm_ref[rows, :], edge], axis=0)
        for j in range(D_FF // FFN_FC):
            cg = j * FFN_FC
            cv = D_FF + j * FFN_FC
            gate = conv(_dot(he, wup_ref[:, cg:cg + FFN_FC]), cg)
            val = conv(_dot(he, wup_ref[:, cv:cv + FFN_FC]), cv)
            a_scr[rows, cg:cg + FFN_FC] = (gate * _sigmoid(gate) * val).astype(BF16)

    out = x1_ref[...] + mod_ref[0, 5:6, :] * _dot(a_scr[...], wd_ref[...])
    if len(out_refs) == 1:
        out_refs[0][...] = out
    else:
        @pl.when(first_seg < 0)
        def _():
            out_refs[0][...] = out

        @pl.when(first_seg >= 0)
        def _():
            out_refs[1][...] = out


def _ffn(x1, mod, h2, w, l, split_out):
    t = FFN_ROWS * FFN_SEGS
    hb = t // FFN_HALO
    last_halo = N_TOK // FFN_HALO - 1
    weights = [w[n] for n in ('ffn_w_up', 'ffn_conv_w', 'ffn_conv_b', 'ffn_w_down')]
    if split_out:
        out_specs = list(_split_specs(t, D_MODEL, 0))
        out_shape = [jax.ShapeDtypeStruct((N_PROMPT, D_MODEL), F32), jax.ShapeDtypeStruct((N_SAMPLE, D_MODEL), F32)]
    else:
        out_specs = pl.BlockSpec((t, D_MODEL), lambda i: (i, 0))
        out_shape = jax.ShapeDtypeStruct((N_TOK, D_MODEL), F32)
    return pl.pallas_call(
        _ffn_kernel,
        grid=(N_TOK // t,),
        in_specs=[pl.BlockSpec((t, D_MODEL), lambda i: (i, 0)),
                  pl.BlockSpec((None, 1, N_MOD, D_MODEL), lambda i: (l, _mod_row(i, t), 0, 0)),
                  pl.BlockSpec((FFN_HALO, D_MODEL), lambda i: (jnp.maximum(i * hb - 1, 0), 0)),
                  pl.BlockSpec((t, D_MODEL), lambda i: (i, 0)),
                  pl.BlockSpec((FFN_HALO, D_MODEL), lambda i: (jnp.minimum((i + 1) * hb, last_halo), 0))]
                 + [_layer_spec(a, l, pipeline_mode=pl.Buffered(1)) for a in weights],
        out_specs=out_specs,
        out_shape=out_shape,
        scratch_shapes=[pltpu.VMEM((t, D_FF), BF16)],
        compiler_params=pltpu.CompilerParams(vmem_limit_bytes=VMEM_LIMIT,
                                             dimension_semantics=("arbitrary",)),
        name="conv_ffn",
    )(x1, mod, h2, h2, h2, *weights)


def _pad_heads(w, head_w, used, offset=0):
    lead = w.shape[:-1]
    w = w.reshape(lead + (MLA_HEADS, head_w))[..., :used]
    pad = [(0, 0)] * (len(lead) + 1) + [(offset, HEAD_PAD - used - offset)]
    return jnp.pad(w, pad).reshape(lead + (MLA_HEADS * HEAD_PAD,))


def _head_gain(g):
    return jnp.pad(g, ((0, 0), (0, HEAD_PAD - MLA_QK)))[:, None, :]


def _ssm_tables(a_re, a_im, b_re, b_im, c_re, c_im, log_dt):
    a = lax.complex(a_re, a_im)
    dt = jnp.exp(log_dt)[..., None]
    a_bar = jnp.exp(a * dt)
    b_bar = ((a_bar - 1.0) / a)[..., None] * lax.complex(b_re, b_im)
    steps = jnp.arange(1, SSM_STEPS + 1, dtype=F32)[None, None, :, None, None]
    pw = jnp.exp((a * dt)[:, :, None] * steps)
    eye = jnp.eye(SSM_G, dtype=F32)
    blk_b = lambda m: jnp.einsum('ldgpc,gh->ldgchp', m, eye).reshape(DEPTH, N_DIR, A_WIDTH, SSM_STATE)
    blk_c = lambda m: jnp.einsum('ldgcp,gh->ldgphc', m, eye).reshape(DEPTH, N_DIR, SSM_STATE, A_WIDTH)
    flat = lambda m: m.reshape(DEPTH, N_DIR, 1, SSM_STATE)
    return {
        'a_r': flat(a_bar.real), 'a_i': flat(a_bar.imag),
        'pw_r': pw.real.reshape(DEPTH, N_DIR, SSM_STEPS, SSM_STATE),
        'pw_i': pw.imag.reshape(DEPTH, N_DIR, SSM_STEPS, SSM_STATE),
        'b_blk': jnp.concatenate([blk_b(b_bar.real), blk_b(b_bar.imag)], axis=-1).astype(BF16),
        'c_r': blk_c(c_re).astype(BF16),
        'c_i': blk_c(-c_im).astype(BF16),
    }


def _rope_tables():
    rows = DEC_SEQ // GRID_W
    row = np.repeat(np.arange(rows, dtype=np.float32), GRID_W)
    col = np.tile(np.arange(GRID_W, dtype=np.float32), rows)
    n_freq = MLA_ROPE // 4
    inv = (np.float32(ROPE_BASE) ** (-np.arange(n_freq, dtype=np.float32) / np.float32(n_freq))).astype(np.float32)
    ang = np.concatenate([row[:, None] * inv, col[:, None] * inv], axis=-1).astype(np.float32)
    cos, sin = np.cos(ang), np.sin(ang)
    cos_t = np.ones((DEC_SEQ, HEAD_PAD), np.float32)
    sin_t = np.zeros((DEC_SEQ, HEAD_PAD), np.float32)
    cos_t[:, ROPE_LANE0:ROPE_LANE0 + MLA_ROPE] = np.concatenate([cos, cos], axis=-1)
    sin_t[:, ROPE_LANE0:ROPE_LANE0 + MLA_ROPE] = np.concatenate([sin, sin], axis=-1)
    return jnp.asarray(cos_t), jnp.asarray(sin_t)


def _rope_partner(r):
    half = MLA_ROPE // 2
    return jnp.concatenate([-r[..., half:], r[..., :half]], axis=-1)


def _rope_lanes(r):
    return jnp.pad(r, [(0, 0)] * (r.ndim - 1) + [(ROPE_LANE0, HEAD_PAD - ROPE_LANE0 - MLA_ROPE)])


def _head_gain_partner(g):
    half = MLA_ROPE // 2
    r = g[:, MLA_NOPE:]
    return _rope_lanes(jnp.concatenate([r[:, half:], r[:, :half]], axis=-1))[:, None, :]


def _prepare_weights(w_in, ssm_d, ssm_w_glu, q_a_norm, kv_a_norm, w_uq, w_ukv, q_norm, k_norm,
                     gmlp_v_norm, gmlp_w_s, gmlp_b_s, w_out_norm, w_out, ffn_w_up, ffn_conv_w,
                     ffn_conv_b, ffn_w_down):
    w_kr = w_in[:, :, OFF_KR:OFF_GM]
    ukv = w_ukv.reshape(DEPTH, KV_RANK, MLA_HEADS, MLA_NOPE + MLA_V)
    uq = w_uq.reshape(DEPTH, Q_RANK, MLA_HEADS, MLA_QK)
    q_gain = q_norm * Q_SCALE
    row = lambda a: a[:, None, :]
    return {
        'w_in_a': jnp.concatenate([w_in[:, :, OFF_SSM:OFF_KR], w_in[:, :, OFF_GM:]], axis=2).astype(BF16),
        'w_in_kr': _rope_lanes(w_kr).astype(BF16),
        'w_in_krp': _rope_lanes(_rope_partner(w_kr)).astype(BF16),
        'w_uqp': _rope_lanes(_rope_partner(uq[..., MLA_NOPE:])).reshape(DEPTH, Q_RANK, -1).astype(BF16),
        'q_a_norm': row(q_a_norm), 'kv_a_norm': row(kv_a_norm),
        'w_uq': _pad_heads(w_uq, MLA_QK, MLA_QK).astype(BF16),
        'w_uk': _pad_heads(ukv[..., :MLA_NOPE].reshape(DEPTH, KV_RANK, -1), MLA_NOPE, MLA_NOPE).astype(BF16),
        'w_uv': _pad_heads(ukv[..., MLA_NOPE:].reshape(DEPTH, KV_RANK, -1), MLA_V, MLA_V).astype(BF16),
        'q_norm': _head_gain(q_gain), 'q_norm_p': _head_gain_partner(q_gain),
        'k_norm': _head_gain(k_norm), 'k_norm_p': _head_gain_partner(k_norm),
        'ssm_d': row(ssm_d), 'w_glu': ssm_w_glu.astype(BF16),
        'gmlp_v_norm': row(gmlp_v_norm), 'gmlp_w_s': gmlp_w_s.astype(BF16),
        'gmlp_bias': jnp.repeat(jnp.swapaxes(gmlp_b_s, 1, 2), GMLP_CH, axis=2),
        'w_out_norm': row(w_out_norm), 'w_out': w_out.astype(BF16),
        'ffn_w_up': ffn_w_up.astype(BF16), 'ffn_conv_w': ffn_conv_w,
        'ffn_conv_b': row(ffn_conv_b), 'ffn_w_down': ffn_w_down.astype(BF16),
    }


def kernel(x_prompt, x_sample, cache_ckv, cache_krope, state_ssm_re, state_ssm_im, c, c_ctx, w_mod, b_mod, w_in, ssm_a_re, ssm_a_im, ssm_b_re, ssm_b_im, ssm_c_re, ssm_c_im, ssm_log_dt, ssm_d, ssm_w_glu, q_a_norm, kv_a_norm, w_uq, w_ukv, q_norm, k_norm, gmlp_v_norm, gmlp_w_s, gmlp_b_s, w_out_norm, w_out, ffn_w_up, ffn_conv_w, ffn_conv_b, ffn_w_down):
    cond8 = jnp.concatenate([c_ctx[None, :], c, jnp.zeros((8 - 1 - DEC_BATCH, D_MODEL), F32)], axis=0)
    mod = _modulation(cond8, w_mod, b_mod).reshape(DEPTH, 8, N_MOD, D_MODEL)
    rope_tabs = _rope_tables()
    w = _prepare_weights(w_in, ssm_d, ssm_w_glu, q_a_norm, kv_a_norm, w_uq, w_ukv, q_norm, k_norm,
                         gmlp_v_norm, gmlp_w_s, gmlp_b_s, w_out_norm, w_out, ffn_w_up, ffn_conv_w,
                         ffn_conv_b, ffn_w_down)
    tabs = _ssm_tables(ssm_a_re, ssm_a_im, ssm_b_re, ssm_b_im, ssm_c_re, ssm_c_im, ssm_log_dt)
    cache_kr = _rope_lanes(cache_krope)

    def h0(st):
        st = jnp.transpose(st.reshape(DEC_BATCH, DEPTH, N_DIR, SSM_STATE), (1, 2, 0, 3))
        return jnp.pad(st, ((0, 0), (0, 0), (0, SSM_STREAMS - DEC_BATCH), (0, 0)))

    h0r, h0i = h0(state_ssm_re), h0(state_ssm_im)

    x = (x_prompt.reshape(N_PROMPT, D_MODEL), x_sample.reshape(N_SAMPLE, D_MODEL))
    ckv_out, kr_out, hre_out, him_out = [], [], [], []
    for l in range(DEPTH):
        u, gm, ckv, kr, q, k, v = _projection(x, mod, rope_tabs, w, l)
        k_c, v_c = _ctx_kv(cache_ckv, cache_kr, w, l)
        yb_ctx = _attention(q, [k], [v], BATCH, SEQ, 0, [SEQ], [0])
        yssm, hf_r, hf_i = _ssm(u, h0r, h0i, tabs, l)
        yb_lat = _attention(q, [k_c, k], [v_c, v], DEC_BATCH, DEC_SEQ, N_PROMPT,
                            [PAST_LEN, DEC_SEQ], [0, N_PROMPT])
        x1, h2 = _merge(x, mod, yssm, u, yb_ctx, yb_lat, gm, w, l)
        x = _ffn(x1, mod, h2, w, l, split_out=(l == DEPTH - 1))

        ckv_out.append(ckv.reshape(BATCH, SEQ, KV_RANK))
        kr_out.append(kr[:, ROPE_LANE0:ROPE_LANE0 + MLA_ROPE].reshape(BATCH, SEQ, MLA_ROPE))
        to_state = lambda st: jnp.transpose(st, (1, 0, 2)).reshape(BATCH, N_DIR, SSM_G, SSM_P)
        hre_out.append(to_state(hf_r))
        him_out.append(to_state(hf_i))

    y_prompt = x[0].reshape(BATCH, SEQ, D_MODEL)
    y_sample = x[1].reshape(DEC_BATCH, DEC_SEQ, D_MODEL)
    return (y_prompt, y_sample, jnp.stack(ckv_out, axis=1), jnp.stack(kr_out, axis=1),
            jnp.stack(hre_out, axis=1), jnp.stack(him_out, axis=1))
```

```python
import functools
import math

import jax
import jax.numpy as jnp
import numpy as np
from jax import lax
from jax.experimental import pallas as pl
from jax.experimental.pallas import tpu as pltpu

F32 = jnp.float32
BF16 = jnp.bfloat16

D_MODEL = 1024
BATCH = 16
SEQ = 256
DEPTH = 2
DEC_BATCH = 2
DEC_SEQ = 4096
PAST_LEN = 256
GRID_W = 64
EPS = 1e-6
N_MOD = 6
A_WIDTH = 256
B_WIDTH = 512
C_WIDTH = 256
SSM_CG = 16
SSM_G = 16
SSM_P = 64
N_DIR = 2
MLA_HEADS = 8
MLA_NOPE = 64
MLA_ROPE = 32
MLA_V = 64
MLA_QK = 96
Q_RANK = 256
KV_RANK = 128
ROPE_BASE = 10000.0
ATTN_SCALE = 1.0 / math.sqrt(MLA_QK)
Q_SCALE = ATTN_SCALE * math.log2(math.e)
GMLP_HEADS = 4
GMLP_CH = 64
CHUNK = 128
OFF_SSM = 0
OFF_Q = OFF_SSM + A_WIDTH
OFF_KV = OFF_Q + Q_RANK
OFF_KR = OFF_KV + KV_RANK
OFF_GM = OFF_KR + MLA_ROPE
D_FF = 2816

N_PROMPT = BATCH * SEQ
N_SAMPLE = DEC_BATCH * DEC_SEQ
N_TOK = N_PROMPT + N_SAMPLE
SSM_STATE = SSM_G * SSM_P

LANES = 128
HEAD_PAD = LANES
ROPE_LANE0 = MLA_NOPE

PROJ_TILE = 512
PROJ_SUB = 256
SSM_STREAMS = 8
SSM_STEPS = 256
SSM_UNIT = SSM_STREAMS * SSM_STEPS
ATTN_TQ = 256
ATTN_KC = 1024
ATTN_HEADS = 4
FFN_ROWS = SEQ
FFN_SEGS = 2
FFN_FC = 256
FFN_HALO = 16
VMEM_LIMIT = 48 * 1024 * 1024


def _mod_row(tile, tile_tokens):
    start = tile * tile_tokens
    return jnp.where(start < N_PROMPT, 0, 1 + (start - N_PROMPT) // DEC_SEQ)


def _layer_spec(a, l, **kw):
    return pl.BlockSpec((None,) + a.shape[1:], lambda *_: (l,) + (0,) * (a.ndim - 1), **kw)


def _split_specs(t, width, second_offset):
    n_a = N_PROMPT // t
    return (pl.BlockSpec((t, width), lambda i: (jnp.minimum(i, n_a - 1), 0)),
            pl.BlockSpec((t, width), lambda i: (jnp.maximum(i - n_a, 0) + second_offset, 0)))


def _split_args(x, t):
    if isinstance(x, tuple):
        return x, 0
    return (x, x), N_PROMPT // t


def _rms(x):
    return x * lax.rsqrt(jnp.mean(x * x, axis=-1, keepdims=True) + EPS)


def _sigmoid(x):
    return 1.0 / (1.0 + jnp.exp(-x))


def _gelu_tanh(x):
    return 0.5 * x * (1.0 + jnp.tanh(math.sqrt(2.0 / math.pi) * (x + 0.044715 * (x * x * x))))


def _dot(a, b):
    return jnp.dot(a, b, preferred_element_type=F32)


def _mod_kernel(cond_ref, w_ref, b_ref, o_ref):
    cond = cond_ref[...]
    s = (cond * _sigmoid(cond)).astype(BF16)
    o_ref[0] = _dot(s, w_ref[0].astype(BF16)) + b_ref[0]


def _modulation(cond8, w_mod, b_mod):
    tn = 1536
    n_cols = N_MOD * D_MODEL
    return pl.pallas_call(
        _mod_kernel,
        grid=(DEPTH, n_cols // tn),
        in_specs=[
            pl.BlockSpec((8, D_MODEL), lambda l, j: (0, 0)),
            pl.BlockSpec((1, D_MODEL, tn), lambda l, j: (l, 0, j)),
            pl.BlockSpec((1, 1, tn), lambda l, j: (l, 0, j)),
        ],
        out_specs=pl.BlockSpec((1, 8, tn), lambda l, j: (l, 0, j)),
        out_shape=jax.ShapeDtypeStruct((DEPTH, 8, n_cols), F32),
        compiler_params=pltpu.CompilerParams(vmem_limit_bytes=VMEM_LIMIT),
        name="modulation",
    )(cond8, w_mod, b_mod.reshape(DEPTH, 1, n_cols))


def _write_heads(allh, extra, gain, partner, out_ref, rows=slice(None)):
    for h in range(MLA_HEADS):
        tile = slice(h * HEAD_PAD, (h + 1) * HEAD_PAD)
        t = allh[:, tile]
        if extra is not None:
            t = t + extra
        rs = lax.rsqrt(jnp.sum(t * t, axis=-1, keepdims=True) * (1.0 / MLA_QK) + EPS)
        o = t * gain
        if isinstance(partner, tuple):
            o = o + partner[0][:, tile] * partner[1]
        elif partner is not None:
            o = o + partner
        out_ref[h, rows, :] = (o * rs).astype(BF16)


def _write_values(vall, v_ref, rows=slice(None)):
    lane = lax.broadcasted_iota(jnp.int32, (1, HEAD_PAD), 1)
    ones = jnp.where(lane >= MLA_V, 1.0, 0.0)
    for h in range(MLA_HEADS):
        v_ref[h, rows, :] = (vall[:, h * HEAD_PAD:(h + 1) * HEAD_PAD] + ones).astype(BF16)


def _proj_kernel(xp_ref, xs_ref, mod_ref, cos_ref, sin_ref, w_in_ref, w_kr_ref, w_krp_ref, qan_ref, kvn_ref,
                 w_uq_ref, w_uqp_ref, w_uk_ref, w_uv_ref, qn_ref, qnp_ref, kn_ref, knp_ref,
                 u_ref, gm_ref, ckv_ref, kr_ref, q_ref, k_ref, v_ref):
    latent = pl.program_id(0) >= N_PROMPT // PROJ_TILE
    shift = mod_ref[0, 0:1, :]
    scale = mod_ref[0, 1:2, :]

    def body(x_ref, rope):
        for sub in range(PROJ_TILE // PROJ_SUB):
            rows = slice(sub * PROJ_SUB, (sub + 1) * PROJ_SUB)
            h = (_rms(x_ref[rows, :]) * (1.0 + scale) + shift).astype(BF16)
            z = _dot(h, w_in_ref[...])
            kr = _dot(h, w_kr_ref[...])
            u_ref[rows, :] = z[:, 0:A_WIDTH]
            gm_ref[rows, :] = z[:, A_WIDTH + Q_RANK + KV_RANK:].astype(gm_ref.dtype)
            ckv = _rms(z[:, A_WIDTH + Q_RANK:A_WIDTH + Q_RANK + KV_RANK]) * kvn_ref[...]
            cq = (_rms(z[:, A_WIDTH:A_WIDTH + Q_RANK]) * qan_ref[...]).astype(BF16)
            ckv_b = ckv.astype(BF16)
            qall = _dot(cq, w_uq_ref[...])
            kall = _dot(ckv_b, w_uk_ref[...])
            _write_values(_dot(ckv_b, w_uv_ref[...]), v_ref, rows)
            if rope:
                cos = cos_ref[rows, :]
                sin = sin_ref[rows, :]
                _write_heads(qall, None, qn_ref[...] * cos, (_dot(cq, w_uqp_ref[...]), qnp_ref[...] * sin),
                             q_ref, rows)
                _write_heads(kall, kr, kn_ref[...] * cos, _dot(h, w_krp_ref[...]) * (knp_ref[...] * sin),
                             k_ref, rows)
            else:
                ckv_ref[rows, :] = ckv
                kr_ref[rows, :] = kr
                _write_heads(qall, None, qn_ref[...], None, q_ref, rows)
                _write_heads(kall, kr, kn_ref[...], None, k_ref, rows)

    pl.when(latent)(lambda: body(xs_ref, True))
    pl.when(jnp.logical_not(latent))(lambda: body(xp_ref, False))


def _projection(x, mod, rope_tabs, w, l):
    t = PROJ_TILE
    n_tiles = N_TOK // t
    n_ctx = N_PROMPT // t
    tok = lambda width: pl.BlockSpec((t, width), lambda i: (i, 0))
    ctx = lambda width: pl.BlockSpec((t, width), lambda i: (jnp.minimum(i, n_ctx - 1), 0))
    heads = pl.BlockSpec((MLA_HEADS, t, HEAD_PAD), lambda i: (0, i, 0))
    pos = pl.BlockSpec((t, HEAD_PAD), lambda i: (jnp.maximum(i - n_ctx, 0) % (DEC_SEQ // t), 0))
    weights = [w[n] for n in ('w_in_a', 'w_in_kr', 'w_in_krp', 'q_a_norm', 'kv_a_norm', 'w_uq', 'w_uqp',
                              'w_uk', 'w_uv', 'q_norm', 'q_norm_p', 'k_norm', 'k_norm_p')]
    xs, off = _split_args(x, t)
    head_shape = jax.ShapeDtypeStruct((MLA_HEADS, N_TOK, HEAD_PAD), BF16)
    return pl.pallas_call(
        _proj_kernel,
        grid=(n_tiles,),
        in_specs=[*_split_specs(t, D_MODEL, off),
                  pl.BlockSpec((None, 1, N_MOD, D_MODEL), lambda i: (l, _mod_row(i, t), 0, 0)),
                  pos, pos] + [_layer_spec(a, l) for a in weights],
        out_specs=[tok(A_WIDTH), tok(2 * C_WIDTH), ctx(KV_RANK), ctx(HEAD_PAD), heads, heads, heads],
        out_shape=[
            jax.ShapeDtypeStruct((N_TOK, A_WIDTH), F32),
            jax.ShapeDtypeStruct((N_TOK, 2 * C_WIDTH), BF16),
            jax.ShapeDtypeStruct((N_PROMPT, KV_RANK), F32),
            jax.ShapeDtypeStruct((N_PROMPT, HEAD_PAD), F32),
            head_shape, head_shape, head_shape,
        ],
        compiler_params=pltpu.CompilerParams(vmem_limit_bytes=VMEM_LIMIT),
        name="projection",
    )(*xs, mod, *rope_tabs, *weights)


def _ctx_kv_kernel(ckv_ref, kr_ref, w_uk_ref, w_uv_ref, kn_ref, k_ref, v_ref):
    n = DEC_BATCH * PAST_LEN
    ckv_b = ckv_ref[...].reshape(n, KV_RANK).astype(BF16)
    _write_heads(_dot(ckv_b, w_uk_ref[...]), kr_ref[...].reshape(n, HEAD_PAD), kn_ref[...], None, k_ref)
    _write_values(_dot(ckv_b, w_uv_ref[...]), v_ref)


def _ctx_kv(cache_ckv, cache_kr, w, l):
    n = DEC_BATCH * PAST_LEN
    cache = pl.BlockSpec((DEC_BATCH, None, PAST_LEN, HEAD_PAD), lambda i: (0, l, 0, 0))
    weights = [w['w_uk'], w['w_uv'], w['k_norm']]
    heads = pl.BlockSpec((MLA_HEADS, n, HEAD_PAD), lambda i: (0, 0, 0))
    head_shape = jax.ShapeDtypeStruct((MLA_HEADS, n, HEAD_PAD), BF16)
    return pl.pallas_call(
        _ctx_kv_kernel,
        grid=(1,),
        in_specs=[cache, cache] + [_layer_spec(a, l) for a in weights],
        out_specs=[heads, heads],
        out_shape=[head_shape, head_shape],
        compiler_params=pltpu.CompilerParams(vmem_limit_bytes=VMEM_LIMIT),
        name="context_kv",
    )(cache_ckv, cache_kr, *weights)


def _ssm_block(d, k):
    return jnp.where((d == 1) & (k >= 2), k ^ 1, k)


def _ssm_kernel(u_ref, h0r_ref, h0i_ref, ar_ref, ai_ref, pwr_ref, pwi_ref, bm_ref, cr_ref, ci_ref,
                y_ref, hfr_ref, hfi_ref, sre, sim, hin_r, hin_i, car_r, car_i):
    d = pl.program_id(0)
    k = pl.program_id(1)
    rows = 256

    @pl.when(k == 0)
    def _():
        car_r[...] = jnp.zeros_like(car_r)
        car_i[...] = jnp.zeros_like(car_i)

    for c in range(SSM_UNIT // rows):
        sl = slice(c * rows, (c + 1) * rows)
        bu = _dot(u_ref[sl, :].astype(BF16), bm_ref[0])
        sre[sl, :] = bu[:, :SSM_STATE]
        sim[sl, :] = bu[:, SSM_STATE:]

    a_r = jnp.broadcast_to(ar_ref[0], (SSM_STREAMS, SSM_STATE))
    a_i = jnp.broadcast_to(ai_ref[0], (SSM_STREAMS, SSM_STATE))

    def stream_rows(tau):
        return pl.ds(pl.multiple_of(tau * SSM_STREAMS, SSM_STREAMS), SSM_STREAMS)

    def scan_step(i, carry):
        h_r, h_i = carry
        tau = jnp.where(d == 0, i, SSM_STEPS - 1 - i)
        n_r = a_r * h_r - a_i * h_i + sre[stream_rows(tau), :]
        n_i = a_r * h_i + a_i * h_r + sim[stream_rows(tau), :]
        sre[stream_rows(tau), :] = n_r
        sim[stream_rows(tau), :] = n_i
        return n_r, n_i

    zero = jnp.zeros((SSM_STREAMS, SSM_STATE), F32)
    end_r, end_i = lax.fori_loop(0, SSM_STEPS, scan_step, (zero, zero), unroll=8)

    @pl.when(k < 2)
    def _():
        hfr_ref[0] = end_r
        hfi_ref[0] = end_i

    @pl.when(k >= 2)
    def _():
        b = (k - 2) // 2
        first = (k % 2) == 0
        c_r = jnp.where(first, h0r_ref[0, pl.ds(b, 1), :], car_r[...])
        c_i = jnp.where(first, h0i_ref[0, pl.ds(b, 1), :], car_i[...])
        full_r = pwr_ref[0, SSM_STEPS - 1:SSM_STEPS, :]
        full_i = pwi_ref[0, SSM_STEPS - 1:SSM_STEPS, :]

        def chain(order):
            cr, ci = c_r, c_i
            for j in order:
                hin_r[j:j + 1, :] = cr
                hin_i[j:j + 1, :] = ci
                nr = end_r[j:j + 1, :] + full_r * cr - full_i * ci
                ni = end_i[j:j + 1, :] + full_r * ci + full_i * cr
                cr, ci = nr, ni
            car_r[...] = cr
            car_i[...] = ci

        @pl.when(d == 0)
        def _():
            chain(range(SSM_STREAMS))

        @pl.when(d == 1)
        def _():
            chain(range(SSM_STREAMS - 1, -1, -1))

        g_r = hin_r[...]
        g_i = hin_i[...]

        def fix_step(i, carry):
            p = jnp.where(d == 0, i, SSM_STEPS - 1 - i)
            p_r = jnp.broadcast_to(pwr_ref[0, pl.ds(p, 1), :], (SSM_STREAMS, SSM_STATE))
            p_i = jnp.broadcast_to(pwi_ref[0, pl.ds(p, 1), :], (SSM_STREAMS, SSM_STATE))
            sr = stream_rows(i)
            sre[sr, :] = sre[sr, :] + (p_r * g_r - p_i * g_i)
            sim[sr, :] = sim[sr, :] + (p_r * g_i + p_i * g_r)
            return carry

        lax.fori_loop(0, SSM_STEPS, fix_step, 0, unroll=4)

    for c in range(SSM_UNIT // rows):
        sl = slice(c * rows, (c + 1) * rows)
        y_ref[0, sl, :] = (_dot(sre[sl, :].astype(BF16), cr_ref[0])
                           + _dot(sim[sl, :].astype(BF16), ci_ref[0])).astype(y_ref.dtype)


def _step_major(x, inverse=False):
    lead = x.shape[:-2]
    a, b = (SSM_STEPS, SSM_STREAMS) if inverse else (SSM_STREAMS, SSM_STEPS)
    x = x.reshape(lead + (N_TOK // SSM_UNIT, a, b, x.shape[-1]))
    return jnp.swapaxes(x, -3, -2).reshape(lead + (N_TOK, x.shape[-1]))


def _ssm(u, h0r, h0i, tabs, l):
    n_units = N_TOK // SSM_UNIT
    u = _step_major(u)
    per_dir = lambda a: pl.BlockSpec((None, 1) + a.shape[2:], lambda d, k: (l, d) + (0,) * (a.ndim - 2))
    operands = [h0r, h0i] + [tabs[n] for n in ('a_r', 'a_i', 'pw_r', 'pw_i', 'b_blk', 'c_r', 'c_i')]
    y, hf_r, hf_i = pl.pallas_call(
        _ssm_kernel,
        grid=(N_DIR, n_units),
        in_specs=[pl.BlockSpec((SSM_UNIT, A_WIDTH), lambda d, k: (_ssm_block(d, k), 0))]
                 + [per_dir(a) for a in operands],
        out_specs=[pl.BlockSpec((1, SSM_UNIT, A_WIDTH), lambda d, k: (d, _ssm_block(d, k), 0)),
                   pl.BlockSpec((1, SSM_STREAMS, SSM_STATE), lambda d, k: (d, jnp.minimum(k, 1), 0)),
                   pl.BlockSpec((1, SSM_STREAMS, SSM_STATE), lambda d, k: (d, jnp.minimum(k, 1), 0))],
        out_shape=[jax.ShapeDtypeStruct((N_DIR, N_TOK, A_WIDTH), BF16),
                   jax.ShapeDtypeStruct((N_DIR, BATCH, SSM_STATE), F32),
                   jax.ShapeDtypeStruct((N_DIR, BATCH, SSM_STATE), F32)],
        scratch_shapes=[pltpu.VMEM((SSM_UNIT, SSM_STATE), F32), pltpu.VMEM((SSM_UNIT, SSM_STATE), F32),
                        pltpu.VMEM((SSM_STREAMS, SSM_STATE), F32), pltpu.VMEM((SSM_STREAMS, SSM_STATE), F32),
                        pltpu.VMEM((1, SSM_STATE), F32), pltpu.VMEM((1, SSM_STATE), F32)],
        compiler_params=pltpu.CompilerParams(vmem_limit_bytes=VMEM_LIMIT,
                                             dimension_semantics=("arbitrary", "arbitrary")),
        name="ssm_scan",
    )(u, *operands)
    return _step_major(y, inverse=True), hf_r, hf_i


def _attn_kernel(*refs, key_lens):
    n_src = len(key_lens)
    q_ref = refs[0]
    k_refs = refs[1:1 + n_src]
    v_refs = refs[1 + n_src:1 + 2 * n_src]
    o_ref = refs[1 + 2 * n_src]
    s_ref = refs[2 + 2 * n_src]
    chunks = []
    off = 0
    for si, n in enumerate(key_lens):
        for c0 in range(0, n, ATTN_KC):
            cl = min(ATTN_KC, n - c0)
            chunks.append((si, c0, cl, off))
            off += cl
    tq = q_ref.shape[1]
    row_max = []
    n_heads = q_ref.shape[0]
    for hh in range(n_heads):
        q = q_ref[hh]
        mp = jnp.full((tq, LANES), -jnp.inf, F32)
        for si, c0, cl, o in chunks:
            s = lax.dot_general(q, k_refs[si][hh, c0:c0 + cl, :], (((1,), (1,)), ((), ())),
                                preferred_element_type=F32)
            s_ref[hh, :, o:o + cl] = s
            for g in range(cl // LANES):
                mp = jnp.maximum(mp, s[:, g * LANES:(g + 1) * LANES])
        row_max.append(jnp.max(mp, axis=-1, keepdims=True))
    outs = []
    for hh in range(n_heads):
        acc = jnp.zeros((tq, LANES), F32)
        for si, c0, cl, o in chunks:
            p = jnp.exp2(s_ref[hh, :, o:o + cl] - row_max[hh])
            acc = acc + _dot(p.astype(BF16), v_refs[si][hh, c0:c0 + cl, :])
        outs.append(acc / pltpu.roll(acc, MLA_V, 1))
    lane = lax.broadcasted_iota(jnp.int32, (tq, LANES), 1)
    for pair in range(n_heads // 2):
        o_ref[:, pair * LANES:(pair + 1) * LANES] = jnp.where(
            lane < MLA_V, outs[2 * pair], pltpu.roll(outs[2 * pair + 1], MLA_V, 1)).astype(o_ref.dtype)


def _attention(q, ks, vs, n_seq, seq_len, tok0, key_lens, key_tok0):
    tq = min(ATTN_TQ, seq_len)
    n_qt = seq_len // tq
    q0 = tok0 // tq
    in_specs = [pl.BlockSpec((ATTN_HEADS, tq, HEAD_PAD), lambda b, hp, qi: (hp, q0 + b * n_qt + qi, 0))]
    for n, t0 in zip(key_lens, key_tok0):
        in_specs.append(pl.BlockSpec((ATTN_HEADS, n, HEAD_PAD), lambda b, hp, qi, n=n, t0=t0: (hp, t0 // n + b, 0)))
    for n, t0 in zip(key_lens, key_tok0):
        in_specs.append(pl.BlockSpec((ATTN_HEADS, n, HEAD_PAD), lambda b, hp, qi, n=n, t0=t0: (hp, t0 // n + b, 0)))
    return pl.pallas_call(
        functools.partial(_attn_kernel, key_lens=tuple(key_lens)),
        grid=(n_seq, MLA_HEADS // ATTN_HEADS, n_qt),
        in_specs=in_specs,
        out_specs=pl.BlockSpec((tq, ATTN_HEADS * MLA_V), lambda b, hp, qi: (b * n_qt + qi, hp)),
        out_shape=jax.ShapeDtypeStruct((n_seq * seq_len, B_WIDTH), BF16),
        scratch_shapes=[pltpu.VMEM((ATTN_HEADS, tq, sum(key_lens)), F32)],
        compiler_params=pltpu.CompilerParams(vmem_limit_bytes=VMEM_LIMIT,
                                             dimension_semantics=("arbitrary", "arbitrary", "arbitrary")),
        name="attention",
    )(q, *ks, *vs)


def _merge_kernel(xp_ref, xs_ref, mod_ref, yssm_ref, u_ref, ybp_ref, ybs_ref, gm_ref, dskip_ref, w_glu_ref,
                  vn_ref, w_s_ref, bias_ref, g_ref, w_out_ref, x1_ref):
    t = xp_ref.shape[0]
    latent = pl.program_id(0) >= N_PROMPT // t
    lane = lax.broadcasted_iota(jnp.int32, (CHUNK, C_WIDTH), 1)
    g = g_ref[...]

    def body(x_ref, yb_ref):
        for sub in range(t // PROJ_SUB):
            rows = slice(sub * PROJ_SUB, (sub + 1) * PROJ_SUB)
            y = (yssm_ref[0, rows, :].astype(F32) + yssm_ref[1, rows, :].astype(F32)
                 + dskip_ref[...] * u_ref[rows, :])
            y = _gelu_tanh(y)
            y_a = y * _sigmoid(_dot(y.astype(BF16), w_glu_ref[...]))
            gm = gm_ref[rows, :].astype(F32)
            vn = (_rms(gm[:, C_WIDTH:]) * vn_ref[...]).astype(BF16)
            mixed = []
            for c in range(PROJ_SUB // CHUNK):
                vc = vn[c * CHUNK:(c + 1) * CHUNK, :]
                m = _dot(w_s_ref[0], vc)
                for h in range(1, GMLP_HEADS):
                    m = jnp.where(lane >= h * GMLP_CH, _dot(w_s_ref[h], vc), m)
                mixed.append(m + bias_ref[...])
            y_c = gm[:, :C_WIDTH] * jnp.concatenate(mixed, axis=0)
            n_a = (_rms(y_a) * g[:, :A_WIDTH]).astype(BF16)
            n_b = (_rms(yb_ref[rows, :].astype(F32)) * g[:, A_WIDTH:A_WIDTH + B_WIDTH]).astype(BF16)
            n_c = (_rms(y_c) * g[:, A_WIDTH + B_WIDTH:]).astype(BF16)
            o = (_dot(n_a, w_out_ref[0:A_WIDTH, :]) + _dot(n_b, w_out_ref[A_WIDTH:A_WIDTH + B_WIDTH, :])
                 + _dot(n_c, w_out_ref[A_WIDTH + B_WIDTH:, :]))
            x1_ref[rows, :] = x_ref[rows, :] + mod_ref[0, 2:3, :] * o

    pl.when(latent)(lambda: body(xs_ref, ybs_ref))
    pl.when(jnp.logical_not(latent))(lambda: body(xp_ref, ybp_ref))


def _merge(x, mod, yssm, u, yb_ctx, yb_lat, gm, w, l):
    t = PROJ_TILE
    tok = lambda width: pl.BlockSpec((t, width), lambda i: (i, 0))
    weights = [w[n] for n in ('ssm_d', 'w_glu', 'gmlp_v_norm', 'gmlp_w_s', 'gmlp_bias', 'w_out_norm', 'w_out')]
    xs, off = _split_args(x, t)
    return pl.pallas_call(
        _merge_kernel,
        grid=(N_TOK // t,),
        in_specs=[*_split_specs(t, D_MODEL, off),
                  pl.BlockSpec((None, 1, N_MOD, D_MODEL), lambda i: (l, _mod_row(i, t), 0, 0)),
                  pl.BlockSpec((N_DIR, t, A_WIDTH), lambda i: (0, i, 0)),
                  tok(A_WIDTH), *_split_specs(t, B_WIDTH, 0), tok(2 * C_WIDTH)]
                 + [_layer_spec(a, l) for a in weights],
        out_specs=tok(D_MODEL),
        out_shape=jax.ShapeDtypeStruct((N_TOK, D_MODEL), F32),
        compiler_params=pltpu.CompilerParams(vmem_limit_bytes=VMEM_LIMIT),
        name="merge",
    )(*xs, mod, yssm, u, yb_ctx, yb_lat, gm, *weights)


def _ffn_kernel(x1_ref, mod_ref, xp_ref, xn_ref, wup_ref, cw_ref, cb_ref, wd_ref, *rest):
    out_refs, a_scr = rest[:-1], rest[-1]

    def modulate(x):
        return _rms(x) * (1.0 + mod_ref[0, 4:5, :]) + mod_ref[0, 3:4, :]

    h_main = modulate(x1_ref[...]).astype(BF16)
    first_seg = pl.program_id(0) * FFN_SEGS - N_PROMPT // FFN_ROWS
    per_seq = DEC_SEQ // FFN_ROWS
    edge_row = lax.broadcasted_iota(jnp.int32, (FFN_HALO, D_MODEL), 0)
    tok_row = lax.broadcasted_iota(jnp.int32, (FFN_ROWS, FFN_FC), 0)

    def conv(u, c0):
        cols = slice(c0, c0 + FFN_FC)
        cur = u[:FFN_ROWS]
        prev = jnp.where(tok_row == 0, u[FFN_ROWS:FFN_ROWS + 1], pltpu.roll(cur, 1, 0))
        nxt = jnp.where(tok_row == FFN_ROWS - 1, u[FFN_ROWS + 1:FFN_ROWS + 2], pltpu.roll(cur, FFN_ROWS - 1, 0))
        return prev * cw_ref[0:1, cols] + cur * cw_ref[1:2, cols] + nxt * cw_ref[2:3, cols] + cb_ref[:, cols]

    for s in range(FFN_SEGS):
        seg = first_seg + s
        starts = (seg < 0) | ((seg & (per_seq - 1)) == 0)
        ends = (seg < 0) | ((seg & (per_seq - 1)) == per_seq - 1)
        rows = slice(s * FFN_ROWS, (s + 1) * FFN_ROWS)
        before = (xp_ref[FFN_HALO - 1:FFN_HALO, :] if s == 0
                  else x1_ref[s * FFN_ROWS - 1:s * FFN_ROWS, :])
        after = (xn_ref[0:1, :] if s == FFN_SEGS - 1
                 else x1_ref[(s + 1) * FFN_ROWS:(s + 1) * FFN_ROWS + 1, :])
        prev_row = modulate(before) * jnp.where(starts, 0.0, 1.0)
        next_row = modulate(after) * jnp.where(ends, 0.0, 1.0)
        edge = jnp.where(edge_row == 0, prev_row, jnp.where(edge_row == 1, next_row, 0.0)).astype(BF16)
        he = jnp.concatenate([h_main[rows, :], edge], axis=0)
        for j in range(D_FF // FFN_FC):
            cg = j * FFN_FC
            cv = D_FF + j * FFN_FC
            gate = conv(_dot(he, wup_ref[:, cg:cg + FFN_FC]), cg)
            val = conv(_dot(he, wup_ref[:, cv:cv + FFN_FC]), cv)
            a_scr[rows, cg:cg + FFN_FC] = (gate * _sigmoid(gate) * val).astype(BF16)

    out = x1_ref[...] + mod_ref[0, 5:6, :] * _dot(a_scr[...], wd_ref[...])
    if len(out_refs) == 1:
        out_refs[0][...] = out
    else:
        @pl.when(first_seg < 0)
        def _():
            out_refs[0][...] = out

        @pl.when(first_seg >= 0)
        def _():
            out_refs[1][...] = out


def _ffn(x1, mod, w, l, split_out):
    t = FFN_ROWS * FFN_SEGS
    hb = t // FFN_HALO
    last_halo = N_TOK // FFN_HALO - 1
    weights = [w[n] for n in ('ffn_w_up', 'ffn_conv_w', 'ffn_conv_b', 'ffn_w_down')]
    if split_out:
        out_specs = list(_split_specs(t, D_MODEL, 0))
        out_shape = [jax.ShapeDtypeStruct((N_PROMPT, D_MODEL), F32), jax.ShapeDtypeStruct((N_SAMPLE, D_MODEL), F32)]
    else:
        out_specs = pl.BlockSpec((t, D_MODEL), lambda i: (i, 0))
        out_shape = jax.ShapeDtypeStruct((N_TOK, D_MODEL), F32)
    return pl.pallas_call(
        _ffn_kernel,
        grid=(N_TOK // t,),
        in_specs=[pl.BlockSpec((t, D_MODEL), lambda i: (i, 0)),
                  pl.BlockSpec((None, 1, N_MOD, D_MODEL), lambda i: (l, _mod_row(i, t), 0, 0)),
                  pl.BlockSpec((FFN_HALO, D_MODEL), lambda i: (jnp.maximum(i * hb - 1, 0), 0)),
                  pl.BlockSpec((FFN_HALO, D_MODEL), lambda i: (jnp.minimum((i + 1) * hb, last_halo), 0))]
                 + [_layer_spec(a, l, pipeline_mode=pl.Buffered(1)) for a in weights],
        out_specs=out_specs,
        out_shape=out_shape,
        scratch_shapes=[pltpu.VMEM((t, D_FF), BF16)],
        compiler_params=pltpu.CompilerParams(vmem_limit_bytes=VMEM_LIMIT,
                                             dimension_semantics=("arbitrary",)),
        name="conv_ffn",
    )(x1, mod, x1, x1, *weights)


def _pad_heads(w, head_w, used, offset=0):
    lead = w.shape[:-1]
    w = w.reshape(lead + (MLA_HEADS, head_w))[..., :used]
    pad = [(0, 0)] * (len(lead) + 1) + [(offset, HEAD_PAD - used - offset)]
    return jnp.pad(w, pad).reshape(lead + (MLA_HEADS * HEAD_PAD,))


def _head_gain(g):
    return jnp.pad(g, ((0, 0), (0, HEAD_PAD - MLA_QK)))[:, None, :]


def _ssm_tables(a_re, a_im, b_re, b_im, c_re, c_im, log_dt):
    a = lax.complex(a_re, a_im)
    dt = jnp.exp(log_dt)[..., None]
    a_bar = jnp.exp(a * dt)
    b_bar = ((a_bar - 1.0) / a)[..., None] * lax.complex(b_re, b_im)
    steps = jnp.arange(1, SSM_STEPS + 1, dtype=F32)[None, None, :, None, None]
    pw = jnp.exp((a * dt)[:, :, None] * steps)
    eye = jnp.eye(SSM_G, dtype=F32)
    blk_b = lambda m: jnp.einsum('ldgpc,gh->ldgchp', m, eye).reshape(DEPTH, N_DIR, A_WIDTH, SSM_STATE)
    blk_c = lambda m: jnp.einsum('ldgcp,gh->ldgphc', m, eye).reshape(DEPTH, N_DIR, SSM_STATE, A_WIDTH)
    flat = lambda m: m.reshape(DEPTH, N_DIR, 1, SSM_STATE)
    return {
        'a_r': flat(a_bar.real), 'a_i': flat(a_bar.imag),
        'pw_r': pw.real.reshape(DEPTH, N_DIR, SSM_STEPS, SSM_STATE),
        'pw_i': pw.imag.reshape(DEPTH, N_DIR, SSM_STEPS, SSM_STATE),
        'b_blk': jnp.concatenate([blk_b(b_bar.real), blk_b(b_bar.imag)], axis=-1).astype(BF16),
        'c_r': blk_c(c_re).astype(BF16),
        'c_i': blk_c(-c_im).astype(BF16),
    }


def _rope_tables():
    rows = DEC_SEQ // GRID_W
    row = np.repeat(np.arange(rows, dtype=np.float32), GRID_W)
    col = np.tile(np.arange(GRID_W, dtype=np.float32), rows)
    n_freq = MLA_ROPE // 4
    inv = (np.float32(ROPE_BASE) ** (-np.arange(n_freq, dtype=np.float32) / np.float32(n_freq))).astype(np.float32)
    ang = np.concatenate([row[:, None] * inv, col[:, None] * inv], axis=-1).astype(np.float32)
    cos, sin = np.cos(ang), np.sin(ang)
    cos_t = np.ones((DEC_SEQ, HEAD_PAD), np.float32)
    sin_t = np.zeros((DEC_SEQ, HEAD_PAD), np.float32)
    cos_t[:, ROPE_LANE0:ROPE_LANE0 + MLA_ROPE] = np.concatenate([cos, cos], axis=-1)
    sin_t[:, ROPE_LANE0:ROPE_LANE0 + MLA_ROPE] = np.concatenate([sin, sin], axis=-1)
    return jnp.asarray(cos_t), jnp.asarray(sin_t)


def _rope_partner(r):
    half = MLA_ROPE // 2
    return jnp.concatenate([-r[..., half:], r[..., :half]], axis=-1)


def _rope_lanes(r):
    return jnp.pad(r, [(0, 0)] * (r.ndim - 1) + [(ROPE_LANE0, HEAD_PAD - ROPE_LANE0 - MLA_ROPE)])


def _head_gain_partner(g):
    half = MLA_ROPE // 2
    r = g[:, MLA_NOPE:]
    return _rope_lanes(jnp.concatenate([r[:, half:], r[:, :half]], axis=-1))[:, None, :]


def _prepare_weights(w_in, ssm_d, ssm_w_glu, q_a_norm, kv_a_norm, w_uq, w_ukv, q_norm, k_norm,
                     gmlp_v_norm, gmlp_w_s, gmlp_b_s, w_out_norm, w_out, ffn_w_up, ffn_conv_w,
                     ffn_conv_b, ffn_w_down):
    w_kr = w_in[:, :, OFF_KR:OFF_GM]
    ukv = w_ukv.reshape(DEPTH, KV_RANK, MLA_HEADS, MLA_NOPE + MLA_V)
    uq = w_uq.reshape(DEPTH, Q_RANK, MLA_HEADS, MLA_QK)
    q_gain = q_norm * Q_SCALE
    row = lambda a: a[:, None, :]
    return {
        'w_in_a': jnp.concatenate([w_in[:, :, OFF_SSM:OFF_KR], w_in[:, :, OFF_GM:]], axis=2).astype(BF16),
        'w_in_kr': _rope_lanes(w_kr).astype(BF16),
        'w_in_krp': _rope_lanes(_rope_partner(w_kr)).astype(BF16),
        'w_uqp': _rope_lanes(_rope_partner(uq[..., MLA_NOPE:])).reshape(DEPTH, Q_RANK, -1).astype(BF16),
        'q_a_norm': row(q_a_norm), 'kv_a_norm': row(kv_a_norm),
        'w_uq': _pad_heads(w_uq, MLA_QK, MLA_QK).astype(BF16),
        'w_uk': _pad_heads(ukv[..., :MLA_NOPE].reshape(DEPTH, KV_RANK, -1), MLA_NOPE, MLA_NOPE).astype(BF16),
        'w_uv': _pad_heads(ukv[..., MLA_NOPE:].reshape(DEPTH, KV_RANK, -1), MLA_V, MLA_V).astype(BF16),
        'q_norm': _head_gain(q_gain), 'q_norm_p': _head_gain_partner(q_gain),
        'k_norm': _head_gain(k_norm), 'k_norm_p': _head_gain_partner(k_norm),
        'ssm_d': row(ssm_d), 'w_glu': ssm_w_glu.astype(BF16),
        'gmlp_v_norm': row(gmlp_v_norm), 'gmlp_w_s': gmlp_w_s.astype(BF16),
        'gmlp_bias': jnp.repeat(jnp.swapaxes(gmlp_b_s, 1, 2), GMLP_CH, axis=2),
        'w_out_norm': row(w_out_norm), 'w_out': w_out.astype(BF16),
        'ffn_w_up': ffn_w_up.astype(BF16), 'ffn_conv_w': ffn_conv_w,
        'ffn_conv_b': row(ffn_conv_b), 'ffn_w_down': ffn_w_down.astype(BF16),
    }


def kernel(x_prompt, x_sample, cache_ckv, cache_krope, state_ssm_re, state_ssm_im, c, c_ctx, w_mod, b_mod, w_in, ssm_a_re, ssm_a_im, ssm_b_re, ssm_b_im, ssm_c_re, ssm_c_im, ssm_log_dt, ssm_d, ssm_w_glu, q_a_norm, kv_a_norm, w_uq, w_ukv, q_norm, k_norm, gmlp_v_norm, gmlp_w_s, gmlp_b_s, w_out_norm, w_out, ffn_w_up, ffn_conv_w, ffn_conv_b, ffn_w_down):
    cond8 = jnp.concatenate([c_ctx[None, :], c, jnp.zeros((8 - 1 - DEC_BATCH, D_MODEL), F32)], axis=0)
    mod = _modulation(cond8, w_mod, b_mod).reshape(DEPTH, 8, N_MOD, D_MODEL)
    rope_tabs = _rope_tables()
    w = _prepare_weights(w_in, ssm_d, ssm_w_glu, q_a_norm, kv_a_norm, w_uq, w_ukv, q_norm, k_norm,
                         gmlp_v_norm, gmlp_w_s, gmlp_b_s, w_out_norm, w_out, ffn_w_up, ffn_conv_w,
                         ffn_conv_b, ffn_w_down)
    tabs = _ssm_tables(ssm_a_re, ssm_a_im, ssm_b_re, ssm_b_im, ssm_c_re, ssm_c_im, ssm_log_dt)
    cache_kr = _rope_lanes(cache_krope)

    def h0(st):
        st = jnp.transpose(st.reshape(DEC_BATCH, DEPTH, N_DIR, SSM_STATE), (1, 2, 0, 3))
        return jnp.pad(st, ((0, 0), (0, 0), (0, SSM_STREAMS - DEC_BATCH), (0, 0)))

    h0r, h0i = h0(state_ssm_re), h0(state_ssm_im)

    x = (x_prompt.reshape(N_PROMPT, D_MODEL), x_sample.reshape(N_SAMPLE, D_MODEL))
    ckv_out, kr_out, hre_out, him_out = [], [], [], []
    for l in range(DEPTH):
        u, gm, ckv, kr, q, k, v = _projection(x, mod, rope_tabs, w, l)
        k_c, v_c = _ctx_kv(cache_ckv, cache_kr, w, l)
        yb_ctx = _attention(q, [k], [v], BATCH, SEQ, 0, [SEQ], [0])
        yssm, hf_r, hf_i = _ssm(u, h0r, h0i, tabs, l)
        yb_lat = _attention(q, [k_c, k], [v_c, v], DEC_BATCH, DEC_SEQ, N_PROMPT,
                            [PAST_LEN, DEC_SEQ], [0, N_PROMPT])
        x1 = _merge(x, mod, yssm, u, yb_ctx, yb_lat, gm, w, l)
        x = _ffn(x1, mod, w, l, split_out=(l == DEPTH - 1))

        ckv_out.append(ckv.reshape(BATCH, SEQ, KV_RANK))
        kr_out.append(kr[:, ROPE_LANE0:ROPE_LANE0 + MLA_ROPE].reshape(BATCH, SEQ, MLA_ROPE))
        to_state = lambda st: jnp.transpose(st, (1, 0, 2)).reshape(BATCH, N_DIR, SSM_G, SSM_P)
        hre_out.append(to_state(hf_r))
        him_out.append(to_state(hf_i))

    y_prompt = x[0].reshape(BATCH, SEQ, D_MODEL)
    y_sample = x[1].reshape(DEC_BATCH, DEC_SEQ, D_MODEL)
    return (y_prompt, y_sample, jnp.stack(ckv_out, axis=1), jnp.stack(kr_out, axis=1),
            jnp.stack(hre_out, axis=1), jnp.stack(him_out, axis=1))
```
